```python
import math
import jax, jax.numpy as jnp
from jax import lax
import numpy as np

D_MODEL = 2048
BATCH = 4
SEQ = 4096
DEPTH = 2

MIX_WIDTH = D_MODEL
HY_WIDTH = MIX_WIDTH // 2
HY_ORDER = 2
HY_DIRS = 2
HY_EMB = 33
HY_BANDS = (HY_EMB - 1) // 2
HY_FFN = 64
HY_RATE_MIN = 3.07
HY_RATE_MAX = 15.35
SHORT_CONV = 3
GLA_WIDTH = MIX_WIDTH - HY_WIDTH
GLA_HEADS = 4
GLA_DK = GLA_WIDTH // (2 * GLA_HEADS)
GLA_DV = GLA_WIDTH // GLA_HEADS
GLA_RANK = 16
GLA_TAU = 16.0
GLA_CHUNK = 64
RW_HEAD = 64
RW_HEADS = D_MODEL // RW_HEAD
RW_DECAY_LORA = 96
RW_ICLR_LORA = 96
RW_GATE_LORA = 256
RW_LOG_DECAY_MAX = 0.606531
RW_GN_EPS = 64e-5
D_FF = 4 * D_MODEL
NORM_EPS = 1e-6
N_EVEN = (DEPTH + 1) // 2
N_ODD = DEPTH // 2
PROJ_SPLITS = (3 * HY_WIDTH, GLA_HEADS * GLA_DK, GLA_HEADS * GLA_DK, GLA_WIDTH, GLA_WIDTH, GLA_RANK, GLA_RANK)
IN_WIDTH = sum(PROJ_SPLITS)

kernel_name = 'hybrid_hyena_gla_rwkv7_encoder'


def rmsnorm(x, g, eps=NORM_EPS):
    xf = x.astype(jnp.float32)
    y = xf * lax.rsqrt(jnp.mean(xf * xf, axis=-1, keepdims=True) + eps)
    return (y * g.astype(jnp.float32)).astype(x.dtype)


def adaln(x, c, norm_g, ada_w, ada_b):
    m = jax.nn.silu(c) @ ada_w + ada_b
    shift, scale, gate = jnp.split(m[:, None, :], 3, axis=-1)
    return rmsnorm(x, norm_g) * (1.0 + scale) + shift, gate


def short_conv_centred(u, w, b):
    pad = SHORT_CONV // 2
    L = u.shape[1]
    up = jnp.pad(u, ((0, 0), (pad, pad), (0, 0)))
    return sum(up[:, j:j + L] * w[j] for j in range(SHORT_CONV)) + b


def hyena_filter_spectra(L, w1, b1, freq1, w2, b2, freq2, w3, decay):
    t = jnp.linspace(0.0, 1.0, L, dtype=jnp.float32)[:, None]
    ang = (2.0 * math.pi / L) * jnp.arange(L, dtype=jnp.float32)[:, None]
    bands = jnp.linspace(1e-4, HY_BANDS - 1, HY_BANDS, dtype=jnp.float32)[None, :]
    z = jnp.concatenate([t, jnp.cos(bands * ang), -jnp.sin(bands * ang)], axis=-1)
    f = jnp.sin(freq1 * (z @ w1 + b1))
    f = jnp.sin(freq2 * (f @ w2 + b2))
    f = (f @ w3).astype(jnp.float32) * jnp.exp(-t * jnp.abs(decay.astype(jnp.float32)))
    f = f.reshape(L, HY_ORDER, HY_DIRS, HY_WIDTH)
    fwd, bwd = f[:, :, 0], f[:, :, 1]
    k_circ = jnp.concatenate([fwd, jnp.zeros_like(fwd[:1]), bwd[:0:-1]], axis=0)
    return jnp.fft.rfft(k_circ, axis=0)


def fft_long_conv(u, k_f, skip):
    L = u.shape[1]
    uf = u.astype(jnp.float32)
    y = jnp.fft.irfft(jnp.fft.rfft(uf, n=2 * L, axis=1) * k_f, n=2 * L, axis=1)[:, :L]
    return (y + uf * skip.astype(jnp.float32)).astype(u.dtype)


def hyena_mixer(u_proj, conv_w, conv_b, f_w1, f_b1, f_freq1, f_w2, f_b2, f_freq2, f_w3, f_decay, skip):
    u = short_conv_centred(u_proj, conv_w, conv_b)
    v, x1, x2 = jnp.split(u, 3, axis=-1)
    k_f = hyena_filter_spectra(u.shape[1], f_w1, f_b1, f_freq1, f_w2, f_b2, f_freq2, f_w3, f_decay)
    z = x1 * fft_long_conv(v, k_f[:, 0], skip[0])
    return x2 * fft_long_conv(z, k_f[:, 1], skip[1])


def gla_chunked_scan(q, k, v, g):
    B, L, H, dk = q.shape
    dv = v.shape[-1]
    n_chunks = L // GLA_CHUNK

    def to_chunks(a):
        return a.reshape(B, n_chunks, GLA_CHUNK, H, a.shape[-1]).transpose(1, 0, 3, 2, 4)

    qc, kc, vc, gc = to_chunks(q), to_chunks(k), to_chunks(v), to_chunks(g)
    vc = vc.astype(jnp.float32)
    b = jnp.cumsum(gc.astype(jnp.float32), axis=3)
    b_last = b[:, :, :, -1:]
    q_in = qc * jnp.exp(b)
    k_in = kc * jnp.exp(-b)
    k_end = kc * jnp.exp(b_last - b)
    mask = jnp.tril(jnp.ones((GLA_CHUNK, GLA_CHUNK), dtype=bool))
    att = jnp.where(mask, jnp.einsum('nbhid,nbhjd->nbhij', q_in, k_in), 0.0)
    o_intra = jnp.einsum('nbhij,nbhjv->nbhiv', att, vc)

    def step(state, inp):
        q_n, k_n, v_n, dec_n = inp
        o_n = jnp.einsum('bhid,bhdv->bhiv', q_n, state)
        state = state * dec_n[..., None] + jnp.einsum('bhjd,bhjv->bhdv', k_n, v_n)
        return state, o_n

    s0 = jnp.zeros((B, H, dk, dv), jnp.float32)
    _, o_inter = lax.scan(step, s0, (q_in, k_end, vc, jnp.exp(b_last[:, :, :, 0])))
    o = o_intra + o_inter
    return o.transpose(1, 0, 3, 2, 4).reshape(B, L, H, dv).astype(v.dtype)


def gla_mixer(q, k, v, r, lr_f, lr_b, up, up_b, norm_g):
    B, L, _ = q.shape
    q = q.reshape(B, L, GLA_HEADS, GLA_DK) * GLA_DK ** -0.5
    k = k.reshape(B, L, GLA_HEADS, GLA_DK)
    v = v.reshape(B, L, GLA_HEADS, GLA_DV)

    def log_gate(lr, d):
        a = jax.nn.log_sigmoid((lr @ up[d] + up_b[d]).astype(jnp.float32)) / GLA_TAU
        return a.reshape(B, L, GLA_HEADS, GLA_DK)

    flip = lambda a: jnp.flip(a, axis=1)
    o_f = gla_chunked_scan(q, k, v, log_gate(lr_f, 0))
    o_b = flip(gla_chunked_scan(flip(q), flip(k), flip(v), flip(log_gate(lr_b, 1))))
    o = rmsnorm(o_f + o_b, norm_g.reshape(GLA_HEADS, GLA_DV))
    return o.reshape(B, L, GLA_WIDTH) * jax.nn.silu(r)


def hyena_gla_block(h, w_in, w_out, conv_w, conv_b, f_w1, f_b1, f_freq1, f_w2, f_b2, f_freq2, f_w3,
                    f_decay, hy_skip, gla_up, gla_up_b, gla_norm_g):
    p = h @ w_in
    cuts = [int(s) for s in np.cumsum(PROJ_SPLITS)[:-1]]
    hy_in, q, k, v, r, lr_f, lr_b = jnp.split(p, cuts, axis=-1)
    y_hy = hyena_mixer(hy_in, conv_w, conv_b, f_w1, f_b1, f_freq1, f_w2, f_b2, f_freq2, f_w3, f_decay, hy_skip)
    y_gla = gla_mixer(q, k, v, r, lr_f, lr_b, gla_up, gla_up_b, gla_norm_g)
    return jnp.concatenate([y_hy, y_gla.astype(y_hy.dtype)], axis=-1) @ w_out


def rwkv7_scan(r, logw, k, v, kk, a):
    B, L, H, N = r.shape
    tm = lambda a: jnp.moveaxis(a.astype(jnp.float32), 1, 0)

    def step(state, inp):
        r_t, w_t, k_t, v_t, kk_t, a_t = inp
        sa = jnp.einsum('bhvk,bhk->bhv', state, -kk_t)
        state = (state * w_t[:, :, None, :] + sa[..., None] * (kk_t * a_t)[:, :, None, :]
                 + v_t[..., None] * k_t[:, :, None, :])
        return state, jnp.einsum('bhvk,bhk->bhv', state, r_t)

    s0 = jnp.zeros((B, H, N, N), jnp.float32)
    _, o = lax.scan(step, s0, (tm(r), tm(jnp.exp(logw)), tm(k), tm(v), tm(kk), tm(a)))
    return jnp.moveaxis(o, 0, 1)


def rwkv7_block(h, mu, w_r, w_k, w_v, w0, w1, w2, a0, a1, a2, g1, g2, k_k, k_a, r_k, gn_g, gn_b, w_o):
    B, L, D = h.shape
    heads = lambda a: a.reshape(a.shape[:-1] + (RW_HEADS, RW_HEAD))
    hp = jnp.pad(h, ((0, 0), (1, 1), (0, 0)))
    xx = 0.5 * (hp[:, :-2] + hp[:, 2:]) - h
    xr, xw, xk, xv, xa, xg = [h + xx * mu[j] for j in range(6)]
    r = heads(xr @ w_r).astype(jnp.float32)
    k = heads(xk @ w_k).astype(jnp.float32)
    v = heads(xv @ w_v).astype(jnp.float32)
    g = jax.nn.sigmoid(xg @ g1) @ g2
    kk = k * heads(k_k).astype(jnp.float32)
    kk = kk * lax.rsqrt(jnp.maximum(jnp.sum(kk * kk, axis=-1, keepdims=True), 1e-24))
    flip = lambda a: jnp.flip(a, axis=1)
    state_out = jnp.zeros((B, L, RW_HEADS, RW_HEAD), jnp.float32)
    bonus = jnp.zeros((B, L, RW_HEADS, RW_HEAD), jnp.float32)
    for d in range(2):
        logw = heads(-RW_LOG_DECAY_MAX * jax.nn.sigmoid((w0[d] + jnp.tanh(xw @ w1[d]) @ w2[d]).astype(jnp.float32)))
        a = heads(jax.nn.sigmoid((a0[d] + (xa @ a1[d]) @ a2[d]).astype(jnp.float32)))
        k_d = k * (1.0 + (a - 1.0) * heads(k_a).astype(jnp.float32))
        seqs = (r, logw, k_d, v, kk, a)
        if d == 0:
            o_d = rwkv7_scan(*seqs)
        else:
            o_d = flip(rwkv7_scan(*[flip(s) for s in seqs]))
        state_out = state_out + o_d
        bonus = bonus + jnp.sum(r * k_d * r_k.astype(jnp.float32), axis=-1, keepdims=True) * v
    mean = jnp.mean(state_out, axis=-1, keepdims=True)
    var = jnp.mean(jnp.square(state_out - mean), axis=-1, keepdims=True)
    o = ((state_out - mean) * lax.rsqrt(var + RW_GN_EPS) * heads(gn_g).astype(jnp.float32)
         + heads(gn_b).astype(jnp.float32))
    o = (o + bonus).reshape(B, L, D).astype(h.dtype) * g
    return o @ w_o


def sqrelu_mlp(h, w1, w2):
    return jnp.square(jax.nn.relu(h @ w1)) @ w2


def setup_inputs(seed: int = 0) -> dict:
    key = jax.random.key(seed)
    keys = iter(jax.random.split(key, 64))

    def nrm(shape, scale):
        return scale * jax.random.normal(next(keys), shape, jnp.float32)

    def uni(shape, lo, hi):
        return jax.random.uniform(next(keys), shape, jnp.float32, lo, hi)

    def gain(shape):
        return 1.0 + nrm(shape, 0.02)

    D, NE, NO = D_MODEL, N_EVEN, N_ODD
    HYC = HY_ORDER * HY_DIRS * HY_WIDTH
    return {
        'x': nrm((BATCH, SEQ, D), 1.0),
        'c': nrm((BATCH, D), 1.0),
        'mix_norm_g': gain((NE, D)),
        'mix_ada_w': nrm((NE, D, 3 * D), 0.5 * D ** -0.5),
        'mix_ada_b': nrm((NE, 3 * D), 0.02),
        'ab_w_in': nrm((NE, D, IN_WIDTH), D ** -0.5),
        'ab_w_out': nrm((NE, MIX_WIDTH, D), MIX_WIDTH ** -0.5),
        'hy_conv_w': nrm((NE, SHORT_CONV, 3 * HY_WIDTH), SHORT_CONV ** -0.5),
        'hy_conv_b': nrm((NE, 3 * HY_WIDTH), 0.02),
        'hy_ffn_w1': nrm((NE, HY_EMB, HY_FFN), HY_EMB ** -0.5),
        'hy_ffn_b1': nrm((NE, HY_FFN), 0.1),
        'hy_freq1': 1.0 + nrm((NE, HY_FFN), 0.1),
        'hy_ffn_w2': nrm((NE, HY_FFN, HY_FFN), HY_FFN ** -0.5),
        'hy_ffn_b2': nrm((NE, HY_FFN), 0.1),
        'hy_freq2': 1.0 + nrm((NE, HY_FFN), 0.1),
        'hy_ffn_w3': nrm((NE, HY_FFN, HYC), 0.05 * HY_FFN ** -0.5),
        'hy_decay': uni((NE, HYC), HY_RATE_MIN, HY_RATE_MAX),
        'hy_skip': nrm((NE, HY_ORDER, HY_WIDTH), 0.5),
        'gla_up': nrm((NE, 2, GLA_RANK, GLA_HEADS * GLA_DK), GLA_RANK ** -0.5),
        'gla_up_b': uni((NE, 2, GLA_HEADS * GLA_DK), 0.0, 3.0),
        'gla_norm_g': gain((NE, GLA_WIDTH)),
        'tm_norm_g': gain((NO, D)),
        'tm_ada_w': nrm((NO, D, 3 * D), 0.5 * D ** -0.5),
        'tm_ada_b': nrm((NO, 3 * D), 0.02),
        'rw_mu': uni((NO, 6, D), 0.0, 1.0),
        'rw_w_r': nrm((NO, D, D), D ** -0.5),
        'rw_w_k': nrm((NO, D, D), D ** -0.5),
        'rw_w_v': nrm((NO, D, D), D ** -0.5),
        'rw_w0': uni((NO, 2, D), -3.0, 1.0),
        'rw_w1': nrm((NO, 2, D, RW_DECAY_LORA), D ** -0.5),
        'rw_w2': nrm((NO, 2, RW_DECAY_LORA, D), RW_DECAY_LORA ** -0.5),
        'rw_a0': nrm((NO, 2, D), 0.5),
        'rw_a1': nrm((NO, 2, D, RW_ICLR_LORA), D ** -0.5),
        'rw_a2': nrm((NO, 2, RW_ICLR_LORA, D), RW_ICLR_LORA ** -0.5),
        'rw_g1': nrm((NO, D, RW_GATE_LORA), D ** -0.5),
        'rw_g2': nrm((NO, RW_GATE_LORA, D), RW_GATE_LORA ** -0.5),
        'rw_k_k': 0.85 + nrm((NO, D), 0.05),
        'rw_k_a': 1.0 + nrm((NO, D), 0.05),
        'rw_r_k': nrm((NO, RW_HEADS, RW_HEAD), 0.1),
        'rw_gn_g': gain((NO, D)),
        'rw_gn_b': nrm((NO, D), 0.02),
        'rw_w_o': nrm((NO, D, D), D ** -0.5),
        'ffn_norm_g': gain((DEPTH, D)),
        'ffn_ada_w': nrm((DEPTH, D, 3 * D), 0.5 * D ** -0.5),
        'ffn_ada_b': nrm((DEPTH, 3 * D), 0.02),
        'ffn_w1': nrm((DEPTH, D, D_FF), D ** -0.5),
        'ffn_w2': nrm((DEPTH, D_FF, D), D_FF ** -0.5),
        'final_norm_g': gain((D,)),
    }


def reference(x, c, mix_norm_g, mix_ada_w, mix_ada_b, ab_w_in, ab_w_out, hy_conv_w, hy_conv_b,
              hy_ffn_w1, hy_ffn_b1, hy_freq1, hy_ffn_w2, hy_ffn_b2, hy_freq2, hy_ffn_w3, hy_decay, hy_skip,
              gla_up, gla_up_b, gla_norm_g, tm_norm_g, tm_ada_w, tm_ada_b, rw_mu, rw_w_r, rw_w_k, rw_w_v,
              rw_w0, rw_w1, rw_w2, rw_a0, rw_a1, rw_a2, rw_g1, rw_g2, rw_k_k, rw_k_a, rw_r_k, rw_gn_g, rw_gn_b,
              rw_w_o, ffn_norm_g, ffn_ada_w, ffn_ada_b, ffn_w1, ffn_w2, final_norm_g):
    for layer in range(DEPTH):
        i = layer // 2
        if layer % 2 == 0:
            h, gate = adaln(x, c, mix_norm_g[i], mix_ada_w[i], mix_ada_b[i])
            y = hyena_gla_block(h, ab_w_in[i], ab_w_out[i], hy_conv_w[i], hy_conv_b[i],
                                hy_ffn_w1[i], hy_ffn_b1[i], hy_freq1[i], hy_ffn_w2[i], hy_ffn_b2[i],
                                hy_freq2[i], hy_ffn_w3[i], hy_decay[i], hy_skip[i],
                                gla_up[i], gla_up_b[i], gla_norm_g[i])
        else:
            h, gate = adaln(x, c, tm_norm_g[i], tm_ada_w[i], tm_ada_b[i])
            y = rwkv7_block(h, rw_mu[i], rw_w_r[i], rw_w_k[i], rw_w_v[i], rw_w0[i], rw_w1[i], rw_w2[i],
                            rw_a0[i], rw_a1[i], rw_a2[i], rw_g1[i], rw_g2[i], rw_k_k[i], rw_k_a[i],
                            rw_r_k[i], rw_gn_g[i], rw_gn_b[i], rw_w_o[i])
        x = x + gate * y
        h, gate = adaln(x, c, ffn_norm_g[layer], ffn_ada_w[layer], ffn_ada_b[layer])
        x = x + gate * sqrelu_mlp(h, ffn_w1[layer], ffn_w2[layer])
    return rmsnorm(x, final_norm_g)
```

```python
import functools
import math

import numpy as np
import jax
import jax.numpy as jnp
from jax import lax
from jax.experimental import pallas as pl
from jax.experimental.pallas import tpu as pltpu

F32 = jnp.float32
BF16 = jnp.bfloat16
HI = lax.Precision.HIGHEST

LANES = 128
NORM_EPS = 1e-6
GLA_HEADS = 4
GLA_DK = 128
GLA_DV = 256
GLA_RANK = 16
GLA_TAU = 16.0
GLA_CHUNK = 64
RW_HEAD = 64
RW_CHUNK = 64
RW_LOG_DECAY_MAX = 0.606531
RW_GN_EPS = 64e-5
HY_ORDER = 2
HY_EMB = 33
HY_FFN = 64
DFT_N2 = 128
VMEM_LIMIT = 52 * 1024 * 1024


def _cparams(*sem):
    return pltpu.CompilerParams(dimension_semantics=sem, vmem_limit_bytes=VMEM_LIMIT)


def _dot(a, b, precision=None):
    return jnp.dot(a, b, preferred_element_type=F32, precision=precision)


def _dot_nt(a, b, precision=None):
    return lax.dot_general(a, b, (((1,), (1,)), ((), ())), preferred_element_type=F32, precision=precision)


def _dot_tn(a, b, precision=None):
    return lax.dot_general(a, b, (((0,), (0,)), ((), ())), preferred_element_type=F32, precision=precision)


def _iota2(shape, axis):
    return lax.broadcasted_iota(jnp.int32, shape, axis)


def _log_sigmoid(x):
    return jnp.minimum(x, 0.0) - jnp.log(1.0 + jnp.exp(-jnp.abs(x)))


def _ada_body(c_ref, w_ref, b_ref, o_ref):
    c = c_ref[...]
    s = c * jax.nn.sigmoid(c)
    o_ref[...] = _dot(s, w_ref[0], HI) + b_ref[0]


def ada_mod(c, w, b, idx):
    B, D = c.shape
    n3 = w.shape[2]
    tn = 512
    out = pl.pallas_call(
        _ada_body,
        grid=(n3 // tn,),
        in_specs=[pl.BlockSpec((B, D), lambda j: (0, 0)),
                  pl.BlockSpec((1, D, tn), lambda j: (idx, 0, j)),
                  pl.BlockSpec((1, 1, tn), lambda j: (idx, 0, j))],
        out_specs=pl.BlockSpec((B, tn), lambda j: (0, j)),
        out_shape=jax.ShapeDtypeStruct((B, n3), F32),
        compiler_params=_cparams("parallel"),
        name="ada_mod",
    )(c, w, b.reshape(b.shape[0], 1, n3))
    shift, scale, gate = jnp.split(out[:, None, :], 3, axis=-1)
    return shift, scale, gate


def _norm_mod_value(x, g, scale, shift):
    y = x * lax.rsqrt(jnp.mean(x * x, axis=-1, keepdims=True) + NORM_EPS)
    return (y * g) * (1.0 + scale) + shift


def _norm_mod_body(x_ref, g_ref, sc_ref, sh_ref, o_ref):
    o_ref[0] = _norm_mod_value(x_ref[0], g_ref[...], sc_ref[0], sh_ref[0]).astype(o_ref.dtype)


def norm_mod(x, g, scale, shift, out_dtype=BF16, tm=512):
    B, L, D = x.shape
    return pl.pallas_call(
        _norm_mod_body,
        grid=(B, L // tm),
        in_specs=[pl.BlockSpec((1, tm, D), lambda b, i: (b, i, 0)),
                  pl.BlockSpec((1, D), lambda b, i: (0, 0)),
                  pl.BlockSpec((1, 1, D), lambda b, i: (b, 0, 0)),
                  pl.BlockSpec((1, 1, D), lambda b, i: (b, 0, 0))],
        out_specs=pl.BlockSpec((1, tm, D), lambda b, i: (b, i, 0)),
        out_shape=jax.ShapeDtypeStruct((B, L, D), out_dtype),
        compiler_params=_cparams("parallel", "parallel"),
        name="norm_mod",
    )(x, g.reshape(1, D), scale, shift)


def _rms_body(x_ref, g_ref, o_ref):
    x = x_ref[0]
    y = x * lax.rsqrt(jnp.mean(x * x, axis=-1, keepdims=True) + NORM_EPS)
    o_ref[0] = y * g_ref[...]


def rms_final(x, g, tm=512):
    B, L, D = x.shape
    return pl.pallas_call(
        _rms_body,
        grid=(B, L // tm),
        in_specs=[pl.BlockSpec((1, tm, D), lambda b, i: (b, i, 0)),
                  pl.BlockSpec((1, D), lambda b, i: (0, 0))],
        out_specs=pl.BlockSpec((1, tm, D), lambda b, i: (b, i, 0)),
        out_shape=jax.ShapeDtypeStruct((B, L, D), F32),
        compiler_params=_cparams("parallel", "parallel"),
        name="rms_final",
    )(x, g.reshape(1, D))


def _mm_body(*refs, nk, a_act, epi, has_bias, has_res):
    it = iter(refs)
    a_ref = next(it)
    w_ref = next(it)
    bias_ref = next(it) if has_bias else None
    res_ref = next(it) if has_res else None
    gate_ref = next(it) if has_res else None
    o_ref = next(it)
    acc_ref = next(it) if nk > 1 else None

    a = a_ref[...]
    if a_act == "tanh":
        a = jnp.tanh(a.astype(F32))
    elif a_act == "sigmoid":
        a = jax.nn.sigmoid(a.astype(F32))
    part = _dot(a.astype(BF16), w_ref[...])

    def finish(acc):
        if has_bias:
            acc = acc + bias_ref[...]
        if epi == "sqrelu":
            acc = jnp.square(jnp.maximum(acc, 0.0))
        if has_res:
            acc = res_ref[...] + gate_ref[0] * acc
        o_ref[...] = acc.astype(o_ref.dtype)

    if nk == 1:
        finish(part)
    else:
        k = pl.program_id(2)

        @pl.when(k == 0)
        def _():
            acc_ref[...] = part

        @pl.when(k > 0)
        def _():
            acc_ref[...] += part

        @pl.when(k == nk - 1)
        def _():
            finish(acc_ref[...])


def matmul(a, w, *, out_dtype=F32, a_act=None, epi=None, bias=None, res=None, gate=None,
           rows_per_batch=None, tm=1024, tn=1024, tk=None, a_col_block=0):
    M = a.shape[0]
    K, N = w.shape
    tm = min(tm, M if rows_per_batch is None else rows_per_batch)
    tn = min(tn, N)
    tk = K if tk is None else min(tk, K)
    nk = K // tk
    assert M % tm == 0 and N % tn == 0 and K % tk == 0
    has_bias = bias is not None
    has_res = res is not None
    in_specs = [pl.BlockSpec((tm, tk), lambda i, j, k: (i, k + a_col_block * nk)),
                pl.BlockSpec((tk, tn), lambda i, j, k: (k, j))]
    args = [a, w]
    if has_bias:
        in_specs.append(pl.BlockSpec((1, tn), lambda i, j, k: (0, j)))
        args.append(bias.reshape(1, N).astype(F32))
    if has_res:
        tiles_per_batch = rows_per_batch // tm
        assert rows_per_batch % tm == 0
        in_specs.append(pl.BlockSpec((tm, tn), lambda i, j, k: (i, j)))
        in_specs.append(pl.BlockSpec((1, 1, tn), lambda i, j, k: (i // tiles_per_batch, 0, j)))
        args += [res, gate]
    scratch = [pltpu.VMEM((tm, tn), F32)] if nk > 1 else []
    body = functools.partial(_mm_body, nk=nk, a_act=a_act, epi=epi, has_bias=has_bias, has_res=has_res)
    return pl.pallas_call(
        body,
        grid=(M // tm, N // tn, nk),
        in_specs=in_specs,
        out_specs=pl.BlockSpec((tm, tn), lambda i, j, k: (i, j)),
        out_shape=jax.ShapeDtypeStruct((M, N), out_dtype),
        scratch_shapes=scratch,
        compiler_params=_cparams("parallel", "parallel", "arbitrary"),
        name="matmul",
    )(*args)


def _neighbours(cur, prev8, next8, i, n_tiles):
    t = cur.shape[0]
    row = _iota2(cur.shape, 0)
    before = jnp.where(i > 0, prev8[7:8, :], 0.0)
    after = jnp.where(i < n_tiles - 1, next8[0:1, :], 0.0)
    xm1 = jnp.where(row == 0, before, pltpu.roll(cur, 1, 0))
    xp1 = jnp.where(row == t - 1, after, pltpu.roll(cur, t - 1, 0))
    return xm1, xp1


def _neighbour_specs(tl, width, n_rows8, col_map=None):
    r8 = tl // 8
    return [pl.BlockSpec((1, tl, width), lambda b, i: (b, i, 0)),
            pl.BlockSpec((1, 8, width), lambda b, i: (b, jnp.maximum(i * r8 - 1, 0), 0)),
            pl.BlockSpec((1, 8, width), lambda b, i: (b, jnp.minimum((i + 1) * r8, n_rows8 - 1), 0))]


def _hy_conv_body(p_ref, pm_ref, pp_ref, w_ref, b_ref, v_ref, x1_ref, x2_ref, *, n_tiles, C):
    i = pl.program_id(1)
    cur = p_ref[0]
    xm1, xp1 = _neighbours(cur, pm_ref[0], pp_ref[0], i, n_tiles)
    w = w_ref[...]
    u = xm1 * w[0:1] + cur * w[1:2] + xp1 * w[2:3] + b_ref[...]
    v_ref[0] = u[:, 0:C]
    x1_ref[0] = u[:, C:2 * C]
    x2_ref[0] = u[:, 2 * C:3 * C]


def hy_short_conv(p, conv_w, conv_b, C, tl=256):
    B, L, _ = p.shape
    W = 3 * C
    n_tiles = L // tl
    out = jax.ShapeDtypeStruct((B, L, C), F32)
    ospec = pl.BlockSpec((1, tl, C), lambda b, i: (b, i, 0))
    return pl.pallas_call(
        functools.partial(_hy_conv_body, n_tiles=n_tiles, C=C),
        grid=(B, n_tiles),
        in_specs=_neighbour_specs(tl, W, L // 8) + [pl.BlockSpec((3, W), lambda b, i: (0, 0)),
                                                    pl.BlockSpec((1, W), lambda b, i: (0, 0))],
        out_specs=[ospec, ospec, ospec],
        out_shape=[out, out, out],
        compiler_params=_cparams("parallel", "parallel"),
        name="hy_short_conv",
    )(p, p, p, conv_w, conv_b.reshape(1, W))


def _hy_filter_body(z_ref, w1_ref, b1_ref, f1_ref, w2_ref, b2_ref, f2_ref, w3_ref, dec_ref, o_ref, *, L, tm):
    i = pl.program_id(0)
    z = z_ref[...]
    f = jnp.sin(f1_ref[...] * (_dot(z, w1_ref[...], HI) + b1_ref[...]))
    f = jnp.sin(f2_ref[...] * (_dot(f, w2_ref[...], HI) + b2_ref[...]))
    t = z[:, 0:1]
    f = _dot(f, w3_ref[...], HI) * jnp.exp(-t * jnp.abs(dec_ref[...]))
    n = i * tm + _iota2(f.shape, 0)
    o_ref[...] = jnp.where(n == L, 0.0, f)


def hy_filter(L, C, w1, b1, freq1, w2, b2, freq2, w3, decay, tm=512):
    pos = np.concatenate([np.arange(L), [0], np.arange(L - 1, 0, -1)]).astype(np.float64)
    t = pos / (L - 1)
    ang = 2.0 * math.pi * pos / L
    nb = (HY_EMB - 1) // 2
    ba = np.linspace(1e-4, nb - 1, nb)[None, :] * ang[:, None]
    z = np.concatenate([t[:, None], np.cos(ba), -np.sin(ba)], axis=-1)
    zp = np.zeros((2 * L, 2 * HY_FFN), np.float32)
    zp[:, :HY_EMB] = z
    w1p = jnp.zeros((2 * HY_FFN, HY_FFN), F32).at[:HY_EMB].set(w1)
    nL = L // tm
    row = lambda a: a.reshape(1, -1)
    return pl.pallas_call(
        functools.partial(_hy_filter_body, L=L, tm=tm),
        grid=(2 * L // tm, HY_ORDER),
        in_specs=[pl.BlockSpec((tm, 2 * HY_FFN), lambda i, o: (i, 0)),
                  pl.BlockSpec((2 * HY_FFN, HY_FFN), lambda i, o: (0, 0)),
                  pl.BlockSpec((1, HY_FFN), lambda i, o: (0, 0)),
                  pl.BlockSpec((1, HY_FFN), lambda i, o: (0, 0)),
                  pl.BlockSpec((HY_FFN, HY_FFN), lambda i, o: (0, 0)),
                  pl.BlockSpec((1, HY_FFN), lambda i, o: (0, 0)),
                  pl.BlockSpec((1, HY_FFN), lambda i, o: (0, 0)),
                  pl.BlockSpec((HY_FFN, C), lambda i, o: (0, 2 * o + i // nL)),
                  pl.BlockSpec((1, C), lambda i, o: (0, 2 * o + i // nL))],
        out_specs=pl.BlockSpec((tm, C), lambda i, o: (i, o)),
        out_shape=jax.ShapeDtypeStruct((2 * L, HY_ORDER * C), F32),
        compiler_params=_cparams("parallel", "parallel"),
        name="hy_filter",
    )(jnp.asarray(zp), w1p, row(b1), row(freq1), w2, row(b2), row(freq2), w3, row(decay))


def _dft_constants(L):
    N = 2 * L
    N2 = DFT_N2
    N1 = N // N2
    K1 = N1 // 2 + 1
    k1 = np.arange(K1, dtype=np.float64)
    n1 = np.arange(N1, dtype=np.float64)
    ang = 2.0 * np.pi * np.outer(k1, n1) / N1
    a1 = np.stack([np.cos(ang), -np.sin(ang)], axis=1).reshape(2 * K1, N1)
    n1h = np.arange(N1 // 2, dtype=np.float64)
    angb = 2.0 * np.pi * np.outer(n1h, k1) / N1
    ck = np.where((np.arange(K1) == 0) | (np.arange(K1) == N1 // 2), 1.0, 2.0)[None, :]
    binv = np.stack([ck * np.cos(angb), -ck * np.sin(angb)], axis=2).reshape(N1 // 2, 2 * K1) / N
    n2 = np.arange(N2, dtype=np.float64)
    k2 = np.arange(N2, dtype=np.float64)
    kk = k1[:, None, None] + N1 * k2[None, :, None]
    ph = -2.0 * np.pi * ((kk * n2[None, None, :]) % N) / N
    mre, mim = np.cos(ph), np.sin(ph)
    mst = np.concatenate([np.concatenate([mre, -mim], axis=2),
                          np.concatenate([mim, mre], axis=2)], axis=1)
    f32 = lambda a: jnp.asarray(a.astype(np.float32))
    return dict(N1=N1, K1=K1, a1_full=f32(a1), a1_half=f32(a1[:, :N1 // 2]), binv=f32(binv),
                m=f32(mst), mt=f32(np.transpose(mst, (0, 2, 1))))


def _lmul_body(a_ref, x_ref, o_ref):
    o_ref[0] = _dot(a_ref[...], x_ref[0], HI)


def lmul(a, x, tile):
    B, K, W = x.shape
    R = a.shape[0]
    tile = min(tile, W)
    return pl.pallas_call(
        _lmul_body,
        grid=(B, W // tile),
        in_specs=[pl.BlockSpec((R, K), lambda b, j: (0, 0)),
                  pl.BlockSpec((1, K, tile), lambda b, j: (b, 0, j))],
        out_specs=pl.BlockSpec((1, R, tile), lambda b, j: (b, 0, j)),
        out_shape=jax.ShapeDtypeStruct((B, R, W), F32),
        compiler_params=_cparams("parallel", "parallel"),
        name="dft_outer",
    )(a, x)


def _dft_inner_body(m_ref, y_ref, o_ref):
    o_ref[0, 0] = _dot(m_ref[0], y_ref[0, 0], HI)


def dft_inner(m, y, ct):
    B, K1, R, W = y.shape
    return pl.pallas_call(
        _dft_inner_body,
        grid=(K1, B, W // ct),
        in_specs=[pl.BlockSpec((1, R, R), lambda k, b, j: (k, 0, 0)),
                  pl.BlockSpec((1, 1, R, ct), lambda k, b, j: (b, k, 0, j))],
        out_specs=pl.BlockSpec((1, 1, R, ct), lambda k, b, j: (b, k, 0, j)),
        out_shape=jax.ShapeDtypeStruct(y.shape, F32),
        compiler_params=_cparams("parallel", "parallel", "parallel"),
        name="dft_inner",
    )(m, y)


def _spec_mul_body(m_ref, mt_ref, y_ref, kf_ref, o_ref):
    z = _dot(m_ref[0], y_ref[0, 0], HI)
    h = z.shape[0] // 2
    zr, zi = z[:h], z[h:]
    kf = kf_ref[0, 0]
    kr, ki = kf[:h], kf[h:]
    p = jnp.concatenate([zr * kr - zi * ki, zr * ki + zi * kr], axis=0)
    o_ref[0, 0] = _dot(mt_ref[0], p, HI)


def spec_mul(m, mt, y, kf, order, C, ct):
    B, K1, R, _ = y.shape
    nc = C // ct
    return pl.pallas_call(
        _spec_mul_body,
        grid=(K1, B, nc),
        in_specs=[pl.BlockSpec((1, R, R), lambda k, b, j: (k, 0, 0)),
                  pl.BlockSpec((1, R, R), lambda k, b, j: (k, 0, 0)),
                  pl.BlockSpec((1, 1, R, ct), lambda k, b, j: (b, k, 0, j)),
                  pl.BlockSpec((1, 1, R, ct), lambda k, b, j: (0, k, 0, order * nc + j))],
        out_specs=pl.BlockSpec((1, 1, R, ct), lambda k, b, j: (b, k, 0, j)),
        out_shape=jax.ShapeDtypeStruct(y.shape, F32),
        compiler_params=_cparams("parallel", "parallel", "parallel"),
        name="hy_spec_mul",
    )(m, mt, y, kf)


def _dft_out_body(a_ref, t_ref, u_ref, g_ref, s_ref, o_ref):
    y = _dot(a_ref[...], t_ref[0], HI)
    u = u_ref[0]
    o_ref[0] = (g_ref[0] * (y + u * s_ref[...])).astype(o_ref.dtype)


def dft_out(binv, t, u, gate, skip_tiled, out_dtype):
    B, K, W = t.shape
    R = binv.shape[0]
    tile = skip_tiled.shape[1]
    return pl.pallas_call(
        _dft_out_body,
        grid=(B, W // tile),
        in_specs=[pl.BlockSpec((R, K), lambda b, j: (0, 0)),
                  pl.BlockSpec((1, K, tile), lambda b, j: (b, 0, j)),
                  pl.BlockSpec((1, R, tile), lambda b, j: (b, 0, j)),
                  pl.BlockSpec((1, R, tile), lambda b, j: (b, 0, j)),
                  pl.BlockSpec((1, tile), lambda b, j: (0, 0))],
        out_specs=pl.BlockSpec((1, R, tile), lambda b, j: (b, 0, j)),
        out_shape=jax.ShapeDtypeStruct((B, R, W), out_dtype),
        compiler_params=_cparams("parallel", "parallel"),
        name="dft_out",
    )(binv, t, u, gate, skip_tiled)


def hyena_mixer(p, C, conv_w, conv_b, f_w1, f_b1, f_freq1, f_w2, f_b2, f_freq2, f_w3, f_decay, skip):
    B, L, _ = p.shape
    cst = _dft_constants(L)
    N1, K1 = cst["N1"], cst["K1"]
    W = DFT_N2 * C
    lane_tile = 8 * C
    v, x1, x2 = hy_short_conv(p, conv_w, conv_b, C)
    kc = hy_filter(L, C, f_w1, f_b1, f_freq1, f_w2, f_b2, f_freq2, f_w3, f_decay)
    yf = lmul(cst["a1_full"], kc.reshape(1, N1, DFT_N2 * HY_ORDER * C), lane_tile)
    kf = dft_inner(cst["m"], yf.reshape(1, K1, 2 * DFT_N2, HY_ORDER * C), C)

    def long_conv(u, order, gate, out_dtype):
        y1 = lmul(cst["a1_half"], u.reshape(B, N1 // 2, W), lane_tile)
        t = spec_mul(cst["m"], cst["mt"], y1.reshape(B, K1, 2 * DFT_N2, C), kf, order, C, C)
        sk = jnp.tile(skip[order].reshape(1, C), (1, lane_tile // C))
        out = dft_out(cst["binv"], t.reshape(B, 2 * K1, W), u.reshape(B, N1 // 2, W),
                      gate.reshape(B, N1 // 2, W), sk, out_dtype)
        return out.reshape(B, L, C)

    z = long_conv(v, 0, x1, F32)
    return long_conv(z, 1, x2, BF16)


def _gla_body(*refs, reverse, final, n_chunks):
    if final:
        q_ref, k_ref, v_ref, lr_ref, up_ref, upb_ref, prev_ref, r_ref, gn_ref, o_ref, st_ref = refs
    else:
        q_ref, k_ref, v_ref, lr_ref, up_ref, upb_ref, o_ref, st_ref = refs
    C = GLA_CHUNK

    @pl.when(pl.program_id(2) == 0)
    def _():
        st_ref[...] = jnp.zeros_like(st_ref)

    rr = _iota2((C, C), 0)
    cc = _iota2((C, C), 1)
    keep = (rr <= cc) if reverse else (rr >= cc)
    tri = keep.astype(F32)
    up = up_ref[0]
    upb = upb_ref[0]
    def chunk(step, carry):
        c = (n_chunks - 1 - step) if reverse else step
        rows = pl.ds(pl.multiple_of(c * C, C), C)
        q = q_ref[0, rows, :] * (GLA_DK ** -0.5)
        k = k_ref[0, rows, :]
        v = v_ref[0, rows, :]
        g = _log_sigmoid(_dot(lr_ref[0, rows, :], up, HI) + upb) / GLA_TAU
        bc = _dot(tri, g, HI)
        blast = bc[0:1] if reverse else bc[C - 1:C]
        q_in = (q * jnp.exp(bc)).astype(BF16)
        k_in = (k * jnp.exp(-bc)).astype(BF16)
        k_end = (k * jnp.exp(blast - bc)).astype(BF16)
        vb = v.astype(BF16)
        att = jnp.where(keep, _dot_nt(q_in, k_in), 0.0)
        st = st_ref[...]
        o = _dot(att.astype(BF16), vb) + _dot_nt(q_in, st.astype(BF16))
        st_ref[...] = st * jnp.exp(blast) + _dot_tn(vb, k_end)
        if final:
            o = o + prev_ref[0, rows, :]
            o = o * lax.rsqrt(jnp.mean(o * o, axis=-1, keepdims=True) + NORM_EPS) * gn_ref[...]
            r = r_ref[0, rows, :]
            o = o * (r * jax.nn.sigmoid(r))
        o_ref[0, rows, :] = o.astype(o_ref.dtype)
        return carry

    lax.fori_loop(0, n_chunks, chunk, 0)


def gla_direction(p, plr, up_pad, up_b, col0, *, reverse, prev=None, norm_g=None, tl=512):
    B, L, _ = p.shape
    H = GLA_HEADS
    nt = L // tl
    d = 1 if reverse else 0
    final = prev is not None
    tmap = (lambda t: nt - 1 - t) if reverse else (lambda t: t)
    qb = col0
    kb = col0 + H
    vb = (col0 + 2 * H) // 2
    rb = vb + H
    in_specs = [pl.BlockSpec((1, tl, GLA_DK), lambda b, h, t: (b, tmap(t), qb + h)),
                pl.BlockSpec((1, tl, GLA_DK), lambda b, h, t: (b, tmap(t), kb + h)),
                pl.BlockSpec((1, tl, GLA_DV), lambda b, h, t: (b, tmap(t), vb + h)),
                pl.BlockSpec((1, tl, LANES), lambda b, h, t: (b, tmap(t), 0)),
                pl.BlockSpec((1, LANES, GLA_DK), lambda b, h, t: (d, 0, h)),
                pl.BlockSpec((1, 1, GLA_DK), lambda b, h, t: (d, 0, h))]
    args = [p, p, p, plr, up_pad, up_b]
    if final:
        in_specs += [pl.BlockSpec((1, tl, GLA_DV), lambda b, h, t: (b, tmap(t), h)),
                     pl.BlockSpec((1, tl, GLA_DV), lambda b, h, t: (b, tmap(t), rb + h)),
                     pl.BlockSpec((1, GLA_DV), lambda b, h, t: (0, h))]
        args += [prev, p, norm_g.reshape(1, H * GLA_DV)]
    return pl.pallas_call(
        functools.partial(_gla_body, reverse=reverse, final=final, n_chunks=tl // GLA_CHUNK),
        grid=(B, H, nt),
        in_specs=in_specs,
        out_specs=pl.BlockSpec((1, tl, GLA_DV), lambda b, h, t: (b, tmap(t), h)),
        out_shape=jax.ShapeDtypeStruct((B, L, H * GLA_DV), BF16 if final else F32),
        scratch_shapes=[pltpu.VMEM((GLA_DV, GLA_DK), F32)],
        compiler_params=_cparams("parallel", "parallel", "arbitrary"),
        name="gla_bwd" if reverse else "gla_fwd",
    )(*args)


def gla_mixer(p, plr, col0, up, up_b, norm_g):
    H = GLA_HEADS
    up_pad = jnp.zeros((2, LANES, H * GLA_DK), F32)
    up_pad = up_pad.at[0, 0:GLA_RANK].set(up[0]).at[1, GLA_RANK:2 * GLA_RANK].set(up[1])
    upb = up_b.reshape(2, 1, H * GLA_DK)
    o_f = gla_direction(p, plr, up_pad, upb, col0, reverse=False)
    return gla_direction(p, plr, up_pad, upb, col0, reverse=True, prev=o_f, norm_g=norm_g)


def _rw_mix_body(x_ref, xm_ref, xp_ref, g_ref, sc_ref, sh_ref, mu_ref, *o_refs, n_tiles):
    i = pl.program_id(1)
    g, sc, sh = g_ref[...], sc_ref[0], sh_ref[0]
    h = _norm_mod_value(x_ref[0], g, sc, sh)
    hm = _norm_mod_value(xm_ref[0], g, sc, sh)
    hp = _norm_mod_value(xp_ref[0], g, sc, sh)
    hm1, hp1 = _neighbours(h, hm, hp, i, n_tiles)
    xx = 0.5 * (hm1 + hp1) - h
    mu = mu_ref[...]
    for j, o_ref in enumerate(o_refs):
        o_ref[0] = (h + xx * mu[j:j + 1]).astype(o_ref.dtype)


def rw_mix(x, g, scale, shift, mu, tl=256):
    B, L, D = x.shape
    n_tiles = L // tl
    out = jax.ShapeDtypeStruct((B, L, D), BF16)
    ospec = pl.BlockSpec((1, tl, D), lambda b, i: (b, i, 0))
    mu8 = jnp.zeros((8, D), F32).at[:6].set(mu)
    return pl.pallas_call(
        functools.partial(_rw_mix_body, n_tiles=n_tiles),
        grid=(B, n_tiles),
        in_specs=_neighbour_specs(tl, D, L // 8) + [pl.BlockSpec((1, D), lambda b, i: (0, 0)),
                                                    pl.BlockSpec((1, 1, D), lambda b, i: (b, 0, 0)),
                                                    pl.BlockSpec((1, 1, D), lambda b, i: (b, 0, 0)),
                                                    pl.BlockSpec((8, D), lambda b, i: (0, 0))],
        out_specs=[ospec] * 6,
        out_shape=[out] * 6,
        compiler_params=_cparams("parallel", "parallel"),
        name="rw_mix",
    )(x, x, x, g.reshape(1, D), scale, shift, mu8)


def _head_ones():
    r = _iota2((LANES, LANES), 0) // RW_HEAD
    c = _iota2((LANES, LANES), 1) // RW_HEAD
    return (r == c).astype(F32)


def _rw_prep_body(k_ref, r_ref, v_ref, a0_ref, a1_ref, kk_w_ref, ka_ref, rk_ref, kk_ref, bon_ref):
    ones = _head_ones()
    k = k_ref[...]
    kk = k * kk_w_ref[...]
    ss = _dot(kk * kk, ones, HI)
    kk_ref[...] = kk * lax.rsqrt(jnp.maximum(ss, 1e-24))
    a_sum = jax.nn.sigmoid(a0_ref[...]) + jax.nn.sigmoid(a1_ref[...])
    kd_sum = k * (2.0 + (a_sum - 2.0) * ka_ref[...])
    bon_ref[...] = _dot(r_ref[...] * kd_sum * rk_ref[...], ones, HI) * v_ref[...]


def rw_prep(k, r, v, apre0, apre1, k_k, k_a, r_k, tm=2048):
    T, D = k.shape
    tm = min(tm, T)
    spec = pl.BlockSpec((tm, LANES), lambda i, j: (i, j))
    pspec = pl.BlockSpec((1, LANES), lambda i, j: (0, j))
    out = jax.ShapeDtypeStruct((T, D), F32)
    return pl.pallas_call(
        _rw_prep_body,
        grid=(T // tm, D // LANES),
        in_specs=[spec, spec, spec, spec, spec, pspec, pspec, pspec],
        out_specs=[spec, spec],
        out_shape=[out, out],
        compiler_params=_cparams("parallel", "parallel"),
        name="rw_prep",
    )(k, r, v, apre0, apre1, k_k.reshape(1, D), k_a.reshape(1, D), r_k.reshape(1, D))


def _rw_scan_body(r_ref, k_ref, v_ref, kk_ref, w_ref, a_ref, ka_ref, o_ref, h_ref, *, reverse, n_chunks):
    C = RW_CHUNK
    P = 2 * C

    @pl.when(pl.program_id(2) == 0)
    def _():
        h_ref[...] = jnp.zeros_like(h_ref)

    rc = _iota2((C, C), 0)
    cc = _iota2((C, C), 1)
    tri = ((rc <= cc) if reverse else (rc >= cc)).astype(F32)
    rp = _iota2((P, P), 0)
    cp = _iota2((P, P), 1)
    strict = (rp < cp) if reverse else (rp > cp)
    incl = (rp <= cp) if reverse else (rp >= cp)
    eye = (rp == cp).astype(F32)
    lane = _iota2((C, LANES), 1)
    head0 = lane < RW_HEAD
    ka = ka_ref[...]

    def stack(x):
        return jnp.concatenate([jnp.where(head0, x, 0.0), jnp.where(head0, 0.0, x)], axis=0)

    def chunk(step, carry):
        c = (n_chunks - 1 - step) if reverse else step
        rows = pl.ds(pl.multiple_of(c * C, C), C)
        r = r_ref[0, rows, :]
        k = k_ref[0, rows, :]
        v = v_ref[0, rows, :]
        kk = kk_ref[0, rows, :]
        lw = -RW_LOG_DECAY_MAX * jax.nn.sigmoid(w_ref[0, rows, :])
        a = jax.nn.sigmoid(a_ref[0, rows, :])
        kd = k * (1.0 + (a - 1.0) * ka)
        b = kk * a
        cum = _dot(tri, lw, HI)
        cum_x = cum - lw
        total = cum[0:1] if reverse else cum[C - 1:C]
        e_neg = jnp.exp(-cum)
        rs = stack(r * jnp.exp(cum))
        xs = stack(kk * jnp.exp(cum_x))
        ks = stack(kd * e_neg)
        bs = stack(b * e_neg)
        vs = stack(v)
        h = h_ref[...]
        a_b = jnp.where(strict, _dot_nt(xs, bs, HI), 0.0)
        a_k = jnp.where(strict, _dot_nt(xs, ks, HI), 0.0)
        q_k = jnp.where(incl, _dot_nt(rs, ks, HI), 0.0)
        q_b = jnp.where(incl, _dot_nt(rs, bs, HI), 0.0)
        pw = -a_b
        tinv = eye + pw
        for _ in range(int(math.log2(C)) - 1):
            pw = _dot(pw, pw, HI)
            tinv = tinv + _dot(tinv, pw, HI)
        u = _dot(tinv, _dot(xs, h, HI) + _dot(a_k, vs, HI), HI)
        os = _dot(rs, h, HI) + _dot(q_k, vs, HI) - _dot(q_b, u, HI)
        o_ref[0, rows, :] = os[:C] + os[C:]
        gcol = jnp.transpose(jnp.broadcast_to(jnp.exp(total), (LANES, LANES)))
        h_ref[...] = gcol * (h + _dot_tn(ks, vs, HI) - _dot_tn(bs, u, HI))
        return carry

    lax.fori_loop(0, n_chunks, chunk, 0)


def rw_scan(r, k, v, kk, wpre, apre, k_a, *, reverse, tl=512):
    B, L, D = r.shape
    nb = D // LANES
    nt = L // tl
    tmap = (lambda t: nt - 1 - t) if reverse else (lambda t: t)
    spec = pl.BlockSpec((1, tl, LANES), lambda b, p, t: (b, tmap(t), p))
    return pl.pallas_call(
        functools.partial(_rw_scan_body, reverse=reverse, n_chunks=tl // RW_CHUNK),
        grid=(B, nb, nt),
        in_specs=[spec, spec, spec, spec, spec, spec, pl.BlockSpec((1, LANES), lambda b, p, t: (0, p))],
        out_specs=spec,
        out_shape=jax.ShapeDtypeStruct((B, L, D), F32),
        scratch_shapes=[pltpu.VMEM((LANES, LANES), F32)],
        compiler_params=_cparams("parallel", "parallel", "arbitrary"),
        name="rw_scan_bwd" if reverse else "rw_scan_fwd",
    )(r, k, v, kk, wpre, apre, k_a.reshape(1, D))


def _rw_post_body(of_ref, ob_ref, bon_ref, g_ref, gng_ref, gnb_ref, o_ref):
    ones = _head_ones() * (1.0 / RW_HEAD)
    so = of_ref[...] + ob_ref[...]
    mean = _dot(so, ones, HI)
    dlt = so - mean
    var = _dot(dlt * dlt, ones, HI)
    y = dlt * lax.rsqrt(var + RW_GN_EPS) * gng_ref[...] + gnb_ref[...]
    o_ref[...] = ((y + bon_ref[...]) * g_ref[...]).astype(o_ref.dtype)


def rw_post(o_f, o_b, bonus, g, gn_g, gn_b, tm=2048):
    T, D = o_f.shape
    tm = min(tm, T)
    spec = pl.BlockSpec((tm, LANES), lambda i, j: (i, j))
    pspec = pl.BlockSpec((1, LANES), lambda i, j: (0, j))
    return pl.pallas_call(
        _rw_post_body,
        grid=(T // tm, D // LANES),
        in_specs=[spec, spec, spec, spec, pspec, pspec],
        out_specs=spec,
        out_shape=jax.ShapeDtypeStruct((T, D), BF16),
        compiler_params=_cparams("parallel", "parallel"),
        name="rw_post",
    )(o_f, o_b, bonus, g, gn_g.reshape(1, D), gn_b.reshape(1, D))


def _pad_cols(w, n):
    return jnp.pad(w, ((0, 0), (0, n - w.shape[1])))


def _pad_rows(w, n):
    return jnp.pad(w, ((0, n - w.shape[0]), (0, 0)))


def rwkv7_block(x, gate, g_norm, scale, shift, mu, w_r, w_k, w_v, w0, w1, w2, a0, a1, a2, g1, g2,
                k_k, k_a, r_k, gn_g, gn_b, w_o):
    B, L, D = x.shape
    T = B * L
    xr, xw, xk, xv, xa, xg = [a.reshape(T, D) for a in rw_mix(x, g_norm, scale, shift, mu)]
    r = matmul(xr, w_r.astype(BF16))
    k = matmul(xk, w_k.astype(BF16))
    v = matmul(xv, w_v.astype(BF16))
    w1c = jnp.concatenate([_pad_cols(w1[0], LANES), _pad_cols(w1[1], LANES)], axis=1).astype(BF16)
    a1c = jnp.concatenate([_pad_cols(a1[0], LANES), _pad_cols(a1[1], LANES)], axis=1).astype(BF16)
    wmid = matmul(xw, w1c)
    amid = matmul(xa, a1c)
    gmid = matmul(xg, g1.astype(BF16))
    wpre = [matmul(wmid, _pad_rows(w2[d], LANES).astype(BF16), a_act="tanh", bias=w0[d], a_col_block=d)
            for d in range(2)]
    apre = [matmul(amid, _pad_rows(a2[d], LANES).astype(BF16), bias=a0[d], a_col_block=d)
            for d in range(2)]
    g = matmul(gmid, g2.astype(BF16), a_act="sigmoid")
    kk, bonus = rw_prep(k, r, v, apre[0], apre[1], k_k, k_a, r_k.reshape(D))
    sh = lambda a: a.reshape(B, L, D)
    o_f = rw_scan(sh(r), sh(k), sh(v), sh(kk), sh(wpre[0]), sh(apre[0]), k_a, reverse=False)
    o_b = rw_scan(sh(r), sh(k), sh(v), sh(kk), sh(wpre[1]), sh(apre[1]), k_a, reverse=True)
    y = rw_post(o_f.reshape(T, D), o_b.reshape(T, D), bonus, g, gn_g, gn_b)
    out = matmul(y, w_o.astype(BF16), res=x.reshape(T, D), gate=gate, rows_per_batch=L)
    return out.reshape(B, L, D)


def ffn_block(x, c, norm_g, ada_w, ada_b, idx, w1, w2):
    B, L, D = x.shape
    T = B * L
    shift, scale, gate = ada_mod(c, ada_w, ada_b, idx)
    h = norm_mod(x, norm_g, scale, shift).reshape(T, D)
    hid = matmul(h, w1.astype(BF16), out_dtype=BF16, epi="sqrelu")
    out = matmul(hid, w2.astype(BF16), res=x.reshape(T, D), gate=gate, rows_per_batch=L, tm=512, tn=512)
    return out.reshape(B, L, D)


def hyena_gla_block(x, gate, h, w_in, w_out, conv_w, conv_b, f_w1, f_b1, f_freq1, f_w2, f_b2, f_freq2,
                    f_w3, f_decay, hy_skip, gla_up, gla_up_b, gla_norm_g):
    B, L, D = x.shape
    T = B * L
    C = conv_w.shape[1] // 3
    n_main = w_in.shape[1] - 2 * GLA_RANK
    p = matmul(h, w_in[:, :n_main].astype(BF16)).reshape(B, L, n_main)
    plr = matmul(h, _pad_cols(w_in[:, n_main:], LANES).astype(BF16)).reshape(B, L, LANES)
    y_hy = hyena_mixer(p, C, conv_w, conv_b, f_w1, f_b1, f_freq1, f_w2, f_b2, f_freq2, f_w3, f_decay, hy_skip)
    y_gla = gla_mixer(p, plr, 3 * C // LANES, gla_up, gla_up_b, gla_norm_g)
    y = jnp.concatenate([y_hy, y_gla], axis=-1).reshape(T, -1)
    out = matmul(y, w_out.astype(BF16), res=x.reshape(T, D), gate=gate, rows_per_batch=L)
    return out.reshape(B, L, D)


def kernel(x, c, mix_norm_g, mix_ada_w, mix_ada_b, ab_w_in, ab_w_out, hy_conv_w, hy_conv_b, hy_ffn_w1, hy_ffn_b1, hy_freq1, hy_ffn_w2, hy_ffn_b2, hy_freq2, hy_ffn_w3, hy_decay, hy_skip, gla_up, gla_up_b, gla_norm_g, tm_norm_g, tm_ada_w, tm_ada_b, rw_mu, rw_w_r, rw_w_k, rw_w_v, rw_w0, rw_w1, rw_w2, rw_a0, rw_a1, rw_a2, rw_g1, rw_g2, rw_k_k, rw_k_a, rw_r_k, rw_gn_g, rw_gn_b, rw_w_o, ffn_norm_g, ffn_ada_w, ffn_ada_b, ffn_w1, ffn_w2, final_norm_g):
    B, L, D = x.shape
    depth = ffn_w1.shape[0]
    for layer in range(depth):
        i = layer // 2
        if layer % 2 == 0:
            shift, scale, gate = ada_mod(c, mix_ada_w, mix_ada_b, i)
            h = norm_mod(x, mix_norm_g[i], scale, shift).reshape(B * L, D)
            x = hyena_gla_block(x, gate, h, ab_w_in[i], ab_w_out[i], hy_conv_w[i], hy_conv_b[i],
                                hy_ffn_w1[i], hy_ffn_b1[i], hy_freq1[i], hy_ffn_w2[i], hy_ffn_b2[i],
                                hy_freq2[i], hy_ffn_w3[i], hy_decay[i], hy_skip[i],
                                gla_up[i], gla_up_b[i], gla_norm_g[i])
        else:
            shift, scale, gate = ada_mod(c, tm_ada_w, tm_ada_b, i)
            x = rwkv7_block(x, gate, tm_norm_g[i], scale, shift, rw_mu[i], rw_w_r[i], rw_w_k[i], rw_w_v[i],
                            rw_w0[i], rw_w1[i], rw_w2[i], rw_a0[i], rw_a1[i], rw_a2[i], rw_g1[i], rw_g2[i],
                            rw_k_k[i], rw_k_a[i], rw_r_k[i], rw_gn_g[i], rw_gn_b[i], rw_w_o[i])
        x = ffn_block(x, c, ffn_norm_g[layer], ffn_ada_w, ffn_ada_b, layer, ffn_w1[layer], ffn_w2[layer])
    return rms_final(x, final_norm_g)
```

```python
import functools
import math

import numpy as np
import jax
import jax.numpy as jnp
from jax import lax
from jax.experimental import pallas as pl
from jax.experimental.pallas import tpu as pltpu

F32 = jnp.float32
BF16 = jnp.bfloat16
HI = lax.Precision.HIGHEST

LANES = 128
NORM_EPS = 1e-6
GLA_HEADS = 4
GLA_DK = 128
GLA_DV = 256
GLA_RANK = 16
GLA_TAU = 16.0
GLA_CHUNK = 64
RW_HEAD = 64
RW_CHUNK = 64
RW_LOG_DECAY_MAX = 0.606531
RW_GN_EPS = 64e-5
HY_ORDER = 2
HY_EMB = 33
HY_FFN = 64
DFT_N2 = 128
VMEM_LIMIT = 52 * 1024 * 1024


def _cparams(*sem):
    return pltpu.CompilerParams(dimension_semantics=sem, vmem_limit_bytes=VMEM_LIMIT)


def _dot(a, b, precision=None):
    return jnp.dot(a, b, preferred_element_type=F32, precision=precision)


def _dot_nt(a, b, precision=None):
    return lax.dot_general(a, b, (((1,), (1,)), ((), ())), preferred_element_type=F32, precision=precision)


def _dot_tn(a, b, precision=None):
    return lax.dot_general(a, b, (((0,), (0,)), ((), ())), preferred_element_type=F32, precision=precision)


def _iota2(shape, axis):
    return lax.broadcasted_iota(jnp.int32, shape, axis)


def _log_sigmoid(x):
    return jnp.minimum(x, 0.0) - jnp.log(1.0 + jnp.exp(-jnp.abs(x)))


def _ada_body(c_ref, w_ref, b_ref, o_ref):
    c = c_ref[...]
    s = c * jax.nn.sigmoid(c)
    o_ref[...] = _dot(s, w_ref[0], HI) + b_ref[0]


def ada_mod(c, w, b, idx):
    B, D = c.shape
    n3 = w.shape[2]
    tn = 512
    out = pl.pallas_call(
        _ada_body,
        grid=(n3 // tn,),
        in_specs=[pl.BlockSpec((B, D), lambda j: (0, 0)),
                  pl.BlockSpec((1, D, tn), lambda j: (idx, 0, j)),
                  pl.BlockSpec((1, 1, tn), lambda j: (idx, 0, j))],
        out_specs=pl.BlockSpec((B, tn), lambda j: (0, j)),
        out_shape=jax.ShapeDtypeStruct((B, n3), F32),
        compiler_params=_cparams("parallel"),
        name="ada_mod",
    )(c, w, b.reshape(b.shape[0], 1, n3))
    shift, scale, gate = jnp.split(out[:, None, :], 3, axis=-1)
    return shift, scale, gate


def _norm_mod_value(x, g, scale, shift):
    y = x * lax.rsqrt(jnp.mean(x * x, axis=-1, keepdims=True) + NORM_EPS)
    return (y * g) * (1.0 + scale) + shift


def _norm_mod_body(x_ref, g_ref, sc_ref, sh_ref, o_ref):
    o_ref[0] = _norm_mod_value(x_ref[0], g_ref[...], sc_ref[0], sh_ref[0]).astype(o_ref.dtype)


def norm_mod(x, g, scale, shift, out_dtype=BF16, tm=512):
    B, L, D = x.shape
    return pl.pallas_call(
        _norm_mod_body,
        grid=(B, L // tm),
        in_specs=[pl.BlockSpec((1, tm, D), lambda b, i: (b, i, 0)),
                  pl.BlockSpec((1, D), lambda b, i: (0, 0)),
                  pl.BlockSpec((1, 1, D), lambda b, i: (b, 0, 0)),
                  pl.BlockSpec((1, 1, D), lambda b, i: (b, 0, 0))],
        out_specs=pl.BlockSpec((1, tm, D), lambda b, i: (b, i, 0)),
        out_shape=jax.ShapeDtypeStruct((B, L, D), out_dtype),
        compiler_params=_cparams("parallel", "parallel"),
        name="norm_mod",
    )(x, g.reshape(1, D), scale, shift)


def _rms_body(x_ref, g_ref, o_ref):
    x = x_ref[0]
    y = x * lax.rsqrt(jnp.mean(x * x, axis=-1, keepdims=True) + NORM_EPS)
    o_ref[0] = y * g_ref[...]


def rms_final(x, g, tm=512):
    B, L, D = x.shape
    return pl.pallas_call(
        _rms_body,
        grid=(B, L // tm),
        in_specs=[pl.BlockSpec((1, tm, D), lambda b, i: (b, i, 0)),
                  pl.BlockSpec((1, D), lambda b, i: (0, 0))],
        out_specs=pl.BlockSpec((1, tm, D), lambda b, i: (b, i, 0)),
        out_shape=jax.ShapeDtypeStruct((B, L, D), F32),
        compiler_params=_cparams("parallel", "parallel"),
        name="rms_final",
    )(x, g.reshape(1, D))


def _mm_body(*refs, nk, a_act, epi, has_bias, has_res):
    it = iter(refs)
    a_ref = next(it)
    w_ref = next(it)
    bias_ref = next(it) if has_bias else None
    res_ref = next(it) if has_res else None
    gate_ref = next(it) if has_res else None
    o_ref = next(it)
    acc_ref = next(it) if nk > 1 else None

    a = a_ref[...]
    if a_act == "tanh":
        a = jnp.tanh(a.astype(F32))
    elif a_act == "sigmoid":
        a = jax.nn.sigmoid(a.astype(F32))
    part = _dot(a.astype(BF16), w_ref[...])

    def finish(acc):
        if has_bias:
            acc = acc + bias_ref[...]
        if epi == "sqrelu":
            acc = jnp.square(jnp.maximum(acc, 0.0))
        if has_res:
            acc = res_ref[...] + gate_ref[0] * acc
        o_ref[...] = acc.astype(o_ref.dtype)

    if nk == 1:
        finish(part)
    else:
        k = pl.program_id(2)

        @pl.when(k == 0)
        def _():
            acc_ref[...] = part

        @pl.when(k > 0)
        def _():
            acc_ref[...] += part

        @pl.when(k == nk - 1)
        def _():
            finish(acc_ref[...])


def matmul(a, w, *, out_dtype=F32, a_act=None, epi=None, bias=None, res=None, gate=None,
           rows_per_batch=None, tm=1024, tn=1024, tk=None, a_col_block=0):
    M = a.shape[0]
    K, N = w.shape
    tm = min(tm, M if rows_per_batch is None else rows_per_batch)
    tn = min(tn, N)
    tk = K if tk is None else min(tk, K)
    nk = K // tk
    assert M % tm == 0 and N % tn == 0 and K % tk == 0
    has_bias = bias is not None
    has_res = res is not None
    in_specs = [pl.BlockSpec((tm, tk), lambda i, j, k: (i, k + a_col_block * nk)),
                pl.BlockSpec((tk, tn), lambda i, j, k: (k, j))]
    args = [a, w]
    if has_bias:
        in_specs.append(pl.BlockSpec((1, tn), lambda i, j, k: (0, j)))
        args.append(bias.reshape(1, N).astype(F32))
    if has_res:
        tiles_per_batch = rows_per_batch // tm
        assert rows_per_batch % tm == 0
        in_specs.append(pl.BlockSpec((tm, tn), lambda i, j, k: (i, j)))
        in_specs.append(pl.BlockSpec((1, 1, tn), lambda i, j, k: (i // tiles_per_batch, 0, j)))
        args += [res, gate]
    scratch = [pltpu.VMEM((tm, tn), F32)] if nk > 1 else []
    body = functools.partial(_mm_body, nk=nk, a_act=a_act, epi=epi, has_bias=has_bias, has_res=has_res)
    return pl.pallas_call(
        body,
        grid=(M // tm, N // tn, nk),
        in_specs=in_specs,
        out_specs=pl.BlockSpec((tm, tn), lambda i, j, k: (i, j)),
        out_shape=jax.ShapeDtypeStruct((M, N), out_dtype),
        scratch_shapes=scratch,
        compiler_params=_cparams("parallel", "parallel", "arbitrary"),
        name="matmul",
    )(*args)


def _neighbours(cur, prev8, next8, i, n_tiles):
    t = cur.shape[0]
    row = _iota2(cur.shape, 0)
    before = jnp.where(i > 0, prev8[7:8, :], 0.0)
    after = jnp.where(i < n_tiles - 1, next8[0:1, :], 0.0)
    xm1 = jnp.where(row == 0, before, pltpu.roll(cur, 1, 0))
    xp1 = jnp.where(row == t - 1, after, pltpu.roll(cur, t - 1, 0))
    return xm1, xp1


def _neighbour_specs(tl, width, n_rows8, col_map=None):
    r8 = tl // 8
    return [pl.BlockSpec((1, tl, width), lambda b, i: (b, i, 0)),
            pl.BlockSpec((1, 8, width), lambda b, i: (b, jnp.maximum(i * r8 - 1, 0), 0)),
            pl.BlockSpec((1, 8, width), lambda b, i: (b, jnp.minimum((i + 1) * r8, n_rows8 - 1), 0))]


def _hy_conv_body(p_ref, pm_ref, pp_ref, w_ref, b_ref, v_ref, x1_ref, x2_ref, *, n_tiles, C):
    i = pl.program_id(1)
    cur = p_ref[0]
    xm1, xp1 = _neighbours(cur, pm_ref[0], pp_ref[0], i, n_tiles)
    w = w_ref[...]
    u = xm1 * w[0:1] + cur * w[1:2] + xp1 * w[2:3] + b_ref[...]
    v_ref[0] = u[:, 0:C]
    x1_ref[0] = u[:, C:2 * C]
    x2_ref[0] = u[:, 2 * C:3 * C]


def hy_short_conv(p, conv_w, conv_b, C, tl=256):
    B, L, _ = p.shape
    W = 3 * C
    n_tiles = L // tl
    out = jax.ShapeDtypeStruct((B, L, C), F32)
    ospec = pl.BlockSpec((1, tl, C), lambda b, i: (b, i, 0))
    return pl.pallas_call(
        functools.partial(_hy_conv_body, n_tiles=n_tiles, C=C),
        grid=(B, n_tiles),
        in_specs=_neighbour_specs(tl, W, L // 8) + [pl.BlockSpec((3, W), lambda b, i: (0, 0)),
                                                    pl.BlockSpec((1, W), lambda b, i: (0, 0))],
        out_specs=[ospec, ospec, ospec],
        out_shape=[out, out, out],
        compiler_params=_cparams("parallel", "parallel"),
        name="hy_short_conv",
    )(p, p, p, conv_w, conv_b.reshape(1, W))


def _hy_filter_body(z_ref, w1_ref, b1_ref, f1_ref, w2_ref, b2_ref, f2_ref, w3_ref, dec_ref, o_ref, *, L, tm):
    i = pl.program_id(0)
    z = z_ref[...]
    f = jnp.sin(f1_ref[...] * (_dot(z, w1_ref[...], HI) + b1_ref[...]))
    f = jnp.sin(f2_ref[...] * (_dot(f, w2_ref[...], HI) + b2_ref[...]))
    t = z[:, 0:1]
    f = _dot(f, w3_ref[...], HI) * jnp.exp(-t * jnp.abs(dec_ref[...]))
    n = i * tm + _iota2(f.shape, 0)
    o_ref[...] = jnp.where(n == L, 0.0, f)


def hy_filter(L, C, w1, b1, freq1, w2, b2, freq2, w3, decay, tm=512):
    pos = np.concatenate([np.arange(L), [0], np.arange(L - 1, 0, -1)]).astype(np.float64)
    t = pos / (L - 1)
    ang = 2.0 * math.pi * pos / L
    nb = (HY_EMB - 1) // 2
    ba = np.linspace(1e-4, nb - 1, nb)[None, :] * ang[:, None]
    z = np.concatenate([t[:, None], np.cos(ba), -np.sin(ba)], axis=-1)
    zp = np.zeros((2 * L, 2 * HY_FFN), np.float32)
    zp[:, :HY_EMB] = z
    w1p = jnp.zeros((2 * HY_FFN, HY_FFN), F32).at[:HY_EMB].set(w1)
    nL = L // tm
    row = lambda a: a.reshape(1, -1)
    return pl.pallas_call(
        functools.partial(_hy_filter_body, L=L, tm=tm),
        grid=(2 * L // tm, HY_ORDER),
        in_specs=[pl.BlockSpec((tm, 2 * HY_FFN), lambda i, o: (i, 0)),
                  pl.BlockSpec((2 * HY_FFN, HY_FFN), lambda i, o: (0, 0)),
                  pl.BlockSpec((1, HY_FFN), lambda i, o: (0, 0)),
                  pl.BlockSpec((1, HY_FFN), lambda i, o: (0, 0)),
                  pl.BlockSpec((HY_FFN, HY_FFN), lambda i, o: (0, 0)),
                  pl.BlockSpec((1, HY_FFN), lambda i, o: (0, 0)),
                  pl.BlockSpec((1, HY_FFN), lambda i, o: (0, 0)),
                  pl.BlockSpec((HY_FFN, C), lambda i, o: (0, 2 * o + i // nL)),
                  pl.BlockSpec((1, C), lambda i, o: (0, 2 * o + i // nL))],
        out_specs=pl.BlockSpec((tm, C), lambda i, o: (i, o)),
        out_shape=jax.ShapeDtypeStruct((2 * L, HY_ORDER * C), F32),
        compiler_params=_cparams("parallel", "parallel"),
        name="hy_filter",
    )(jnp.asarray(zp), w1p, row(b1), row(freq1), w2, row(b2), row(freq2), w3, row(decay))


def _dft_constants(L):
    N = 2 * L
    N2 = DFT_N2
    N1 = N // N2
    K1 = N1 // 2 + 1
    k1 = np.arange(K1, dtype=np.float64)
    n1 = np.arange(N1, dtype=np.float64)
    ang = 2.0 * np.pi * np.outer(k1, n1) / N1
    a1 = np.stack([np.cos(ang), -np.sin(ang)], axis=1).reshape(2 * K1, N1)
    n1h = np.arange(N1 // 2, dtype=np.float64)
    angb = 2.0 * np.pi * np.outer(n1h, k1) / N1
    ck = np.where((np.arange(K1) == 0) | (np.arange(K1) == N1 // 2), 1.0, 2.0)[None, :]
    binv = np.stack([ck * np.cos(angb), -ck * np.sin(angb)], axis=2).reshape(N1 // 2, 2 * K1) / N
    n2 = np.arange(N2, dtype=np.float64)
    ph = -2.0 * np.pi * (np.outer(n2, n2) % N2) / N2
    dre, dim = np.cos(ph), np.sin(ph)
    dst = np.block([[dre, -dim], [dim, dre]])
    f32 = lambda a: jnp.asarray(a.astype(np.float32))
    return dict(N1=N1, K1=K1, a1_full=f32(a1), a1_half=f32(a1[:, :N1 // 2]), binv=f32(binv),
                d=f32(dst), dt=f32(dst.T))


def _twiddle(k1, n_total, width):
    n2 = _iota2((DFT_N2, LANES), 0).astype(F32)
    ang = (-2.0 * math.pi / n_total) * (k1.astype(F32) * n2)
    reps = width // LANES
    return jnp.tile(jnp.cos(ang), (1, reps)), jnp.tile(jnp.sin(ang), (1, reps))


def _inner_forward(d, y, k1, n_total):
    h = DFT_N2
    c, s = _twiddle(k1, n_total, y.shape[1])
    yr, yi = y[:h], y[h:]
    return _dot(d, jnp.concatenate([yr * c - yi * s, yr * s + yi * c], axis=0), HI)


def _lmul_body(a_ref, x_ref, o_ref):
    o_ref[0] = _dot(a_ref[...], x_ref[0], HI)


def lmul(a, x, tile):
    B, K, W = x.shape
    R = a.shape[0]
    tile = min(tile, W)
    return pl.pallas_call(
        _lmul_body,
        grid=(B, W // tile),
        in_specs=[pl.BlockSpec((R, K), lambda b, j: (0, 0)),
                  pl.BlockSpec((1, K, tile), lambda b, j: (b, 0, j))],
        out_specs=pl.BlockSpec((1, R, tile), lambda b, j: (b, 0, j)),
        out_shape=jax.ShapeDtypeStruct((B, R, W), F32),
        compiler_params=_cparams("parallel", "parallel"),
        name="dft_outer",
    )(a, x)


def _dft_inner_body(d_ref, y_ref, o_ref, *, n_total):
    o_ref[0, 0] = _inner_forward(d_ref[...], y_ref[0, 0], pl.program_id(0), n_total)


def dft_inner(d, y, ct, n_total):
    B, K1, R, W = y.shape
    return pl.pallas_call(
        functools.partial(_dft_inner_body, n_total=n_total),
        grid=(K1, B, W // ct),
        in_specs=[pl.BlockSpec((R, R), lambda k, b, j: (0, 0)),
                  pl.BlockSpec((1, 1, R, ct), lambda k, b, j: (b, k, 0, j))],
        out_specs=pl.BlockSpec((1, 1, R, ct), lambda k, b, j: (b, k, 0, j)),
        out_shape=jax.ShapeDtypeStruct(y.shape, F32),
        compiler_params=_cparams("parallel", "parallel", "parallel"),
        name="dft_inner",
    )(d, y)


def _spec_mul_body(d_ref, dt_ref, y_ref, kf_ref, o_ref, *, n_total):
    k1 = pl.program_id(0)
    z = _inner_forward(d_ref[...], y_ref[0, 0], k1, n_total)
    h = DFT_N2
    zr, zi = z[:h], z[h:]
    kf = kf_ref[0, 0]
    kr, ki = kf[:h], kf[h:]
    q = _dot(dt_ref[...], jnp.concatenate([zr * kr - zi * ki, zr * ki + zi * kr], axis=0), HI)
    c, s = _twiddle(k1, n_total, q.shape[1])
    qr, qi = q[:h], q[h:]
    o_ref[0, 0] = jnp.concatenate([qr * c + qi * s, qi * c - qr * s], axis=0)


def spec_mul(d, dt, y, kf, order, C, ct, n_total):
    B, K1, R, _ = y.shape
    nc = C // ct
    return pl.pallas_call(
        functools.partial(_spec_mul_body, n_total=n_total),
        grid=(K1, B, nc),
        in_specs=[pl.BlockSpec((R, R), lambda k, b, j: (0, 0)),
                  pl.BlockSpec((R, R), lambda k, b, j: (0, 0)),
                  pl.BlockSpec((1, 1, R, ct), lambda k, b, j: (b, k, 0, j)),
                  pl.BlockSpec((1, 1, R, ct), lambda k, b, j: (0, k, 0, order * nc + j))],
        out_specs=pl.BlockSpec((1, 1, R, ct), lambda k, b, j: (b, k, 0, j)),
        out_shape=jax.ShapeDtypeStruct(y.shape, F32),
        compiler_params=_cparams("parallel", "parallel", "parallel"),
        name="hy_spec_mul",
    )(d, dt, y, kf)


def _dft_out_body(a_ref, t_ref, u_ref, g_ref, s_ref, o_ref):
    y = _dot(a_ref[...], t_ref[0], HI)
    u = u_ref[0]
    o_ref[0] = (g_ref[0] * (y + u * s_ref[...])).astype(o_ref.dtype)


def dft_out(binv, t, u, gate, skip_tiled, out_dtype):
    B, K, W = t.shape
    R = binv.shape[0]
    tile = skip_tiled.shape[1]
    return pl.pallas_call(
        _dft_out_body,
        grid=(B, W // tile),
        in_specs=[pl.BlockSpec((R, K), lambda b, j: (0, 0)),
                  pl.BlockSpec((1, K, tile), lambda b, j: (b, 0, j)),
                  pl.BlockSpec((1, R, tile), lambda b, j: (b, 0, j)),
                  pl.BlockSpec((1, R, tile), lambda b, j: (b, 0, j)),
                  pl.BlockSpec((1, tile), lambda b, j: (0, 0))],
        out_specs=pl.BlockSpec((1, R, tile), lambda b, j: (b, 0, j)),
        out_shape=jax.ShapeDtypeStruct((B, R, W), out_dtype),
        compiler_params=_cparams("parallel", "parallel"),
        name="dft_out",
    )(binv, t, u, gate, skip_tiled)


def hyena_mixer(p, C, conv_w, conv_b, f_w1, f_b1, f_freq1, f_w2, f_b2, f_freq2, f_w3, f_decay, skip):
    B, L, _ = p.shape
    cst = _dft_constants(L)
    N1, K1 = cst["N1"], cst["K1"]
    W = DFT_N2 * C
    lane_tile = 8 * C
    v, x1, x2 = hy_short_conv(p, conv_w, conv_b, C)
    kc = hy_filter(L, C, f_w1, f_b1, f_freq1, f_w2, f_b2, f_freq2, f_w3, f_decay)
    yf = lmul(cst["a1_full"], kc.reshape(1, N1, DFT_N2 * HY_ORDER * C), lane_tile)
    kf = dft_inner(cst["d"], yf.reshape(1, K1, 2 * DFT_N2, HY_ORDER * C), C, 2 * L)

    def long_conv(u, order, gate, out_dtype):
        y1 = lmul(cst["a1_half"], u.reshape(B, N1 // 2, W), lane_tile)
        t = spec_mul(cst["d"], cst["dt"], y1.reshape(B, K1, 2 * DFT_N2, C), kf, order, C, C, 2 * L)
        sk = jnp.tile(skip[order].reshape(1, C), (1, lane_tile // C))
        out = dft_out(cst["binv"], t.reshape(B, 2 * K1, W), u.reshape(B, N1 // 2, W),
                      gate.reshape(B, N1 // 2, W), sk, out_dtype)
        return out.reshape(B, L, C)

    z = long_conv(v, 0, x1, F32)
    return long_conv(z, 1, x2, BF16)


def _gla_chunk(q, k, v, lr, up, upb, st, reverse):
    C = GLA_CHUNK
    rr = _iota2((C, C), 0)
    cc = _iota2((C, C), 1)
    keep = (rr <= cc) if reverse else (rr >= cc)
    g = _log_sigmoid(_dot(lr, up, HI) + upb) / GLA_TAU
    yield None
    bc = _dot(keep.astype(F32), g, HI)
    yield None
    blast = bc[0:1] if reverse else bc[C - 1:C]
    q_in = (q * (GLA_DK ** -0.5) * jnp.exp(bc)).astype(BF16)
    k_in = (k * jnp.exp(-bc)).astype(BF16)
    k_end = (k * jnp.exp(blast - bc)).astype(BF16)
    vb = v.astype(BF16)
    att = _dot_nt(q_in, k_in)
    o_inter = _dot_nt(q_in, st.astype(BF16))
    st_new = st * jnp.exp(blast) + _dot_tn(vb, k_end)
    yield None
    o = _dot(jnp.where(keep, att, 0.0).astype(BF16), vb) + o_inter
    yield o, st_new


def _gla_body(qf_ref, kf_ref, vf_ref, lf_ref, qb_ref, kb_ref, vb_ref, lb_ref, up_ref, upb_ref,
              of_ref, ob_ref, st_ref, *, n_chunks):
    C = GLA_CHUNK

    @pl.when(pl.program_id(1) == 0)
    def _():
        st_ref[...] = jnp.zeros_like(st_ref)

    def chunk(step, carry):
        dirs = ((0, step, False, (qf_ref, kf_ref, vf_ref, lf_ref), of_ref),
                (1, n_chunks - 1 - step, True, (qb_ref, kb_ref, vb_ref, lb_ref), ob_ref))
        work = []
        for d, c, reverse, (q_ref, k_ref, v_ref, l_ref), o_ref in dirs:
            rows = pl.ds(pl.multiple_of(c * C, C), C)
            lr = l_ref[0, rows, :]
            for h in range(GLA_HEADS):
                kl = slice(h * GLA_DK, (h + 1) * GLA_DK)
                vl = slice(h * GLA_DV, (h + 1) * GLA_DV)
                vals = (q_ref[0, rows, kl], k_ref[0, rows, kl], v_ref[0, rows, vl], lr,
                        up_ref[d, :, kl], upb_ref[d, :, kl], st_ref[d, h])
                work.append((vals, reverse, o_ref, rows, vl, d, h))
        done = _interleave([_gla_chunk(*vals, reverse) for vals, reverse, *_ in work])
        for (o, st_new), (_, _, o_ref, rows, vl, d, h) in zip(done, work):
            o_ref[0, rows, vl] = o
            st_ref[d, h] = st_new
        return carry

    lax.fori_loop(0, n_chunks, chunk, 0)


def gla_scan(p, plr, up_pad, up_b, col0, tl=256):
    B, L, _ = p.shape
    H = GLA_HEADS
    nt = L // tl
    kw, vw = H * GLA_DK, H * GLA_DV
    qb = col0 * LANES // kw
    vb = (col0 * LANES + 2 * kw) // vw
    assert col0 * LANES % kw == 0 and (col0 * LANES + 2 * kw) % vw == 0
    specs = []
    for tmap in (lambda t: t, lambda t: nt - 1 - t):
        specs += [pl.BlockSpec((1, tl, kw), lambda b, t, tmap=tmap: (b, tmap(t), qb)),
                  pl.BlockSpec((1, tl, kw), lambda b, t, tmap=tmap: (b, tmap(t), qb + 1)),
                  pl.BlockSpec((1, tl, vw), lambda b, t, tmap=tmap: (b, tmap(t), vb)),
                  pl.BlockSpec((1, tl, LANES), lambda b, t, tmap=tmap: (b, tmap(t), 0))]
    specs += [pl.BlockSpec((2, LANES, kw), lambda b, t: (0, 0, 0)),
              pl.BlockSpec((2, 1, kw), lambda b, t: (0, 0, 0))]
    out = jax.ShapeDtypeStruct((B, L, vw), F32)
    return pl.pallas_call(
        functools.partial(_gla_body, n_chunks=tl // GLA_CHUNK),
        grid=(B, nt),
        in_specs=specs,
        out_specs=[pl.BlockSpec((1, tl, vw), lambda b, t: (b, t, 0)),
                   pl.BlockSpec((1, tl, vw), lambda b, t: (b, nt - 1 - t, 0))],
        out_shape=[out, out],
        scratch_shapes=[pltpu.VMEM((2, H, GLA_DV, GLA_DK), F32)],
        compiler_params=_cparams("parallel", "arbitrary"),
        name="gla_scan",
    )(p, p, p, plr, p, p, p, plr, up_pad, up_b)


def _gla_post_body(of_ref, ob_ref, r_ref, gn_ref, o_ref):
    o = of_ref[0] + ob_ref[0]
    o = o * lax.rsqrt(jnp.mean(o * o, axis=-1, keepdims=True) + NORM_EPS) * gn_ref[...]
    r = r_ref[0]
    o_ref[0] = (o * (r * jax.nn.sigmoid(r))).astype(o_ref.dtype)


def gla_post(o_f, o_b, p, r_block0, norm_g, tl=512):
    B, L, W = o_f.shape
    spec = pl.BlockSpec((1, tl, GLA_DV), lambda b, t, h: (b, t, h))
    return pl.pallas_call(
        _gla_post_body,
        grid=(B, L // tl, W // GLA_DV),
        in_specs=[spec, spec,
                  pl.BlockSpec((1, tl, GLA_DV), lambda b, t, h: (b, t, r_block0 + h)),
                  pl.BlockSpec((1, GLA_DV), lambda b, t, h: (0, h))],
        out_specs=spec,
        out_shape=jax.ShapeDtypeStruct((B, L, W), BF16),
        compiler_params=_cparams("parallel", "parallel", "parallel"),
        name="gla_post",
    )(o_f, o_b, p, norm_g.reshape(1, W))


def gla_mixer(p, plr, col0, up, up_b, norm_g):
    H = GLA_HEADS
    up_pad = jnp.zeros((2, LANES, H * GLA_DK), F32)
    up_pad = up_pad.at[0, 0:GLA_RANK].set(up[0]).at[1, GLA_RANK:2 * GLA_RANK].set(up[1])
    o_f, o_b = gla_scan(p, plr, up_pad, up_b.reshape(2, 1, H * GLA_DK), col0)
    r_block0 = (col0 * LANES + 2 * H * GLA_DK + H * GLA_DV) // GLA_DV
    return gla_post(o_f, o_b, p, r_block0, norm_g)


def _rw_mix_body(x_ref, xm_ref, xp_ref, g_ref, sc_ref, sh_ref, mu_ref, *o_refs, n_tiles):
    i = pl.program_id(1)
    g, sc, sh = g_ref[...], sc_ref[0], sh_ref[0]
    h = _norm_mod_value(x_ref[0], g, sc, sh)
    hm = _norm_mod_value(xm_ref[0], g, sc, sh)
    hp = _norm_mod_value(xp_ref[0], g, sc, sh)
    hm1, hp1 = _neighbours(h, hm, hp, i, n_tiles)
    xx = 0.5 * (hm1 + hp1) - h
    mu = mu_ref[...]
    for j, o_ref in enumerate(o_refs):
        o_ref[0] = (h + xx * mu[j:j + 1]).astype(o_ref.dtype)


def rw_mix(x, g, scale, shift, mu, tl=256):
    B, L, D = x.shape
    n_tiles = L // tl
    out = jax.ShapeDtypeStruct((B, L, D), BF16)
    ospec = pl.BlockSpec((1, tl, D), lambda b, i: (b, i, 0))
    mu8 = jnp.zeros((8, D), F32).at[:6].set(mu)
    return pl.pallas_call(
        functools.partial(_rw_mix_body, n_tiles=n_tiles),
        grid=(B, n_tiles),
        in_specs=_neighbour_specs(tl, D, L // 8) + [pl.BlockSpec((1, D), lambda b, i: (0, 0)),
                                                    pl.BlockSpec((1, 1, D), lambda b, i: (b, 0, 0)),
                                                    pl.BlockSpec((1, 1, D), lambda b, i: (b, 0, 0)),
                                                    pl.BlockSpec((8, D), lambda b, i: (0, 0))],
        out_specs=[ospec] * 6,
        out_shape=[out] * 6,
        compiler_params=_cparams("parallel", "parallel"),
        name="rw_mix",
    )(x, x, x, g.reshape(1, D), scale, shift, mu8)


def _head_ones():
    r = _iota2((LANES, LANES), 0) // RW_HEAD
    c = _iota2((LANES, LANES), 1) // RW_HEAD
    return (r == c).astype(F32)


def _rw_prep_body(k_ref, r_ref, v_ref, a0_ref, a1_ref, kk_w_ref, ka_ref, rk_ref, kk_ref, bon_ref):
    ones = _head_ones()
    k = k_ref[...]
    kk = k * kk_w_ref[...]
    ss = _dot(kk * kk, ones, HI)
    kk_ref[...] = kk * lax.rsqrt(jnp.maximum(ss, 1e-24))
    a_sum = jax.nn.sigmoid(a0_ref[...]) + jax.nn.sigmoid(a1_ref[...])
    kd_sum = k * (2.0 + (a_sum - 2.0) * ka_ref[...])
    bon_ref[...] = _dot(r_ref[...] * kd_sum * rk_ref[...], ones, HI) * v_ref[...]


def rw_prep(k, r, v, apre0, apre1, k_k, k_a, r_k, tm=2048):
    T, D = k.shape
    tm = min(tm, T)
    spec = pl.BlockSpec((tm, LANES), lambda i, j: (i, j))
    pspec = pl.BlockSpec((1, LANES), lambda i, j: (0, j))
    out = jax.ShapeDtypeStruct((T, D), F32)
    return pl.pallas_call(
        _rw_prep_body,
        grid=(T // tm, D // LANES),
        in_specs=[spec, spec, spec, spec, spec, pspec, pspec, pspec],
        out_specs=[spec, spec],
        out_shape=[out, out],
        compiler_params=_cparams("parallel", "parallel"),
        name="rw_prep",
    )(k, r, v, apre0, apre1, k_k.reshape(1, D), k_a.reshape(1, D), r_k.reshape(1, D))


def _rw_chunk(r, k, v, kk, wpre, apre, ka, h, reverse):
    C = RW_CHUNK
    P = 2 * C
    rc = _iota2((C, C), 0)
    cc = _iota2((C, C), 1)
    tri = ((rc <= cc) if reverse else (rc >= cc)).astype(F32)
    rp = _iota2((P, P), 0)
    cp = _iota2((P, P), 1)
    strict = (rp < cp) if reverse else (rp > cp)
    incl = (rp <= cp) if reverse else (rp >= cp)
    head0 = _iota2((C, LANES), 1) < RW_HEAD

    def stack(x):
        return jnp.concatenate([jnp.where(head0, x, 0.0), jnp.where(head0, 0.0, x)], axis=0).astype(BF16)

    lw = -RW_LOG_DECAY_MAX * jax.nn.sigmoid(wpre)
    a = jax.nn.sigmoid(apre)
    kd = k * (1.0 + (a - 1.0) * ka)
    b = kk * a
    cum = _dot(tri, lw, HI)
    yield None
    cum_x = cum - lw
    total = cum[0:1] if reverse else cum[C - 1:C]
    e_neg = jnp.exp(-cum)
    rs = stack(r * jnp.exp(cum))
    xs = stack(kk * jnp.exp(cum_x))
    ks = stack(kd * e_neg)
    bs = stack(b * e_neg)
    vs = stack(v)
    hb = h.astype(BF16)
    xr = jnp.concatenate([xs, rs], axis=0)
    coef = _dot_nt(xr, jnp.concatenate([ks, bs], axis=0))
    xr_h = _dot(xr, hb)
    kv = _dot_tn(ks, vs)
    yield None
    a_k = jnp.where(strict, coef[:P, :P], 0.0)
    q_k = jnp.where(incl, coef[P:, :P], 0.0)
    q_b = jnp.where(incl, coef[P:, P:], 0.0).astype(BF16)
    aq_v = _dot(jnp.concatenate([a_k, q_k], axis=0).astype(BF16), vs)
    n = jnp.where(strict, -coef[:P, P:], 0.0)
    nb = n.astype(BF16)
    pw = _dot(nb, nb)
    tinv = (rp == cp).astype(F32) + n
    yield None
    for _ in range(int(math.log2(C)) - 2):
        both = _dot(jnp.concatenate([tinv, pw], axis=0).astype(BF16), pw.astype(BF16))
        tinv = tinv + both[:P]
        pw = both[P:]
        yield None
    tinv = tinv + _dot(tinv.astype(BF16), pw.astype(BF16))
    yield None
    u = _dot(tinv.astype(BF16), (xr_h[:P] + aq_v[:P]).astype(BF16)).astype(BF16)
    yield None
    os = xr_h[P:] + aq_v[P:] - _dot(q_b, u)
    gcol = jnp.transpose(jnp.broadcast_to(jnp.exp(total), (LANES, LANES)))
    h_new = gcol * (h + kv - _dot_tn(bs, u))
    yield os[:C] + os[C:], h_new


def _interleave(chains):
    while True:
        outs = [next(ch) for ch in chains]
        if outs[0] is not None:
            return outs


def _rw_scan_body(rf_ref, kf_ref, vf_ref, kkf_ref, wf_ref, af_ref, rb_ref, kb_ref, vb_ref, kkb_ref, wb_ref,
                  ab_ref, ka_ref, of_ref, ob_ref, h_ref, *, n_chunks, n_pairs):
    C = RW_CHUNK

    @pl.when(pl.program_id(2) == 0)
    def _():
        h_ref[...] = jnp.zeros_like(h_ref)

    def chunk(step, carry):
        dirs = ((0, step, False, (rf_ref, kf_ref, vf_ref, kkf_ref, wf_ref, af_ref), of_ref),
                (1, n_chunks - 1 - step, True, (rb_ref, kb_ref, vb_ref, kkb_ref, wb_ref, ab_ref), ob_ref))
        work = []
        for d, c, reverse, refs, o_ref in dirs:
            rows = pl.ds(pl.multiple_of(c * C, C), C)
            for g in range(n_pairs):
                lanes = slice(g * LANES, (g + 1) * LANES)
                vals = [ref[0, rows, lanes] for ref in refs]
                work.append((vals, ka_ref[:, lanes], h_ref[d, g], reverse, o_ref, rows, lanes, d, g))
        done = _interleave([_rw_chunk(*vals, ka, h, reverse) for vals, ka, h, reverse, *_ in work])
        for (o, h_new), (_, _, _, _, o_ref, rows, lanes, d, g) in zip(done, work):
            o_ref[0, rows, lanes] = o
            h_ref[d, g] = h_new
        return carry

    lax.fori_loop(0, n_chunks, chunk, 0)


def rw_scan(r, k, v, kk, wpre, apre, k_a, tl=512, n_pairs=2):
    B, L, D = r.shape
    width = n_pairs * LANES
    nt = L // tl
    fspec = pl.BlockSpec((1, tl, width), lambda b, p, t: (b, t, p))
    bspec = pl.BlockSpec((1, tl, width), lambda b, p, t: (b, nt - 1 - t, p))
    out = jax.ShapeDtypeStruct((B, L, D), F32)
    return pl.pallas_call(
        functools.partial(_rw_scan_body, n_chunks=tl // RW_CHUNK, n_pairs=n_pairs),
        grid=(B, D // width, nt),
        in_specs=[fspec] * 6 + [bspec] * 6 + [pl.BlockSpec((1, width), lambda b, p, t: (0, p))],
        out_specs=[fspec, bspec],
        out_shape=[out, out],
        scratch_shapes=[pltpu.VMEM((2, n_pairs, LANES, LANES), F32)],
        compiler_params=_cparams("parallel", "parallel", "arbitrary"),
        name="rw_scan",
    )(r, k, v, kk, wpre[0], apre[0], r, k, v, kk, wpre[1], apre[1], k_a.reshape(1, D))


def _rw_post_body(of_ref, ob_ref, bon_ref, g_ref, gng_ref, gnb_ref, o_ref):
    ones = _head_ones() * (1.0 / RW_HEAD)
    so = of_ref[...] + ob_ref[...]
    mean = _dot(so, ones, HI)
    dlt = so - mean
    var = _dot(dlt * dlt, ones, HI)
    y = dlt * lax.rsqrt(var + RW_GN_EPS) * gng_ref[...] + gnb_ref[...]
    o_ref[...] = ((y + bon_ref[...]) * g_ref[...]).astype(o_ref.dtype)


def rw_post(o_f, o_b, bonus, g, gn_g, gn_b, tm=2048):
    T, D = o_f.shape
    tm = min(tm, T)
    spec = pl.BlockSpec((tm, LANES), lambda i, j: (i, j))
    pspec = pl.BlockSpec((1, LANES), lambda i, j: (0, j))
    return pl.pallas_call(
        _rw_post_body,
        grid=(T // tm, D // LANES),
        in_specs=[spec, spec, spec, spec, pspec, pspec],
        out_specs=spec,
        out_shape=jax.ShapeDtypeStruct((T, D), BF16),
        compiler_params=_cparams("parallel", "parallel"),
        name="rw_post",
    )(o_f, o_b, bonus, g, gn_g.reshape(1, D), gn_b.reshape(1, D))


def _pad_cols(w, n):
    return jnp.pad(w, ((0, 0), (0, n - w.shape[1])))


def _pad_rows(w, n):
    return jnp.pad(w, ((0, n - w.shape[0]), (0, 0)))


def rwkv7_block(x, gate, g_norm, scale, shift, mu, w_r, w_k, w_v, w0, w1, w2, a0, a1, a2, g1, g2,
                k_k, k_a, r_k, gn_g, gn_b, w_o):
    B, L, D = x.shape
    T = B * L
    xr, xw, xk, xv, xa, xg = [a.reshape(T, D) for a in rw_mix(x, g_norm, scale, shift, mu)]
    r = matmul(xr, w_r.astype(BF16))
    k = matmul(xk, w_k.astype(BF16))
    v = matmul(xv, w_v.astype(BF16))
    w1c = jnp.concatenate([_pad_cols(w1[0], LANES), _pad_cols(w1[1], LANES)], axis=1).astype(BF16)
    a1c = jnp.concatenate([_pad_cols(a1[0], LANES), _pad_cols(a1[1], LANES)], axis=1).astype(BF16)
    wmid = matmul(xw, w1c)
    amid = matmul(xa, a1c)
    gmid = matmul(xg, g1.astype(BF16))
    wpre = [matmul(wmid, _pad_rows(w2[d], LANES).astype(BF16), a_act="tanh", bias=w0[d], a_col_block=d)
            for d in range(2)]
    apre = [matmul(amid, _pad_rows(a2[d], LANES).astype(BF16), bias=a0[d], a_col_block=d)
            for d in range(2)]
    g = matmul(gmid, g2.astype(BF16), a_act="sigmoid")
    kk, bonus = rw_prep(k, r, v, apre[0], apre[1], k_k, k_a, r_k.reshape(D))
    sh = lambda a: a.reshape(B, L, D)
    o_f, o_b = rw_scan(sh(r), sh(k), sh(v), sh(kk), [sh(a) for a in wpre], [sh(a) for a in apre], k_a)
    y = rw_post(o_f.reshape(T, D), o_b.reshape(T, D), bonus, g, gn_g, gn_b)
    out = matmul(y, w_o.astype(BF16), res=x.reshape(T, D), gate=gate, rows_per_batch=L)
    return out.reshape(B, L, D)


def ffn_block(x, c, norm_g, ada_w, ada_b, idx, w1, w2):
    B, L, D = x.shape
    T = B * L
    shift, scale, gate = ada_mod(c, ada_w, ada_b, idx)
    h = norm_mod(x, norm_g, scale, shift).reshape(T, D)
    hid = matmul(h, w1.astype(BF16), out_dtype=BF16, epi="sqrelu")
    out = matmul(hid, w2.astype(BF16), res=x.reshape(T, D), gate=gate, rows_per_batch=L, tm=512, tn=512)
    return out.reshape(B, L, D)


def hyena_gla_block(x, gate, h, w_in, w_out, conv_w, conv_b, f_w1, f_b1, f_freq1, f_w2, f_b2, f_freq2,
                    f_w3, f_decay, hy_skip, gla_up, gla_up_b, gla_norm_g):
    B, L, D = x.shape
    T = B * L
    C = conv_w.shape[1] // 3
    n_main = w_in.shape[1] - 2 * GLA_RANK
    p = matmul(h, w_in[:, :n_main].astype(BF16)).reshape(B, L, n_main)
    plr = matmul(h, _pad_cols(w_in[:, n_main:], LANES).astype(BF16)).reshape(B, L, LANES)
    y_hy = hyena_mixer(p, C, conv_w, conv_b, f_w1, f_b1, f_freq1, f_w2, f_b2, f_freq2, f_w3, f_decay, hy_skip)
    y_gla = gla_mixer(p, plr, 3 * C // LANES, gla_up, gla_up_b, gla_norm_g)
    y = jnp.concatenate([y_hy, y_gla], axis=-1).reshape(T, -1)
    out = matmul(y, w_out.astype(BF16), res=x.reshape(T, D), gate=gate, rows_per_batch=L)
    return out.reshape(B, L, D)


def kernel(x, c, mix_norm_g, mix_ada_w, mix_ada_b, ab_w_in, ab_w_out, hy_conv_w, hy_conv_b, hy_ffn_w1, hy_ffn_b1, hy_freq1, hy_ffn_w2, hy_ffn_b2, hy_freq2, hy_ffn_w3, hy_decay, hy_skip, gla_up, gla_up_b, gla_norm_g, tm_norm_g, tm_ada_w, tm_ada_b, rw_mu, rw_w_r, rw_w_k, rw_w_v, rw_w0, rw_w1, rw_w2, rw_a0, rw_a1, rw_a2, rw_g1, rw_g2, rw_k_k, rw_k_a, rw_r_k, rw_gn_g, rw_gn_b, rw_w_o, ffn_norm_g, ffn_ada_w, ffn_ada_b, ffn_w1, ffn_w2, final_norm_g):
    B, L, D = x.shape
    depth = ffn_w1.shape[0]
    for layer in range(depth):
        i = layer // 2
        if layer % 2 == 0:
            shift, scale, gate = ada_mod(c, mix_ada_w, mix_ada_b, i)
            h = norm_mod(x, mix_norm_g[i], scale, shift).reshape(B * L, D)
            x = hyena_gla_block(x, gate, h, ab_w_in[i], ab_w_out[i], hy_conv_w[i], hy_conv_b[i],
                                hy_ffn_w1[i], hy_ffn_b1[i], hy_freq1[i], hy_ffn_w2[i], hy_ffn_b2[i],
                                hy_freq2[i], hy_ffn_w3[i], hy_decay[i], hy_skip[i],
                                gla_up[i], gla_up_b[i], gla_norm_g[i])
        else:
            shift, scale, gate = ada_mod(c, tm_ada_w, tm_ada_b, i)
            x = rwkv7_block(x, gate, tm_norm_g[i], scale, shift, rw_mu[i], rw_w_r[i], rw_w_k[i], rw_w_v[i],
                            rw_w0[i], rw_w1[i], rw_w2[i], rw_a0[i], rw_a1[i], rw_a2[i], rw_g1[i], rw_g2[i],
                            rw_k_k[i], rw_k_a[i], rw_r_k[i], rw_gn_g[i], rw_gn_b[i], rw_w_o[i])
        x = ffn_block(x, c, ffn_norm_g[layer], ffn_ada_w, ffn_ada_b, layer, ffn_w1[layer], ffn_w2[layer])
    return rms_final(x, final_norm_g)
```

```python
import functools
import math

import numpy as np
import jax
import jax.numpy as jnp
from jax import lax
from jax.experimental import pallas as pl
from jax.experimental.pallas import tpu as pltpu

F32 = jnp.float32
BF16 = jnp.bfloat16
HI = lax.Precision.HIGHEST

LANES = 128
SUBLANES = 8
NORM_EPS = 1e-6
GLA_HEADS = 4
GLA_DK = 128
GLA_DV = 256
GLA_RANK = 16
GLA_TAU = 16.0
GLA_CHUNK = 64
RW_HEAD = 64
RW_CHUNK = 64
RW_LOG_DECAY_MAX = 0.606531
RW_GN_EPS = 64e-5
HY_ORDER = 2
HY_EMB = 33
HY_FFN = 64
DFT_N2 = 128
VMEM_LIMIT = 52 * 1024 * 1024


def _cparams(*sem):
    return pltpu.CompilerParams(dimension_semantics=sem, vmem_limit_bytes=VMEM_LIMIT)


def _dot(a, b, precision=None):
    return jnp.dot(a, b, preferred_element_type=F32, precision=precision)


def _dot_nt(a, b, precision=None):
    return lax.dot_general(a, b, (((1,), (1,)), ((), ())), preferred_element_type=F32, precision=precision)


def _dot_tn(a, b, precision=None):
    return lax.dot_general(a, b, (((0,), (0,)), ((), ())), preferred_element_type=F32, precision=precision)


def _iota2(shape, axis):
    return lax.broadcasted_iota(jnp.int32, shape, axis)


def _log_sigmoid(x):
    return jnp.minimum(x, 0.0) - jnp.log(1.0 + jnp.exp(-jnp.abs(x)))


def _split(x):
    hi = x.astype(BF16)
    return hi, (x - hi.astype(F32)).astype(BF16)


def _mask_dot(mask_bf16, x):
    hi, lo = _split(x)
    n = x.shape[1]
    y = _dot(mask_bf16, jnp.concatenate([hi, lo], axis=1))
    return y[:, :n] + y[:, n:]


def _stack3(a):
    a = np.asarray(a, np.float32)
    hi = a.astype(BF16)
    lo = (a - hi.astype(np.float32)).astype(BF16)
    return jnp.asarray(np.concatenate([hi, hi, lo], axis=1))


def _dot3(a3, x):
    hi, lo = _split(x)
    return _dot(a3, jnp.concatenate([hi, lo, hi], axis=0))


def _dot_mask(x, mask_bf16):
    hi, lo = _split(x)
    return _dot(hi, mask_bf16) + _dot(lo, mask_bf16)


def _ada_body(c_ref, w_ref, b_ref, o_ref):
    c = c_ref[...]
    s = c * jax.nn.sigmoid(c)
    o_ref[...] = _dot(s, w_ref[0], HI) + b_ref[0]


def ada_mod(c, w, b, idx):
    B, D = c.shape
    n3 = w.shape[2]
    tn = 512
    out = pl.pallas_call(
        _ada_body,
        grid=(n3 // tn,),
        in_specs=[pl.BlockSpec((B, D), lambda j: (0, 0)),
                  pl.BlockSpec((1, D, tn), lambda j: (idx, 0, j)),
                  pl.BlockSpec((1, 1, tn), lambda j: (idx, 0, j))],
        out_specs=pl.BlockSpec((B, tn), lambda j: (0, j)),
        out_shape=jax.ShapeDtypeStruct((B, n3), F32),
        compiler_params=_cparams("parallel"),
        name="ada_mod",
    )(c, w, b.reshape(b.shape[0], 1, n3))
    shift, scale, gate = jnp.split(out[:, None, :], 3, axis=-1)
    return shift, scale, gate


def _norm_mod_value(x, g, scale, shift):
    y = x * lax.rsqrt(jnp.mean(x * x, axis=-1, keepdims=True) + NORM_EPS)
    return (y * g) * (1.0 + scale) + shift


def _norm_mod_body(x_ref, g_ref, sc_ref, sh_ref, o_ref):
    o_ref[0] = _norm_mod_value(x_ref[0], g_ref[...], sc_ref[0], sh_ref[0]).astype(o_ref.dtype)


def norm_mod(x, g, scale, shift, out_dtype=BF16, tm=512):
    B, L, D = x.shape
    return pl.pallas_call(
        _norm_mod_body,
        grid=(B, L // tm),
        in_specs=[pl.BlockSpec((1, tm, D), lambda b, i: (b, i, 0)),
                  pl.BlockSpec((1, D), lambda b, i: (0, 0)),
                  pl.BlockSpec((1, 1, D), lambda b, i: (b, 0, 0)),
                  pl.BlockSpec((1, 1, D), lambda b, i: (b, 0, 0))],
        out_specs=pl.BlockSpec((1, tm, D), lambda b, i: (b, i, 0)),
        out_shape=jax.ShapeDtypeStruct((B, L, D), out_dtype),
        compiler_params=_cparams("parallel", "parallel"),
        name="norm_mod",
    )(x, g.reshape(1, D), scale, shift)


def _rms_body(x_ref, g_ref, o_ref):
    x = x_ref[0]
    y = x * lax.rsqrt(jnp.mean(x * x, axis=-1, keepdims=True) + NORM_EPS)
    o_ref[0] = y * g_ref[...]


def rms_final(x, g, tm=512):
    B, L, D = x.shape
    return pl.pallas_call(
        _rms_body,
        grid=(B, L // tm),
        in_specs=[pl.BlockSpec((1, tm, D), lambda b, i: (b, i, 0)),
                  pl.BlockSpec((1, D), lambda b, i: (0, 0))],
        out_specs=pl.BlockSpec((1, tm, D), lambda b, i: (b, i, 0)),
        out_shape=jax.ShapeDtypeStruct((B, L, D), F32),
        compiler_params=_cparams("parallel", "parallel"),
        name="rms_final",
    )(x, g.reshape(1, D))


def _mm_body(*refs, nk, a_act, epi, has_bias, has_res):
    it = iter(refs)
    a_ref = next(it)
    w_ref = next(it)
    bias_ref = next(it) if has_bias else None
    res_ref = next(it) if has_res else None
    gate_ref = next(it) if has_res else None
    o_ref = next(it)
    acc_ref = next(it) if nk > 1 else None

    a = a_ref[...]
    if a_act == "tanh":
        a = jnp.tanh(a.astype(F32))
    elif a_act == "sigmoid":
        a = jax.nn.sigmoid(a.astype(F32))
    part = _dot(a.astype(BF16), w_ref[...])

    def finish(acc):
        if has_bias:
            acc = acc + bias_ref[...]
        if epi == "sqrelu":
            acc = jnp.square(jnp.maximum(acc, 0.0))
        if has_res:
            acc = res_ref[...] + gate_ref[0] * acc
        o_ref[...] = acc.astype(o_ref.dtype)

    if nk == 1:
        finish(part)
    else:
        k = pl.program_id(2)

        @pl.when(k == 0)
        def _():
            acc_ref[...] = part

        @pl.when(k > 0)
        def _():
            acc_ref[...] += part

        @pl.when(k == nk - 1)
        def _():
            finish(acc_ref[...])


def matmul(a, w, *, out_dtype=F32, a_act=None, epi=None, bias=None, res=None, gate=None,
           rows_per_batch=None, tm=1024, tn=1024, tk=None, a_col_block=0):
    M = a.shape[0]
    K, N = w.shape
    tm = min(tm, M if rows_per_batch is None else rows_per_batch)
    tn = min(tn, N)
    tk = K if tk is None else min(tk, K)
    nk = K // tk
    assert M % tm == 0 and N % tn == 0 and K % tk == 0
    has_bias = bias is not None
    has_res = res is not None
    in_specs = [pl.BlockSpec((tm, tk), lambda i, j, k: (i, k + a_col_block * nk)),
                pl.BlockSpec((tk, tn), lambda i, j, k: (k, j))]
    args = [a, w]
    if has_bias:
        in_specs.append(pl.BlockSpec((1, tn), lambda i, j, k: (0, j)))
        args.append(bias.reshape(1, N).astype(F32))
    if has_res:
        tiles_per_batch = rows_per_batch // tm
        assert rows_per_batch % tm == 0
        in_specs.append(pl.BlockSpec((tm, tn), lambda i, j, k: (i, j)))
        in_specs.append(pl.BlockSpec((1, 1, tn), lambda i, j, k: (i // tiles_per_batch, 0, j)))
        args += [res, gate]
    scratch = [pltpu.VMEM((tm, tn), F32)] if nk > 1 else []
    body = functools.partial(_mm_body, nk=nk, a_act=a_act, epi=epi, has_bias=has_bias, has_res=has_res)
    return pl.pallas_call(
        body,
        grid=(M // tm, N // tn, nk),
        in_specs=in_specs,
        out_specs=pl.BlockSpec((tm, tn), lambda i, j, k: (i, j)),
        out_shape=jax.ShapeDtypeStruct((M, N), out_dtype),
        scratch_shapes=scratch,
        compiler_params=_cparams("parallel", "parallel", "arbitrary"),
        name="matmul",
    )(*args)


def _neighbours(cur, prev8, next8, i, n_tiles):
    t = cur.shape[0]
    row = _iota2(cur.shape, 0)
    before = jnp.where(i > 0, prev8[7:8, :], 0.0)
    after = jnp.where(i < n_tiles - 1, next8[0:1, :], 0.0)
    xm1 = jnp.where(row == 0, before, pltpu.roll(cur, 1, 0))
    xp1 = jnp.where(row == t - 1, after, pltpu.roll(cur, t - 1, 0))
    return xm1, xp1


def _neighbour_specs(tl, width, n_rows8, col_map=None):
    r8 = tl // 8
    return [pl.BlockSpec((1, tl, width), lambda b, i: (b, i, 0)),
            pl.BlockSpec((1, 8, width), lambda b, i: (b, jnp.maximum(i * r8 - 1, 0), 0)),
            pl.BlockSpec((1, 8, width), lambda b, i: (b, jnp.minimum((i + 1) * r8, n_rows8 - 1), 0))]


def _hy_conv_body(p_ref, pm_ref, pp_ref, w_ref, b_ref, v_ref, x1_ref, x2_ref, *, n_tiles, C):
    i = pl.program_id(1)
    cur = p_ref[0]
    xm1, xp1 = _neighbours(cur, pm_ref[0], pp_ref[0], i, n_tiles)
    w = w_ref[...]
    u = xm1 * w[0:1] + cur * w[1:2] + xp1 * w[2:3] + b_ref[...]
    v_ref[0] = u[:, 0:C]
    x1_ref[0] = u[:, C:2 * C]
    x2_ref[0] = u[:, 2 * C:3 * C]


def hy_short_conv(p, conv_w, conv_b, C, tl=256):
    B, L, _ = p.shape
    W = 3 * C
    n_tiles = L // tl
    out = jax.ShapeDtypeStruct((B, L, C), F32)
    ospec = pl.BlockSpec((1, tl, C), lambda b, i: (b, i, 0))
    return pl.pallas_call(
        functools.partial(_hy_conv_body, n_tiles=n_tiles, C=C),
        grid=(B, n_tiles),
        in_specs=_neighbour_specs(tl, W, L // 8) + [pl.BlockSpec((3, W), lambda b, i: (0, 0)),
                                                    pl.BlockSpec((1, W), lambda b, i: (0, 0))],
        out_specs=[ospec, ospec, ospec],
        out_shape=[out, out, out],
        compiler_params=_cparams("parallel", "parallel"),
        name="hy_short_conv",
    )(p, p, p, conv_w, conv_b.reshape(1, W))


def _hy_filter_body(z_ref, w1_ref, b1_ref, f1_ref, w2_ref, b2_ref, f2_ref, w3_ref, dec_ref, o_ref, *, L, tm):
    i = pl.program_id(0)
    z = z_ref[...]
    f = jnp.sin(f1_ref[...] * (_dot(z, w1_ref[...], HI) + b1_ref[...]))
    f = jnp.sin(f2_ref[...] * (_dot(f, w2_ref[...], HI) + b2_ref[...]))
    t = z[:, 0:1]
    f = _dot(f, w3_ref[...], HI) * jnp.exp(-t * jnp.abs(dec_ref[...]))
    n = i * tm + _iota2(f.shape, 0)
    o_ref[...] = jnp.where(n == L, 0.0, f)


def hy_filter(L, C, w1, b1, freq1, w2, b2, freq2, w3, decay, tm=512):
    pos = np.concatenate([np.arange(L), [0], np.arange(L - 1, 0, -1)]).astype(np.float64)
    t = pos / (L - 1)
    ang = 2.0 * math.pi * pos / L
    nb = (HY_EMB - 1) // 2
    ba = np.linspace(1e-4, nb - 1, nb)[None, :] * ang[:, None]
    z = np.concatenate([t[:, None], np.cos(ba), -np.sin(ba)], axis=-1)
    zp = np.zeros((2 * L, 2 * HY_FFN), np.float32)
    zp[:, :HY_EMB] = z
    w1p = jnp.zeros((2 * HY_FFN, HY_FFN), F32).at[:HY_EMB].set(w1)
    nL = L // tm
    row = lambda a: a.reshape(1, -1)
    return pl.pallas_call(
        functools.partial(_hy_filter_body, L=L, tm=tm),
        grid=(2 * L // tm, HY_ORDER),
        in_specs=[pl.BlockSpec((tm, 2 * HY_FFN), lambda i, o: (i, 0)),
                  pl.BlockSpec((2 * HY_FFN, HY_FFN), lambda i, o: (0, 0)),
                  pl.BlockSpec((1, HY_FFN), lambda i, o: (0, 0)),
                  pl.BlockSpec((1, HY_FFN), lambda i, o: (0, 0)),
                  pl.BlockSpec((HY_FFN, HY_FFN), lambda i, o: (0, 0)),
                  pl.BlockSpec((1, HY_FFN), lambda i, o: (0, 0)),
                  pl.BlockSpec((1, HY_FFN), lambda i, o: (0, 0)),
                  pl.BlockSpec((HY_FFN, C), lambda i, o: (0, 2 * o + i // nL)),
                  pl.BlockSpec((1, C), lambda i, o: (0, 2 * o + i // nL))],
        out_specs=pl.BlockSpec((tm, C), lambda i, o: (i, o)),
        out_shape=jax.ShapeDtypeStruct((2 * L, HY_ORDER * C), F32),
        compiler_params=_cparams("parallel", "parallel"),
        name="hy_filter",
    )(jnp.asarray(zp), w1p, row(b1), row(freq1), w2, row(b2), row(freq2), w3, row(decay))


def _dft_constants(L):
    N = 2 * L
    N2 = DFT_N2
    N1 = N // N2
    K1 = N1 // 2 + 1
    k1 = np.arange(K1, dtype=np.float64)
    n1 = np.arange(N1, dtype=np.float64)
    ang = 2.0 * np.pi * np.outer(k1, n1) / N1
    a1 = np.stack([np.cos(ang), -np.sin(ang)], axis=1).reshape(2 * K1, N1)
    n1h = np.arange(N1 // 2, dtype=np.float64)
    angb = 2.0 * np.pi * np.outer(n1h, k1) / N1
    ck = np.where((np.arange(K1) == 0) | (np.arange(K1) == N1 // 2), 1.0, 2.0)[None, :]
    binv = np.stack([ck * np.cos(angb), -ck * np.sin(angb)], axis=2).reshape(N1 // 2, 2 * K1) / N
    n2 = np.arange(N2, dtype=np.float64)
    ph = -2.0 * np.pi * (np.outer(n2, n2) % N2) / N2
    dre, dim = np.cos(ph), np.sin(ph)
    dst = np.block([[dre, -dim], [dim, dre]])
    kron = lambda a: _stack3(np.kron(a, np.eye(SUBLANES)))
    return dict(N1=N1, K1=K1, a1_full=kron(a1), a1_half=kron(a1[:, :N1 // 2]), binv=kron(binv),
                d=_stack3(dst), dt=_stack3(dst.T))


def _twiddle(k1, n_total, width):
    n2 = _iota2((DFT_N2, LANES), 0).astype(F32)
    ang = (-2.0 * math.pi / n_total) * (k1.astype(F32) * n2)
    reps = width // LANES
    return jnp.tile(jnp.cos(ang), (1, reps)), jnp.tile(jnp.sin(ang), (1, reps))


def _inner_forward(d, y, k1, n_total):
    h = DFT_N2
    c, s = _twiddle(k1, n_total, y.shape[1])
    yr, yi = y[:h], y[h:]
    return _dot3(d, jnp.concatenate([yr * c - yi * s, yr * s + yi * c], axis=0))


def _dft_outer_body(a_ref, x_ref, o_ref):
    _, k, s, ct = x_ref.shape
    y = _dot3(a_ref[...], x_ref[0].reshape(k * s, ct))
    o_ref[0] = y.reshape(o_ref.shape[1], s, ct)


def dft_outer(a_kron, x, ct):
    B, K, N2, W = x.shape
    R = a_kron.shape[0] // SUBLANES
    ct = min(ct, W)
    return pl.pallas_call(
        _dft_outer_body,
        grid=(B, N2 // SUBLANES, W // ct),
        in_specs=[pl.BlockSpec(a_kron.shape, lambda b, s, j: (0, 0)),
                  pl.BlockSpec((1, K, SUBLANES, ct), lambda b, s, j: (b, 0, s, j))],
        out_specs=pl.BlockSpec((1, R, SUBLANES, ct), lambda b, s, j: (b, 0, s, j)),
        out_shape=jax.ShapeDtypeStruct((B, R, N2, W), F32),
        compiler_params=_cparams("parallel", "parallel", "parallel"),
        name="dft_outer",
    )(a_kron, x)


def _dft_inner_body(d_ref, y_ref, o_ref, *, n_total):
    o_ref[0, 0] = _inner_forward(d_ref[...], y_ref[0, 0], pl.program_id(0), n_total)


def dft_inner(d, y, ct, n_total):
    B, K1, R, W = y.shape
    return pl.pallas_call(
        functools.partial(_dft_inner_body, n_total=n_total),
        grid=(K1, B, W // ct),
        in_specs=[pl.BlockSpec(d.shape, lambda k, b, j: (0, 0)),
                  pl.BlockSpec((1, 1, R, ct), lambda k, b, j: (b, k, 0, j))],
        out_specs=pl.BlockSpec((1, 1, R, ct), lambda k, b, j: (b, k, 0, j)),
        out_shape=jax.ShapeDtypeStruct(y.shape, F32),
        compiler_params=_cparams("parallel", "parallel", "parallel"),
        name="dft_inner",
    )(d, y)


def _spec_mul_body(d_ref, dt_ref, y_ref, kf_ref, o_ref, *, n_total):
    k1 = pl.program_id(0)
    z = _inner_forward(d_ref[...], y_ref[0, 0], k1, n_total)
    h = DFT_N2
    zr, zi = z[:h], z[h:]
    kf = kf_ref[0, 0]
    kr, ki = kf[:h], kf[h:]
    q = _dot3(dt_ref[...], jnp.concatenate([zr * kr - zi * ki, zr * ki + zi * kr], axis=0))
    c, s = _twiddle(k1, n_total, q.shape[1])
    qr, qi = q[:h], q[h:]
    o_ref[0, 0] = jnp.concatenate([qr * c + qi * s, qi * c - qr * s], axis=0)


def spec_mul(d, dt, y, kf, order, C, ct, n_total):
    B, K1, R, _ = y.shape
    nc = C // ct
    return pl.pallas_call(
        functools.partial(_spec_mul_body, n_total=n_total),
        grid=(K1, B, nc),
        in_specs=[pl.BlockSpec(d.shape, lambda k, b, j: (0, 0)),
                  pl.BlockSpec(dt.shape, lambda k, b, j: (0, 0)),
                  pl.BlockSpec((1, 1, R, ct), lambda k, b, j: (b, k, 0, j)),
                  pl.BlockSpec((1, 1, R, ct), lambda k, b, j: (0, k, 0, order * nc + j))],
        out_specs=pl.BlockSpec((1, 1, R, ct), lambda k, b, j: (b, k, 0, j)),
        out_shape=jax.ShapeDtypeStruct(y.shape, F32),
        compiler_params=_cparams("parallel", "parallel", "parallel"),
        name="hy_spec_mul",
    )(d, dt, y, kf)


def _dft_out_body(a_ref, t_ref, u_ref, g_ref, s_ref, o_ref):
    _, k, s, ct = t_ref.shape
    y = _dot3(a_ref[...], t_ref[0].reshape(k * s, ct)).reshape(u_ref.shape[1], s, ct)
    o_ref[0] = g_ref[0] * (y + u_ref[0] * s_ref[...])


def dft_out(binv_kron, t, u, gate, skip):
    B, K, N2, C = t.shape
    R = u.shape[1]
    uspec = pl.BlockSpec((1, R, SUBLANES, C), lambda b, s: (b, 0, s, 0))
    return pl.pallas_call(
        _dft_out_body,
        grid=(B, N2 // SUBLANES),
        in_specs=[pl.BlockSpec(binv_kron.shape, lambda b, s: (0, 0)),
                  pl.BlockSpec((1, K, SUBLANES, C), lambda b, s: (b, 0, s, 0)),
                  uspec, uspec,
                  pl.BlockSpec((1, C), lambda b, s: (0, 0))],
        out_specs=uspec,
        out_shape=jax.ShapeDtypeStruct(u.shape, F32),
        compiler_params=_cparams("parallel", "parallel"),
        name="dft_out",
    )(binv_kron, t, u, gate, skip)


def hyena_mixer(p, C, conv_w, conv_b, f_w1, f_b1, f_freq1, f_w2, f_b2, f_freq2, f_w3, f_decay, skip):
    B, L, _ = p.shape
    cst = _dft_constants(L)
    N1, K1 = cst["N1"], cst["K1"]
    N2 = DFT_N2
    v, x1, x2 = hy_short_conv(p, conv_w, conv_b, C)
    kc = hy_filter(L, C, f_w1, f_b1, f_freq1, f_w2, f_b2, f_freq2, f_w3, f_decay)
    yf = dft_outer(cst["a1_full"], kc.reshape(1, N1, N2, HY_ORDER * C), C)
    kf = dft_inner(cst["d"], yf.reshape(1, K1, 2 * N2, HY_ORDER * C), C, 2 * L)

    def long_conv(u, order, gate):
        u4 = u.reshape(B, N1 // 2, N2, C)
        y1 = dft_outer(cst["a1_half"], u4, C)
        t = spec_mul(cst["d"], cst["dt"], y1.reshape(B, K1, 2 * N2, C), kf, order, C, C, 2 * L)
        out = dft_out(cst["binv"], t.reshape(B, 2 * K1, N2, C), u4, gate.reshape(B, N1 // 2, N2, C),
                      skip[order].reshape(1, C))
        return out.reshape(B, L, C)

    z = long_conv(v, 0, x1)
    return long_conv(z, 1, x2)


def _gla_chunk(q, k, v, lr3, up3, upb, st, reverse):
    C = GLA_CHUNK
    rr = _iota2((C, C), 0)
    cc = _iota2((C, C), 1)
    keep = (rr <= cc) if reverse else (rr >= cc)
    g = _log_sigmoid(_dot(lr3, up3) + upb) / GLA_TAU
    yield None
    bc = _mask_dot(keep.astype(BF16), g)
    yield None
    blast = bc[0:1] if reverse else bc[C - 1:C]
    q_in = (q * (GLA_DK ** -0.5) * jnp.exp(bc)).astype(BF16)
    k_in = (k * jnp.exp(-bc)).astype(BF16)
    k_end = (k * jnp.exp(blast - bc)).astype(BF16)
    vb = v.astype(BF16)
    att = _dot_nt(q_in, k_in)
    o_inter = _dot_nt(q_in, st.astype(BF16))
    st_new = st * jnp.exp(blast) + _dot_tn(vb, k_end)
    yield None
    o = _dot(jnp.where(keep, att, 0.0).astype(BF16), vb) + o_inter
    yield o, st_new


def _gla_body(qf_ref, kf_ref, vf_ref, lf_ref, qb_ref, kb_ref, vb_ref, lb_ref, up_ref, upb_ref,
              of_ref, ob_ref, st_ref, *, n_chunks):
    C = GLA_CHUNK

    @pl.when(pl.program_id(1) == 0)
    def _():
        st_ref[...] = jnp.zeros_like(st_ref)

    def chunk(step, carry):
        dirs = ((0, step, False, (qf_ref, kf_ref, vf_ref, lf_ref), of_ref),
                (1, n_chunks - 1 - step, True, (qb_ref, kb_ref, vb_ref, lb_ref), ob_ref))
        work = []
        for d, c, reverse, (q_ref, k_ref, v_ref, l_ref), o_ref in dirs:
            rows = pl.ds(pl.multiple_of(c * C, C), C)
            lr_hi, lr_lo = _split(l_ref[0, rows, :])
            lr3 = jnp.concatenate([lr_hi, lr_lo, lr_hi], axis=1)
            for h in range(GLA_HEADS):
                kl = slice(h * GLA_DK, (h + 1) * GLA_DK)
                vl = slice(h * GLA_DV, (h + 1) * GLA_DV)
                vals = (q_ref[0, rows, kl], k_ref[0, rows, kl], v_ref[0, rows, vl], lr3,
                        up_ref[d, :, kl], upb_ref[d, :, kl], st_ref[d, h])
                work.append((vals, reverse, o_ref, rows, vl, d, h))
        done = _interleave([_gla_chunk(*vals, reverse) for vals, reverse, *_ in work])
        for (o, st_new), (_, _, o_ref, rows, vl, d, h) in zip(done, work):
            o_ref[0, rows, vl] = o
            st_ref[d, h] = st_new
        return carry

    lax.fori_loop(0, n_chunks, chunk, 0)


def gla_scan(p, plr, up_pad, up_b, col0, tl=256):
    B, L, _ = p.shape
    H = GLA_HEADS
    nt = L // tl
    kw, vw = H * GLA_DK, H * GLA_DV
    qb = col0 * LANES // kw
    vb = (col0 * LANES + 2 * kw) // vw
    assert col0 * LANES % kw == 0 and (col0 * LANES + 2 * kw) % vw == 0
    specs = []
    for tmap in (lambda t: t, lambda t: nt - 1 - t):
        specs += [pl.BlockSpec((1, tl, kw), lambda b, t, tmap=tmap: (b, tmap(t), qb)),
                  pl.BlockSpec((1, tl, kw), lambda b, t, tmap=tmap: (b, tmap(t), qb + 1)),
                  pl.BlockSpec((1, tl, vw), lambda b, t, tmap=tmap: (b, tmap(t), vb)),
                  pl.BlockSpec((1, tl, LANES), lambda b, t, tmap=tmap: (b, tmap(t), 0))]
    specs += [pl.BlockSpec((2, 3 * LANES, kw), lambda b, t: (0, 0, 0)),
              pl.BlockSpec((2, 1, kw), lambda b, t: (0, 0, 0))]
    out = jax.ShapeDtypeStruct((B, L, vw), F32)
    return pl.pallas_call(
        functools.partial(_gla_body, n_chunks=tl // GLA_CHUNK),
        grid=(B, nt),
        in_specs=specs,
        out_specs=[pl.BlockSpec((1, tl, vw), lambda b, t: (b, t, 0)),
                   pl.BlockSpec((1, tl, vw), lambda b, t: (b, nt - 1 - t, 0))],
        out_shape=[out, out],
        scratch_shapes=[pltpu.VMEM((2, H, GLA_DV, GLA_DK), F32)],
        compiler_params=_cparams("parallel", "arbitrary"),
        name="gla_scan",
    )(p, p, p, plr, p, p, p, plr, up_pad, up_b)


def _gla_post_body(of_ref, ob_ref, r_ref, gn_ref, o_ref):
    o = of_ref[0] + ob_ref[0]
    o = o * lax.rsqrt(jnp.mean(o * o, axis=-1, keepdims=True) + NORM_EPS) * gn_ref[...]
    r = r_ref[0]
    o_ref[0] = (o * (r * jax.nn.sigmoid(r))).astype(o_ref.dtype)


def gla_post(o_f, o_b, p, r_block0, norm_g, tl=512):
    B, L, W = o_f.shape
    spec = pl.BlockSpec((1, tl, GLA_DV), lambda b, t, h: (b, t, h))
    return pl.pallas_call(
        _gla_post_body,
        grid=(B, L // tl, W // GLA_DV),
        in_specs=[spec, spec,
                  pl.BlockSpec((1, tl, GLA_DV), lambda b, t, h: (b, t, r_block0 + h)),
                  pl.BlockSpec((1, GLA_DV), lambda b, t, h: (0, h))],
        out_specs=spec,
        out_shape=jax.ShapeDtypeStruct((B, L, W), BF16),
        compiler_params=_cparams("parallel", "parallel", "parallel"),
        name="gla_post",
    )(o_f, o_b, p, norm_g.reshape(1, W))


def gla_mixer(p, plr, col0, up, up_b, norm_g):
    H = GLA_HEADS
    up_pad = jnp.zeros((2, LANES, H * GLA_DK), F32)
    up_pad = up_pad.at[0, 0:GLA_RANK].set(up[0]).at[1, GLA_RANK:2 * GLA_RANK].set(up[1])
    up_hi = up_pad.astype(BF16)
    up_lo = (up_pad - up_hi.astype(F32)).astype(BF16)
    up3 = jnp.concatenate([up_hi, up_hi, up_lo], axis=1)
    o_f, o_b = gla_scan(p, plr, up3, up_b.reshape(2, 1, H * GLA_DK), col0)
    r_block0 = (col0 * LANES + 2 * H * GLA_DK + H * GLA_DV) // GLA_DV
    return gla_post(o_f, o_b, p, r_block0, norm_g)


def _rw_mix_body(x_ref, xm_ref, xp_ref, g_ref, sc_ref, sh_ref, mu_ref, *o_refs, n_tiles):
    i = pl.program_id(1)
    g, sc, sh = g_ref[...], sc_ref[0], sh_ref[0]
    h = _norm_mod_value(x_ref[0], g, sc, sh)
    hm = _norm_mod_value(xm_ref[0], g, sc, sh)
    hp = _norm_mod_value(xp_ref[0], g, sc, sh)
    hm1, hp1 = _neighbours(h, hm, hp, i, n_tiles)
    xx = 0.5 * (hm1 + hp1) - h
    mu = mu_ref[...]
    for j, o_ref in enumerate(o_refs):
        o_ref[0] = (h + xx * mu[j:j + 1]).astype(o_ref.dtype)


def rw_mix(x, g, scale, shift, mu, tl=256):
    B, L, D = x.shape
    n_tiles = L // tl
    out = jax.ShapeDtypeStruct((B, L, D), BF16)
    ospec = pl.BlockSpec((1, tl, D), lambda b, i: (b, i, 0))
    mu8 = jnp.zeros((8, D), F32).at[:6].set(mu)
    return pl.pallas_call(
        functools.partial(_rw_mix_body, n_tiles=n_tiles),
        grid=(B, n_tiles),
        in_specs=_neighbour_specs(tl, D, L // 8) + [pl.BlockSpec((1, D), lambda b, i: (0, 0)),
                                                    pl.BlockSpec((1, 1, D), lambda b, i: (b, 0, 0)),
                                                    pl.BlockSpec((1, 1, D), lambda b, i: (b, 0, 0)),
                                                    pl.BlockSpec((8, D), lambda b, i: (0, 0))],
        out_specs=[ospec] * 6,
        out_shape=[out] * 6,
        compiler_params=_cparams("parallel", "parallel"),
        name="rw_mix",
    )(x, x, x, g.reshape(1, D), scale, shift, mu8)


def _head_ones():
    r = _iota2((LANES, LANES), 0) // RW_HEAD
    c = _iota2((LANES, LANES), 1) // RW_HEAD
    return (r == c).astype(BF16)


def _rw_prep_body(k_ref, r_ref, v_ref, a0_ref, a1_ref, kk_w_ref, ka_ref, rk_ref, kk_ref, bon_ref):
    ones = _head_ones()
    k = k_ref[...]
    kk = k * kk_w_ref[...]
    ss = _dot_mask(kk * kk, ones)
    kk_ref[...] = kk * lax.rsqrt(jnp.maximum(ss, 1e-24))
    a_sum = jax.nn.sigmoid(a0_ref[...]) + jax.nn.sigmoid(a1_ref[...])
    kd_sum = k * (2.0 + (a_sum - 2.0) * ka_ref[...])
    bon_ref[...] = _dot_mask(r_ref[...] * kd_sum * rk_ref[...], ones) * v_ref[...]


def rw_prep(k, r, v, apre0, apre1, k_k, k_a, r_k, tm=2048):
    T, D = k.shape
    tm = min(tm, T)
    spec = pl.BlockSpec((tm, LANES), lambda i, j: (i, j))
    pspec = pl.BlockSpec((1, LANES), lambda i, j: (0, j))
    out = jax.ShapeDtypeStruct((T, D), F32)
    return pl.pallas_call(
        _rw_prep_body,
        grid=(T // tm, D // LANES),
        in_specs=[spec, spec, spec, spec, spec, pspec, pspec, pspec],
        out_specs=[spec, spec],
        out_shape=[out, out],
        compiler_params=_cparams("parallel", "parallel"),
        name="rw_prep",
    )(k, r, v, apre0, apre1, k_k.reshape(1, D), k_a.reshape(1, D), r_k.reshape(1, D))


def _rw_chunk(r, k, v, kk, wpre, apre, ka, h, reverse):
    C = RW_CHUNK
    P = 2 * C
    rc = _iota2((C, C), 0)
    cc = _iota2((C, C), 1)
    tri = ((rc <= cc) if reverse else (rc >= cc)).astype(BF16)
    rp = _iota2((P, P), 0)
    cp = _iota2((P, P), 1)
    strict = (rp < cp) if reverse else (rp > cp)
    incl = (rp <= cp) if reverse else (rp >= cp)
    head0 = _iota2((C, LANES), 1) < RW_HEAD

    def stack(x):
        return jnp.concatenate([jnp.where(head0, x, 0.0), jnp.where(head0, 0.0, x)], axis=0).astype(BF16)

    lw = -RW_LOG_DECAY_MAX * jax.nn.sigmoid(wpre)
    a = jax.nn.sigmoid(apre)
    kd = k * (1.0 + (a - 1.0) * ka)
    b = kk * a
    cum = _mask_dot(tri, lw)
    yield None
    cum_x = cum - lw
    total = cum[0:1] if reverse else cum[C - 1:C]
    e_neg = jnp.exp(-cum)
    rs = stack(r * jnp.exp(cum))
    xs = stack(kk * jnp.exp(cum_x))
    ks = stack(kd * e_neg)
    bs = stack(b * e_neg)
    vs = stack(v)
    kb = jnp.concatenate([ks, bs], axis=0)
    coef = _dot_nt(jnp.concatenate([xs, rs], axis=0), kb)
    yield None
    a_k = jnp.where(strict, coef[:P, :P], 0.0).astype(BF16)
    q_k = jnp.where(incl, coef[P:, :P], 0.0).astype(BF16)
    q_b = jnp.where(incl, coef[P:, P:], 0.0).astype(BF16)
    pw = jnp.where(strict, -coef[:P, P:], 0.0)
    hv = jnp.concatenate([h.astype(BF16), vs], axis=0)
    x = _dot(jnp.concatenate([xs, a_k], axis=1), hv)
    o_hv = _dot(jnp.concatenate([rs, q_k], axis=1), hv)
    yield None
    for _ in range(int(math.log2(C)) - 1):
        pwb = pw.astype(BF16)
        both = _dot(pwb, jnp.concatenate([x.astype(BF16), pwb], axis=1))
        x = x + both[:, :P]
        pw = both[:, P:]
        yield None
    u = (x + _dot(pw.astype(BF16), x.astype(BF16))).astype(BF16)
    yield None
    os = o_hv - _dot(q_b, u)
    gcol = jnp.transpose(jnp.broadcast_to(jnp.exp(total), (LANES, LANES)))
    h_new = gcol * (h + _dot_tn(kb, jnp.concatenate([vs, -u], axis=0)))
    yield os[:C] + os[C:], h_new


def _interleave(chains):
    while True:
        outs = [next(ch) for ch in chains]
        if outs[0] is not None:
            return outs


def _rw_scan_body(rf_ref, kf_ref, vf_ref, kkf_ref, wf_ref, af_ref, rb_ref, kb_ref, vb_ref, kkb_ref, wb_ref,
                  ab_ref, ka_ref, of_ref, ob_ref, h_ref, *, n_chunks, n_pairs):
    C = RW_CHUNK

    @pl.when(pl.program_id(2) == 0)
    def _():
        h_ref[...] = jnp.zeros_like(h_ref)

    def chunk(step, carry):
        dirs = ((0, step, False, (rf_ref, kf_ref, vf_ref, kkf_ref, wf_ref, af_ref), of_ref),
                (1, n_chunks - 1 - step, True, (rb_ref, kb_ref, vb_ref, kkb_ref, wb_ref, ab_ref), ob_ref))
        work = []
        for d, c, reverse, refs, o_ref in dirs:
            rows = pl.ds(pl.multiple_of(c * C, C), C)
            for g in range(n_pairs):
                lanes = slice(g * LANES, (g + 1) * LANES)
                vals = [ref[0, rows, lanes] for ref in refs]
                work.append((vals, ka_ref[:, lanes], h_ref[d, g], reverse, o_ref, rows, lanes, d, g))
        done = _interleave([_rw_chunk(*vals, ka, h, reverse) for vals, ka, h, reverse, *_ in work])
        for (o, h_new), (_, _, _, _, o_ref, rows, lanes, d, g) in zip(done, work):
            o_ref[0, rows, lanes] = o
            h_ref[d, g] = h_new
        return carry

    lax.fori_loop(0, n_chunks, chunk, 0)


def rw_scan(r, k, v, kk, wpre, apre, k_a, tl=512, n_pairs=4):
    B, L, D = r.shape
    width = n_pairs * LANES
    nt = L // tl
    fspec = pl.BlockSpec((1, tl, width), lambda b, p, t: (b, t, p))
    bspec = pl.BlockSpec((1, tl, width), lambda b, p, t: (b, nt - 1 - t, p))
    out = jax.ShapeDtypeStruct((B, L, D), F32)
    return pl.pallas_call(
        functools.partial(_rw_scan_body, n_chunks=tl // RW_CHUNK, n_pairs=n_pairs),
        grid=(B, D // width, nt),
        in_specs=[fspec] * 6 + [bspec] * 6 + [pl.BlockSpec((1, width), lambda b, p, t: (0, p))],
        out_specs=[fspec, bspec],
        out_shape=[out, out],
        scratch_shapes=[pltpu.VMEM((2, n_pairs, LANES, LANES), F32)],
        compiler_params=_cparams("parallel", "parallel", "arbitrary"),
        name="rw_scan",
    )(r, k, v, kk, wpre[0], apre[0], r, k, v, kk, wpre[1], apre[1], k_a.reshape(1, D))


def _rw_post_body(of_ref, ob_ref, bon_ref, g_ref, gng_ref, gnb_ref, o_ref):
    ones = _head_ones()
    so = of_ref[...] + ob_ref[...]
    mean = _dot_mask(so, ones) * (1.0 / RW_HEAD)
    dlt = so - mean
    var = _dot_mask(dlt * dlt, ones) * (1.0 / RW_HEAD)
    y = dlt * lax.rsqrt(var + RW_GN_EPS) * gng_ref[...] + gnb_ref[...]
    o_ref[...] = ((y + bon_ref[...]) * g_ref[...]).astype(o_ref.dtype)


def rw_post(o_f, o_b, bonus, g, gn_g, gn_b, tm=2048):
    T, D = o_f.shape
    tm = min(tm, T)
    spec = pl.BlockSpec((tm, LANES), lambda i, j: (i, j))
    pspec = pl.BlockSpec((1, LANES), lambda i, j: (0, j))
    return pl.pallas_call(
        _rw_post_body,
        grid=(T // tm, D // LANES),
        in_specs=[spec, spec, spec, spec, pspec, pspec],
        out_specs=spec,
        out_shape=jax.ShapeDtypeStruct((T, D), BF16),
        compiler_params=_cparams("parallel", "parallel"),
        name="rw_post",
    )(o_f, o_b, bonus, g, gn_g.reshape(1, D), gn_b.reshape(1, D))


def _pad_cols(w, n):
    return jnp.pad(w, ((0, 0), (0, n - w.shape[1])))


def _pad_rows(w, n):
    return jnp.pad(w, ((0, n - w.shape[0]), (0, 0)))


def rwkv7_block(x, gate, g_norm, scale, shift, mu, w_r, w_k, w_v, w0, w1, w2, a0, a1, a2, g1, g2,
                k_k, k_a, r_k, gn_g, gn_b, w_o):
    B, L, D = x.shape
    T = B * L
    xr, xw, xk, xv, xa, xg = [a.reshape(T, D) for a in rw_mix(x, g_norm, scale, shift, mu)]
    r = matmul(xr, w_r.astype(BF16))
    k = matmul(xk, w_k.astype(BF16))
    v = matmul(xv, w_v.astype(BF16))
    w1c = jnp.concatenate([_pad_cols(w1[0], LANES), _pad_cols(w1[1], LANES)], axis=1).astype(BF16)
    a1c = jnp.concatenate([_pad_cols(a1[0], LANES), _pad_cols(a1[1], LANES)], axis=1).astype(BF16)
    wmid = matmul(xw, w1c)
    amid = matmul(xa, a1c)
    gmid = matmul(xg, g1.astype(BF16))
    wpre = [matmul(wmid, _pad_rows(w2[d], LANES).astype(BF16), a_act="tanh", bias=w0[d], a_col_block=d)
            for d in range(2)]
    apre = [matmul(amid, _pad_rows(a2[d], LANES).astype(BF16), bias=a0[d], a_col_block=d)
            for d in range(2)]
    g = matmul(gmid, g2.astype(BF16), a_act="sigmoid")
    kk, bonus = rw_prep(k, r, v, apre[0], apre[1], k_k, k_a, r_k.reshape(D))
    sh = lambda a: a.reshape(B, L, D)
    o_f, o_b = rw_scan(sh(r), sh(k), sh(v), sh(kk), [sh(a) for a in wpre], [sh(a) for a in apre], k_a)
    y = rw_post(o_f.reshape(T, D), o_b.reshape(T, D), bonus, g, gn_g, gn_b)
    out = matmul(y, w_o.astype(BF16), res=x.reshape(T, D), gate=gate, rows_per_batch=L)
    return out.reshape(B, L, D)


def ffn_block(x, c, norm_g, ada_w, ada_b, idx, w1, w2):
    B, L, D = x.shape
    T = B * L
    shift, scale, gate = ada_mod(c, ada_w, ada_b, idx)
    h = norm_mod(x, norm_g, scale, shift).reshape(T, D)
    hid = matmul(h, w1.astype(BF16), out_dtype=BF16, epi="sqrelu")
    out = matmul(hid, w2.astype(BF16), res=x.reshape(T, D), gate=gate, rows_per_batch=L, tm=512, tn=512)
    return out.reshape(B, L, D)


def hyena_gla_block(x, gate, h, w_in, w_out, conv_w, conv_b, f_w1, f_b1, f_freq1, f_w2, f_b2, f_freq2,
                    f_w3, f_decay, hy_skip, gla_up, gla_up_b, gla_norm_g):
    B, L, D = x.shape
    T = B * L
    C = conv_w.shape[1] // 3
    n_main = w_in.shape[1] - 2 * GLA_RANK
    p = matmul(h, w_in[:, :n_main].astype(BF16)).reshape(B, L, n_main)
    plr = matmul(h, _pad_cols(w_in[:, n_main:], LANES).astype(BF16)).reshape(B, L, LANES)
    y_hy = hyena_mixer(p, C, conv_w, conv_b, f_w1, f_b1, f_freq1, f_w2, f_b2, f_freq2, f_w3, f_decay, hy_skip)
    y_gla = gla_mixer(p, plr, 3 * C // LANES, gla_up, gla_up_b, gla_norm_g)
    y = jnp.concatenate([y_hy.astype(BF16), y_gla], axis=-1).reshape(T, -1)
    out = matmul(y, w_out.astype(BF16), res=x.reshape(T, D), gate=gate, rows_per_batch=L)
    return out.reshape(B, L, D)


def kernel(x, c, mix_norm_g, mix_ada_w, mix_ada_b, ab_w_in, ab_w_out, hy_conv_w, hy_conv_b, hy_ffn_w1, hy_ffn_b1, hy_freq1, hy_ffn_w2, hy_ffn_b2, hy_freq2, hy_ffn_w3, hy_decay, hy_skip, gla_up, gla_up_b, gla_norm_g, tm_norm_g, tm_ada_w, tm_ada_b, rw_mu, rw_w_r, rw_w_k, rw_w_v, rw_w0, rw_w1, rw_w2, rw_a0, rw_a1, rw_a2, rw_g1, rw_g2, rw_k_k, rw_k_a, rw_r_k, rw_gn_g, rw_gn_b, rw_w_o, ffn_norm_g, ffn_ada_w, ffn_ada_b, ffn_w1, ffn_w2, final_norm_g):
    B, L, D = x.shape
    depth = ffn_w1.shape[0]
    for layer in range(depth):
        i = layer // 2
        if layer % 2 == 0:
            shift, scale, gate = ada_mod(c, mix_ada_w, mix_ada_b, i)
            h = norm_mod(x, mix_norm_g[i], scale, shift).reshape(B * L, D)
            x = hyena_gla_block(x, gate, h, ab_w_in[i], ab_w_out[i], hy_conv_w[i], hy_conv_b[i],
                                hy_ffn_w1[i], hy_ffn_b1[i], hy_freq1[i], hy_ffn_w2[i], hy_ffn_b2[i],
                                hy_freq2[i], hy_ffn_w3[i], hy_decay[i], hy_skip[i],
                                gla_up[i], gla_up_b[i], gla_norm_g[i])
        else:
            shift, scale, gate = ada_mod(c, tm_ada_w, tm_ada_b, i)
            x = rwkv7_block(x, gate, tm_norm_g[i], scale, shift, rw_mu[i], rw_w_r[i], rw_w_k[i], rw_w_v[i],
                            rw_w0[i], rw_w1[i], rw_w2[i], rw_a0[i], rw_a1[i], rw_a2[i], rw_g1[i], rw_g2[i],
                            rw_k_k[i], rw_k_a[i], rw_r_k[i], rw_gn_g[i], rw_gn_b[i], rw_w_o[i])
        x = ffn_block(x, c, ffn_norm_g[layer], ffn_ada_w, ffn_ada_b, layer, ffn_w1[layer], ffn_w2[layer])
    return rms_final(x, final_norm_g)
```

```python
import functools
import math

import numpy as np
import jax
import jax.numpy as jnp
from jax import lax
from jax.experimental import pallas as pl
from jax.experimental.pallas import tpu as pltpu

F32 = jnp.float32
BF16 = jnp.bfloat16
HI = lax.Precision.HIGHEST

LANES = 128
SUBLANES = 8
NORM_EPS = 1e-6
GLA_HEADS = 4
GLA_DK = 128
GLA_DV = 256
GLA_RANK = 16
GLA_TAU = 16.0
GLA_CHUNK = 64
RW_HEAD = 64
RW_CHUNK = 64
RW_LOG_DECAY_MAX = 0.606531
RW_GN_EPS = 64e-5
HY_ORDER = 2
HY_EMB = 33
HY_FFN = 64
DFT_N2 = 128
VMEM_LIMIT = 52 * 1024 * 1024


def _cparams(*sem):
    return pltpu.CompilerParams(dimension_semantics=sem, vmem_limit_bytes=VMEM_LIMIT)


def _dot(a, b, precision=None):
    return jnp.dot(a, b, preferred_element_type=F32, precision=precision)


def _dot_nt(a, b, precision=None):
    return lax.dot_general(a, b, (((1,), (1,)), ((), ())), preferred_element_type=F32, precision=precision)


def _dot_tn(a, b, precision=None):
    return lax.dot_general(a, b, (((0,), (0,)), ((), ())), preferred_element_type=F32, precision=precision)


def _iota2(shape, axis):
    return lax.broadcasted_iota(jnp.int32, shape, axis)


def _log_sigmoid(x):
    return jnp.minimum(x, 0.0) - jnp.log(1.0 + jnp.exp(-jnp.abs(x)))


def _split(x):
    hi = x.astype(BF16)
    return hi, (x - hi.astype(F32)).astype(BF16)


def _mask_dot(mask_bf16, x):
    hi, lo = _split(x)
    n = x.shape[1]
    y = _dot(mask_bf16, jnp.concatenate([hi, lo], axis=1))
    return y[:, :n] + y[:, n:]


def _stack3(a):
    a = np.asarray(a, np.float32)
    hi = a.astype(BF16)
    lo = (a - hi.astype(np.float32)).astype(BF16)
    return jnp.asarray(np.concatenate([hi, hi, lo], axis=1))


def _dot3(a3, x):
    hi, lo = _split(x)
    return _dot(a3, jnp.concatenate([hi, lo, hi], axis=0))


def _dot_mask(x, mask_bf16):
    hi, lo = _split(x)
    return _dot(hi, mask_bf16) + _dot(lo, mask_bf16)


def _ada_body(c_ref, w_ref, b_ref, o_ref):
    c = c_ref[...]
    s = c * jax.nn.sigmoid(c)
    o_ref[...] = _dot(s, w_ref[0], HI) + b_ref[0]


def ada_mod(c, w, b, idx):
    B, D = c.shape
    n3 = w.shape[2]
    tn = 512
    out = pl.pallas_call(
        _ada_body,
        grid=(n3 // tn,),
        in_specs=[pl.BlockSpec((B, D), lambda j: (0, 0)),
                  pl.BlockSpec((1, D, tn), lambda j: (idx, 0, j)),
                  pl.BlockSpec((1, 1, tn), lambda j: (idx, 0, j))],
        out_specs=pl.BlockSpec((B, tn), lambda j: (0, j)),
        out_shape=jax.ShapeDtypeStruct((B, n3), F32),
        compiler_params=_cparams("parallel"),
        name="ada_mod",
    )(c, w, b.reshape(b.shape[0], 1, n3))
    shift, scale, gate = jnp.split(out[:, None, :], 3, axis=-1)
    return shift, scale, gate


def _norm_mod_value(x, g, scale, shift):
    y = x * lax.rsqrt(jnp.mean(x * x, axis=-1, keepdims=True) + NORM_EPS)
    return (y * g) * (1.0 + scale) + shift


def _norm_mod_body(x_ref, g_ref, sc_ref, sh_ref, o_ref):
    o_ref[0] = _norm_mod_value(x_ref[0], g_ref[...], sc_ref[0], sh_ref[0]).astype(o_ref.dtype)


def norm_mod(x, g, scale, shift, out_dtype=BF16, tm=512):
    B, L, D = x.shape
    return pl.pallas_call(
        _norm_mod_body,
        grid=(B, L // tm),
        in_specs=[pl.BlockSpec((1, tm, D), lambda b, i: (b, i, 0)),
                  pl.BlockSpec((1, D), lambda b, i: (0, 0)),
                  pl.BlockSpec((1, 1, D), lambda b, i: (b, 0, 0)),
                  pl.BlockSpec((1, 1, D), lambda b, i: (b, 0, 0))],
        out_specs=pl.BlockSpec((1, tm, D), lambda b, i: (b, i, 0)),
        out_shape=jax.ShapeDtypeStruct((B, L, D), out_dtype),
        compiler_params=_cparams("parallel", "parallel"),
        name="norm_mod",
    )(x, g.reshape(1, D), scale, shift)


def _rms_body(x_ref, g_ref, o_ref):
    x = x_ref[0]
    y = x * lax.rsqrt(jnp.mean(x * x, axis=-1, keepdims=True) + NORM_EPS)
    o_ref[0] = y * g_ref[...]


def rms_final(x, g, tm=512):
    B, L, D = x.shape
    return pl.pallas_call(
        _rms_body,
        grid=(B, L // tm),
        in_specs=[pl.BlockSpec((1, tm, D), lambda b, i: (b, i, 0)),
                  pl.BlockSpec((1, D), lambda b, i: (0, 0))],
        out_specs=pl.BlockSpec((1, tm, D), lambda b, i: (b, i, 0)),
        out_shape=jax.ShapeDtypeStruct((B, L, D), F32),
        compiler_params=_cparams("parallel", "parallel"),
        name="rms_final",
    )(x, g.reshape(1, D))


def _mm_body(*refs, nk, a_act, epi, has_a2, has_bias, has_res):
    it = iter(refs)
    a_ref = next(it)
    w_ref = next(it)
    a2_ref = next(it) if has_a2 else None
    w2_ref = next(it) if has_a2 else None
    bias_ref = next(it) if has_bias else None
    res_ref = next(it) if has_res else None
    gate_ref = next(it) if has_res else None
    o_ref = next(it)
    acc_ref = next(it) if nk > 1 else None

    a = a_ref[...]
    if a_act == "tanh":
        a = jnp.tanh(a.astype(F32))
    elif a_act == "sigmoid":
        a = jax.nn.sigmoid(a.astype(F32))
    part = _dot(a.astype(BF16), w_ref[0])
    if has_a2:
        part = part + _dot(a2_ref[...].astype(BF16), w2_ref[0])

    def finish(acc):
        if has_bias:
            acc = acc + bias_ref[...]
        if epi == "sqrelu":
            acc = jnp.square(jnp.maximum(acc, 0.0))
        if has_res:
            acc = res_ref[...] + gate_ref[0] * acc
        o_ref[...] = acc.astype(o_ref.dtype)

    if nk == 1:
        finish(part)
    else:
        k = pl.program_id(2)

        @pl.when(k == 0)
        def _():
            acc_ref[...] = part

        @pl.when(k > 0)
        def _():
            acc_ref[...] += part

        @pl.when(k == nk - 1)
        def _():
            finish(acc_ref[...])


def matmul(a, w, *, a2=None, w_idx=0, out_dtype=F32, a_act=None, epi=None, bias=None, res=None, gate=None,
           rows_per_batch=None, tm=1024, tn=1024, tk=None, a_col_block=0):
    M = a.shape[0]
    if w.ndim == 2:
        w = w[None]
    K, N = w.shape[1:]
    has_a2 = a2 is not None
    ka = K // 2 if has_a2 else K
    tm = min(tm, M if rows_per_batch is None else rows_per_batch)
    tn = min(tn, N)
    tk = ka if tk is None else min(tk, ka)
    nk = ka // tk
    assert M % tm == 0 and N % tn == 0 and ka % tk == 0 and not (has_a2 and nk > 1)
    has_bias = bias is not None
    has_res = res is not None
    in_specs = [pl.BlockSpec((tm, tk), lambda i, j, k: (i, k + a_col_block * nk)),
                pl.BlockSpec((1, tk, tn), lambda i, j, k: (w_idx, k, j))]
    args = [a, w]
    if has_a2:
        in_specs += [pl.BlockSpec((tm, tk), lambda i, j, k: (i, 0)),
                     pl.BlockSpec((1, tk, tn), lambda i, j, k: (w_idx, 1, j))]
        args += [a2, w]
    if has_bias:
        in_specs.append(pl.BlockSpec((1, tn), lambda i, j, k: (0, j)))
        args.append(bias.reshape(1, N).astype(F32))
    if has_res:
        tiles_per_batch = rows_per_batch // tm
        assert rows_per_batch % tm == 0
        in_specs.append(pl.BlockSpec((tm, tn), lambda i, j, k: (i, j)))
        in_specs.append(pl.BlockSpec((1, 1, tn), lambda i, j, k: (i // tiles_per_batch, 0, j)))
        args += [res, gate]
    scratch = [pltpu.VMEM((tm, tn), F32)] if nk > 1 else []
    body = functools.partial(_mm_body, nk=nk, a_act=a_act, epi=epi, has_a2=has_a2, has_bias=has_bias,
                             has_res=has_res)
    return pl.pallas_call(
        body,
        grid=(M // tm, N // tn, nk),
        in_specs=in_specs,
        out_specs=pl.BlockSpec((tm, tn), lambda i, j, k: (i, j)),
        out_shape=jax.ShapeDtypeStruct((M, N), out_dtype),
        scratch_shapes=scratch,
        compiler_params=_cparams("parallel", "parallel", "arbitrary"),
        name="matmul",
    )(*args)


def _neighbours(cur, prev8, next8, i, n_tiles):
    t = cur.shape[0]
    row = _iota2(cur.shape, 0)
    before = jnp.where(i > 0, prev8[7:8, :], 0.0)
    after = jnp.where(i < n_tiles - 1, next8[0:1, :], 0.0)
    xm1 = jnp.where(row == 0, before, pltpu.roll(cur, 1, 0))
    xp1 = jnp.where(row == t - 1, after, pltpu.roll(cur, t - 1, 0))
    return xm1, xp1


def _neighbour_specs(tl, width, n_rows8, col_map=None):
    r8 = tl // 8
    return [pl.BlockSpec((1, tl, width), lambda b, i: (b, i, 0)),
            pl.BlockSpec((1, 8, width), lambda b, i: (b, jnp.maximum(i * r8 - 1, 0), 0)),
            pl.BlockSpec((1, 8, width), lambda b, i: (b, jnp.minimum((i + 1) * r8, n_rows8 - 1), 0))]


def _hy_conv_body(p_ref, pm_ref, pp_ref, w_ref, b_ref, v_ref, x1_ref, x2_ref, *, n_tiles, C):
    i = pl.program_id(1)
    cur = p_ref[0]
    xm1, xp1 = _neighbours(cur, pm_ref[0], pp_ref[0], i, n_tiles)
    w = w_ref[...]
    u = xm1 * w[0:1] + cur * w[1:2] + xp1 * w[2:3] + b_ref[...]
    v_ref[0] = u[:, 0:C]
    x1_ref[0] = u[:, C:2 * C]
    x2_ref[0] = u[:, 2 * C:3 * C]


def hy_short_conv(p, conv_w, conv_b, C, tl=256):
    B, L, _ = p.shape
    W = 3 * C
    n_tiles = L // tl
    out = jax.ShapeDtypeStruct((B, L, C), F32)
    ospec = pl.BlockSpec((1, tl, C), lambda b, i: (b, i, 0))
    return pl.pallas_call(
        functools.partial(_hy_conv_body, n_tiles=n_tiles, C=C),
        grid=(B, n_tiles),
        in_specs=_neighbour_specs(tl, W, L // 8) + [pl.BlockSpec((3, W), lambda b, i: (0, 0)),
                                                    pl.BlockSpec((1, W), lambda b, i: (0, 0))],
        out_specs=[ospec, ospec, ospec],
        out_shape=[out, out, out],
        compiler_params=_cparams("parallel", "parallel"),
        name="hy_short_conv",
    )(p, p, p, conv_w, conv_b.reshape(1, W))


def _hy_filter_body(z_ref, w1_ref, b1_ref, f1_ref, w2_ref, b2_ref, f2_ref, w3_ref, dec_ref, o_ref, *, L, tm):
    i = pl.program_id(0)
    z = z_ref[...]
    f = jnp.sin(f1_ref[...] * (_dot(z, w1_ref[...], HI) + b1_ref[...]))
    f = jnp.sin(f2_ref[...] * (_dot(f, w2_ref[...], HI) + b2_ref[...]))
    t = z[:, 0:1]
    f = _dot(f, w3_ref[...], HI) * jnp.exp(-t * jnp.abs(dec_ref[...]))
    n = i * tm + _iota2(f.shape, 0)
    o_ref[...] = jnp.where(n == L, 0.0, f)


def hy_filter(L, C, w1, b1, freq1, w2, b2, freq2, w3, decay, tm=512):
    pos = np.concatenate([np.arange(L), [0], np.arange(L - 1, 0, -1)]).astype(np.float64)
    t = pos / (L - 1)
    ang = 2.0 * math.pi * pos / L
    nb = (HY_EMB - 1) // 2
    ba = np.linspace(1e-4, nb - 1, nb)[None, :] * ang[:, None]
    z = np.concatenate([t[:, None], np.cos(ba), -np.sin(ba)], axis=-1)
    zp = np.zeros((2 * L, 2 * HY_FFN), np.float32)
    zp[:, :HY_EMB] = z
    w1p = jnp.zeros((2 * HY_FFN, HY_FFN), F32).at[:HY_EMB].set(w1)
    nL = L // tm
    row = lambda a: a.reshape(1, -1)
    return pl.pallas_call(
        functools.partial(_hy_filter_body, L=L, tm=tm),
        grid=(2 * L // tm, HY_ORDER),
        in_specs=[pl.BlockSpec((tm, 2 * HY_FFN), lambda i, o: (i, 0)),
                  pl.BlockSpec((2 * HY_FFN, HY_FFN), lambda i, o: (0, 0)),
                  pl.BlockSpec((1, HY_FFN), lambda i, o: (0, 0)),
                  pl.BlockSpec((1, HY_FFN), lambda i, o: (0, 0)),
                  pl.BlockSpec((HY_FFN, HY_FFN), lambda i, o: (0, 0)),
                  pl.BlockSpec((1, HY_FFN), lambda i, o: (0, 0)),
                  pl.BlockSpec((1, HY_FFN), lambda i, o: (0, 0)),
                  pl.BlockSpec((HY_FFN, C), lambda i, o: (0, 2 * o + i // nL)),
                  pl.BlockSpec((1, C), lambda i, o: (0, 2 * o + i // nL))],
        out_specs=pl.BlockSpec((tm, C), lambda i, o: (i, o)),
        out_shape=jax.ShapeDtypeStruct((2 * L, HY_ORDER * C), F32),
        compiler_params=_cparams("parallel", "parallel"),
        name="hy_filter",
    )(jnp.asarray(zp), w1p, row(b1), row(freq1), w2, row(b2), row(freq2), w3, row(decay))


def _dft_constants(L):
    N = 2 * L
    N2 = DFT_N2
    N1 = N // N2
    K1 = N1 // 2 + 1
    k1 = np.arange(K1, dtype=np.float64)
    n1 = np.arange(N1, dtype=np.float64)
    ang = 2.0 * np.pi * np.outer(k1, n1) / N1
    a1 = np.stack([np.cos(ang), -np.sin(ang)], axis=1).reshape(2 * K1, N1)
    n1h = np.arange(N1 // 2, dtype=np.float64)
    angb = 2.0 * np.pi * np.outer(n1h, k1) / N1
    ck = np.where((np.arange(K1) == 0) | (np.arange(K1) == N1 // 2), 1.0, 2.0)[None, :]
    binv = np.stack([ck * np.cos(angb), -ck * np.sin(angb)], axis=2).reshape(N1 // 2, 2 * K1) / N
    n2 = np.arange(N2, dtype=np.float64)
    ph = -2.0 * np.pi * (np.outer(n2, n2) % N2) / N2
    dre, dim = np.cos(ph), np.sin(ph)
    dst = np.block([[dre, -dim], [dim, dre]])
    kron = lambda a: _stack3(np.kron(a, np.eye(SUBLANES)))
    return dict(N1=N1, K1=K1, a1_full=kron(a1), a1_half=kron(a1[:, :N1 // 2]), binv=kron(binv),
                d=_stack3(dst), dt=_stack3(dst.T))


def _twiddle(k1, n_total, width):
    n2 = _iota2((DFT_N2, LANES), 0).astype(F32)
    ang = (-2.0 * math.pi / n_total) * (k1.astype(F32) * n2)
    reps = width // LANES
    return jnp.tile(jnp.cos(ang), (1, reps)), jnp.tile(jnp.sin(ang), (1, reps))


def _inner_forward(d, y, k1, n_total):
    h = DFT_N2
    c, s = _twiddle(k1, n_total, y.shape[1])
    yr, yi = y[:h], y[h:]
    return _dot3(d, jnp.concatenate([yr * c - yi * s, yr * s + yi * c], axis=0))


def _dft_outer_body(a_ref, x_ref, o_ref):
    _, k, s, ct = x_ref.shape
    y = _dot3(a_ref[...], x_ref[0].reshape(k * s, ct))
    o_ref[0] = y.reshape(o_ref.shape[1], s, ct)


def dft_outer(a_kron, x, ct):
    B, K, N2, W = x.shape
    R = a_kron.shape[0] // SUBLANES
    ct = min(ct, W)
    return pl.pallas_call(
        _dft_outer_body,
        grid=(B, N2 // SUBLANES, W // ct),
        in_specs=[pl.BlockSpec(a_kron.shape, lambda b, s, j: (0, 0)),
                  pl.BlockSpec((1, K, SUBLANES, ct), lambda b, s, j: (b, 0, s, j))],
        out_specs=pl.BlockSpec((1, R, SUBLANES, ct), lambda b, s, j: (b, 0, s, j)),
        out_shape=jax.ShapeDtypeStruct((B, R, N2, W), F32),
        compiler_params=_cparams("parallel", "parallel", "parallel"),
        name="dft_outer",
    )(a_kron, x)


def _dft_inner_body(d_ref, y_ref, o_ref, *, n_total):
    o_ref[0, 0] = _inner_forward(d_ref[...], y_ref[0, 0], pl.program_id(0), n_total)


def dft_inner(d, y, ct, n_total):
    B, K1, R, W = y.shape
    return pl.pallas_call(
        functools.partial(_dft_inner_body, n_total=n_total),
        grid=(K1, B, W // ct),
        in_specs=[pl.BlockSpec(d.shape, lambda k, b, j: (0, 0)),
                  pl.BlockSpec((1, 1, R, ct), lambda k, b, j: (b, k, 0, j))],
        out_specs=pl.BlockSpec((1, 1, R, ct), lambda k, b, j: (b, k, 0, j)),
        out_shape=jax.ShapeDtypeStruct(y.shape, F32),
        compiler_params=_cparams("parallel", "parallel", "parallel"),
        name="dft_inner",
    )(d, y)


def _spec_mul_body(d_ref, dt_ref, y_ref, kf_ref, o_ref, *, n_total):
    k1 = pl.program_id(0)
    z = _inner_forward(d_ref[...], y_ref[0, 0], k1, n_total)
    h = DFT_N2
    zr, zi = z[:h], z[h:]
    kf = kf_ref[0, 0]
    kr, ki = kf[:h], kf[h:]
    q = _dot3(dt_ref[...], jnp.concatenate([zr * kr - zi * ki, zr * ki + zi * kr], axis=0))
    c, s = _twiddle(k1, n_total, q.shape[1])
    qr, qi = q[:h], q[h:]
    o_ref[0, 0] = jnp.concatenate([qr * c + qi * s, qi * c - qr * s], axis=0)


def spec_mul(d, dt, y, kf, order, C, ct, n_total):
    B, K1, R, _ = y.shape
    nc = C // ct
    return pl.pallas_call(
        functools.partial(_spec_mul_body, n_total=n_total),
        grid=(K1, B, nc),
        in_specs=[pl.BlockSpec(d.shape, lambda k, b, j: (0, 0)),
                  pl.BlockSpec(dt.shape, lambda k, b, j: (0, 0)),
                  pl.BlockSpec((1, 1, R, ct), lambda k, b, j: (b, k, 0, j)),
                  pl.BlockSpec((1, 1, R, ct), lambda k, b, j: (0, k, 0, order * nc + j))],
        out_specs=pl.BlockSpec((1, 1, R, ct), lambda k, b, j: (b, k, 0, j)),
        out_shape=jax.ShapeDtypeStruct(y.shape, F32),
        compiler_params=_cparams("parallel", "parallel", "parallel"),
        name="hy_spec_mul",
    )(d, dt, y, kf)


def _dft_out_body(a_ref, t_ref, u_ref, g_ref, s_ref, o_ref):
    _, k, s, ct = t_ref.shape
    y = _dot3(a_ref[...], t_ref[0].reshape(k * s, ct)).reshape(u_ref.shape[1], s, ct)
    o_ref[0] = g_ref[0] * (y + u_ref[0] * s_ref[...])


def dft_out(binv_kron, t, u, gate, skip):
    B, K, N2, C = t.shape
    R = u.shape[1]
    uspec = pl.BlockSpec((1, R, SUBLANES, C), lambda b, s: (b, 0, s, 0))
    return pl.pallas_call(
        _dft_out_body,
        grid=(B, N2 // SUBLANES),
        in_specs=[pl.BlockSpec(binv_kron.shape, lambda b, s: (0, 0)),
                  pl.BlockSpec((1, K, SUBLANES, C), lambda b, s: (b, 0, s, 0)),
                  uspec, uspec,
                  pl.BlockSpec((1, C), lambda b, s: (0, 0))],
        out_specs=uspec,
        out_shape=jax.ShapeDtypeStruct(u.shape, F32),
        compiler_params=_cparams("parallel", "parallel"),
        name="dft_out",
    )(binv_kron, t, u, gate, skip)


def hyena_mixer(p, C, conv_w, conv_b, f_w1, f_b1, f_freq1, f_w2, f_b2, f_freq2, f_w3, f_decay, skip):
    B, L, _ = p.shape
    cst = _dft_constants(L)
    N1, K1 = cst["N1"], cst["K1"]
    N2 = DFT_N2
    v, x1, x2 = hy_short_conv(p, conv_w, conv_b, C)
    kc = hy_filter(L, C, f_w1, f_b1, f_freq1, f_w2, f_b2, f_freq2, f_w3, f_decay)
    yf = dft_outer(cst["a1_full"], kc.reshape(1, N1, N2, HY_ORDER * C), C)
    kf = dft_inner(cst["d"], yf.reshape(1, K1, 2 * N2, HY_ORDER * C), C, 2 * L)

    def long_conv(u, order, gate):
        u4 = u.reshape(B, N1 // 2, N2, C)
        y1 = dft_outer(cst["a1_half"], u4, C)
        t = spec_mul(cst["d"], cst["dt"], y1.reshape(B, K1, 2 * N2, C), kf, order, C, C, 2 * L)
        out = dft_out(cst["binv"], t.reshape(B, 2 * K1, N2, C), u4, gate.reshape(B, N1 // 2, N2, C),
                      skip[order].reshape(1, C))
        return out.reshape(B, L, C)

    z = long_conv(v, 0, x1)
    return long_conv(z, 1, x2)


def _gla_chunk(q, k, v, lr3, up3, upb, st, reverse):
    C = GLA_CHUNK
    rr = _iota2((C, C), 0)
    cc = _iota2((C, C), 1)
    keep = (rr <= cc) if reverse else (rr >= cc)
    g = _log_sigmoid(_dot(lr3, up3) + upb) / GLA_TAU
    yield None
    bc = _mask_dot(keep.astype(BF16), g)
    yield None
    blast = bc[0:1] if reverse else bc[C - 1:C]
    q_in = (q * (GLA_DK ** -0.5) * jnp.exp(bc)).astype(BF16)
    k_in = (k * jnp.exp(-bc)).astype(BF16)
    k_end = (k * jnp.exp(blast - bc)).astype(BF16)
    vb = v.astype(BF16)
    att = _dot_nt(q_in, k_in)
    o_inter = _dot_nt(q_in, st.astype(BF16))
    st_new = st * jnp.exp(blast) + _dot_tn(vb, k_end)
    yield None
    o = _dot(jnp.where(keep, att, 0.0).astype(BF16), vb) + o_inter
    yield o, st_new


def _gla_body(qf_ref, kf_ref, vf_ref, lf_ref, qb_ref, kb_ref, vb_ref, lb_ref, up_ref, upb_ref,
              of_ref, ob_ref, st_ref, *, n_chunks):
    C = GLA_CHUNK

    @pl.when(pl.program_id(1) == 0)
    def _():
        st_ref[...] = jnp.zeros_like(st_ref)

    def chunk(step, carry):
        dirs = ((0, step, False, (qf_ref, kf_ref, vf_ref, lf_ref), of_ref),
                (1, n_chunks - 1 - step, True, (qb_ref, kb_ref, vb_ref, lb_ref), ob_ref))
        work = []
        for d, c, reverse, (q_ref, k_ref, v_ref, l_ref), o_ref in dirs:
            rows = pl.ds(pl.multiple_of(c * C, C), C)
            lr_hi, lr_lo = _split(l_ref[0, rows, :])
            lr3 = jnp.concatenate([lr_hi, lr_lo, lr_hi], axis=1)
            for h in range(GLA_HEADS):
                kl = slice(h * GLA_DK, (h + 1) * GLA_DK)
                vl = slice(h * GLA_DV, (h + 1) * GLA_DV)
                vals = (q_ref[0, rows, kl], k_ref[0, rows, kl], v_ref[0, rows, vl], lr3,
                        up_ref[d, :, kl], upb_ref[d, :, kl], st_ref[d, h])
                work.append((vals, reverse, o_ref, rows, vl, d, h))
        done = _interleave([_gla_chunk(*vals, reverse) for vals, reverse, *_ in work])
        for (o, st_new), (_, _, o_ref, rows, vl, d, h) in zip(done, work):
            o_ref[0, rows, vl] = o
            st_ref[d, h] = st_new
        return carry

    lax.fori_loop(0, n_chunks, chunk, 0)


def gla_scan(p, plr, up_pad, up_b, col0, tl=256):
    B, L, _ = p.shape
    H = GLA_HEADS
    nt = L // tl
    kw, vw = H * GLA_DK, H * GLA_DV
    qb = col0 * LANES // kw
    vb = (col0 * LANES + 2 * kw) // vw
    assert col0 * LANES % kw == 0 and (col0 * LANES + 2 * kw) % vw == 0
    specs = []
    for tmap in (lambda t: t, lambda t: nt - 1 - t):
        specs += [pl.BlockSpec((1, tl, kw), lambda b, t, tmap=tmap: (b, tmap(t), qb)),
                  pl.BlockSpec((1, tl, kw), lambda b, t, tmap=tmap: (b, tmap(t), qb + 1)),
                  pl.BlockSpec((1, tl, vw), lambda b, t, tmap=tmap: (b, tmap(t), vb)),
                  pl.BlockSpec((1, tl, LANES), lambda b, t, tmap=tmap: (b, tmap(t), 0))]
    specs += [pl.BlockSpec((2, 3 * LANES, kw), lambda b, t: (0, 0, 0)),
              pl.BlockSpec((2, 1, kw), lambda b, t: (0, 0, 0))]
    out = jax.ShapeDtypeStruct((B, L, vw), F32)
    return pl.pallas_call(
        functools.partial(_gla_body, n_chunks=tl // GLA_CHUNK),
        grid=(B, nt),
        in_specs=specs,
        out_specs=[pl.BlockSpec((1, tl, vw), lambda b, t: (b, t, 0)),
                   pl.BlockSpec((1, tl, vw), lambda b, t: (b, nt - 1 - t, 0))],
        out_shape=[out, out],
        scratch_shapes=[pltpu.VMEM((2, H, GLA_DV, GLA_DK), F32)],
        compiler_params=_cparams("parallel", "arbitrary"),
        name="gla_scan",
    )(p, p, p, plr, p, p, p, plr, up_pad, up_b)


def _gla_post_body(of_ref, ob_ref, r_ref, gn_ref, o_ref):
    o = of_ref[0] + ob_ref[0]
    o = o * lax.rsqrt(jnp.mean(o * o, axis=-1, keepdims=True) + NORM_EPS) * gn_ref[...]
    r = r_ref[0]
    o_ref[0] = (o * (r * jax.nn.sigmoid(r))).astype(o_ref.dtype)


def gla_post(o_f, o_b, p, r_block0, norm_g, tl=512):
    B, L, W = o_f.shape
    spec = pl.BlockSpec((1, tl, GLA_DV), lambda b, t, h: (b, t, h))
    return pl.pallas_call(
        _gla_post_body,
        grid=(B, L // tl, W // GLA_DV),
        in_specs=[spec, spec,
                  pl.BlockSpec((1, tl, GLA_DV), lambda b, t, h: (b, t, r_block0 + h)),
                  pl.BlockSpec((1, GLA_DV), lambda b, t, h: (0, h))],
        out_specs=spec,
        out_shape=jax.ShapeDtypeStruct((B, L, W), BF16),
        compiler_params=_cparams("parallel", "parallel", "parallel"),
        name="gla_post",
    )(o_f, o_b, p, norm_g.reshape(1, W))


def gla_mixer(p, plr, col0, up, up_b, norm_g):
    H = GLA_HEADS
    up_pad = jnp.zeros((2, LANES, H * GLA_DK), F32)
    up_pad = up_pad.at[0, 0:GLA_RANK].set(up[0]).at[1, GLA_RANK:2 * GLA_RANK].set(up[1])
    up_hi = up_pad.astype(BF16)
    up_lo = (up_pad - up_hi.astype(F32)).astype(BF16)
    up3 = jnp.concatenate([up_hi, up_hi, up_lo], axis=1)
    o_f, o_b = gla_scan(p, plr, up3, up_b.reshape(2, 1, H * GLA_DK), col0)
    r_block0 = (col0 * LANES + 2 * H * GLA_DK + H * GLA_DV) // GLA_DV
    return gla_post(o_f, o_b, p, r_block0, norm_g)


def _rw_mix_body(x_ref, xm_ref, xp_ref, g_ref, sc_ref, sh_ref, mu_ref, *o_refs, n_tiles):
    i = pl.program_id(1)
    g, sc, sh = g_ref[...], sc_ref[0], sh_ref[0]
    h = _norm_mod_value(x_ref[0], g, sc, sh)
    hm = _norm_mod_value(xm_ref[0], g, sc, sh)
    hp = _norm_mod_value(xp_ref[0], g, sc, sh)
    hm1, hp1 = _neighbours(h, hm, hp, i, n_tiles)
    xx = 0.5 * (hm1 + hp1) - h
    mu = mu_ref[...]
    for j, o_ref in enumerate(o_refs):
        o_ref[0] = (h + xx * mu[j:j + 1]).astype(o_ref.dtype)


def rw_mix(x, g, scale, shift, mu, tl=256):
    B, L, D = x.shape
    n_tiles = L // tl
    out = jax.ShapeDtypeStruct((B, L, D), BF16)
    ospec = pl.BlockSpec((1, tl, D), lambda b, i: (b, i, 0))
    mu8 = jnp.zeros((8, D), F32).at[:6].set(mu)
    return pl.pallas_call(
        functools.partial(_rw_mix_body, n_tiles=n_tiles),
        grid=(B, n_tiles),
        in_specs=_neighbour_specs(tl, D, L // 8) + [pl.BlockSpec((1, D), lambda b, i: (0, 0)),
                                                    pl.BlockSpec((1, 1, D), lambda b, i: (b, 0, 0)),
                                                    pl.BlockSpec((1, 1, D), lambda b, i: (b, 0, 0)),
                                                    pl.BlockSpec((8, D), lambda b, i: (0, 0))],
        out_specs=[ospec] * 6,
        out_shape=[out] * 6,
        compiler_params=_cparams("parallel", "parallel"),
        name="rw_mix",
    )(x, x, x, g.reshape(1, D), scale, shift, mu8)


def _head_ones():
    r = _iota2((LANES, LANES), 0) // RW_HEAD
    c = _iota2((LANES, LANES), 1) // RW_HEAD
    return (r == c).astype(BF16)


def _rw_prep_body(k_ref, r_ref, v_ref, a0_ref, a1_ref, kk_w_ref, ka_ref, rk_ref, kk_ref, bon_ref):
    ones = _head_ones()
    k = k_ref[...]
    kk = k * kk_w_ref[...]
    ss = _dot_mask(kk * kk, ones)
    kk_ref[...] = kk * lax.rsqrt(jnp.maximum(ss, 1e-24))
    a_sum = jax.nn.sigmoid(a0_ref[...]) + jax.nn.sigmoid(a1_ref[...])
    kd_sum = k * (2.0 + (a_sum - 2.0) * ka_ref[...])
    bon_ref[...] = _dot_mask(r_ref[...] * kd_sum * rk_ref[...], ones) * v_ref[...]


def rw_prep(k, r, v, apre0, apre1, k_k, k_a, r_k, tm=2048):
    T, D = k.shape
    tm = min(tm, T)
    spec = pl.BlockSpec((tm, LANES), lambda i, j: (i, j))
    pspec = pl.BlockSpec((1, LANES), lambda i, j: (0, j))
    out = jax.ShapeDtypeStruct((T, D), F32)
    return pl.pallas_call(
        _rw_prep_body,
        grid=(T // tm, D // LANES),
        in_specs=[spec, spec, spec, spec, spec, pspec, pspec, pspec],
        out_specs=[spec, spec],
        out_shape=[out, out],
        compiler_params=_cparams("parallel", "parallel"),
        name="rw_prep",
    )(k, r, v, apre0, apre1, k_k.reshape(1, D), k_a.reshape(1, D), r_k.reshape(1, D))


def _rw_chunk(r, k, v, kk, wpre, apre, ka, h, reverse):
    C = RW_CHUNK
    P = 2 * C
    rc = _iota2((C, C), 0)
    cc = _iota2((C, C), 1)
    tri = ((rc <= cc) if reverse else (rc >= cc)).astype(BF16)
    t_idx = _iota2((C, P), 0)
    s_idx = _iota2((C, P), 1) % C
    strict = (t_idx < s_idx) if reverse else (t_idx > s_idx)
    incl = (t_idx <= s_idx) if reverse else (t_idx >= s_idx)
    head0 = _iota2((C, LANES), 1) < RW_HEAD
    same_head = (_iota2((LANES, LANES), 0) // RW_HEAD) == (_iota2((LANES, LANES), 1) // RW_HEAD)

    def stack(x):
        return jnp.concatenate([jnp.where(head0, x, 0.0), jnp.where(head0, 0.0, x)], axis=0).astype(BF16)

    lw = -RW_LOG_DECAY_MAX * jax.nn.sigmoid(wpre)
    a = jax.nn.sigmoid(apre)
    kd = k * (1.0 + (a - 1.0) * ka)
    b = kk * a
    cum = _mask_dot(tri, lw)
    yield None
    cum_x = cum - lw
    total = cum[0:1] if reverse else cum[C - 1:C]
    e_neg = jnp.exp(-cum)
    rn = (r * jnp.exp(cum)).astype(BF16)
    xn = (kk * jnp.exp(cum_x)).astype(BF16)
    kn = kd * e_neg
    bn = b * e_neg
    vs = stack(v)
    coef = _dot_nt(jnp.concatenate([xn, rn], axis=0), jnp.concatenate([stack(kn), stack(bn)], axis=0))
    yield None
    a_k = jnp.where(strict, coef[:C, :P], 0.0).astype(BF16)
    pw = jnp.where(strict, -coef[:C, P:], 0.0)
    q_k = jnp.where(incl, coef[C:, :P], 0.0).astype(BF16)
    q_b = jnp.where(incl, coef[C:, P:], 0.0).astype(BF16)
    hv = jnp.concatenate([h.astype(BF16), vs], axis=0)
    x = _dot(jnp.concatenate([xn, a_k], axis=1), hv)
    o_hv = _dot(jnp.concatenate([rn, q_k], axis=1), hv)
    yield None
    for _ in range(int(math.log2(C)) - 1):
        both = _dot(pw.astype(BF16), jnp.concatenate([stack(x), stack(pw)], axis=1))
        x = x + both[:, :P]
        pw = both[:, P:]
        yield None
    u = x + _dot(pw.astype(BF16), stack(x))
    yield None
    o = o_hv - _dot(q_b, stack(u))
    upd = _dot_tn(jnp.concatenate([kn, bn], axis=0).astype(BF16),
                  jnp.concatenate([v, -u], axis=0).astype(BF16))
    gcol = jnp.transpose(jnp.broadcast_to(jnp.exp(total), (LANES, LANES)))
    h_new = gcol * (h + jnp.where(same_head, upd, 0.0))
    yield o, h_new


def _interleave(chains):
    while True:
        outs = [next(ch) for ch in chains]
        if outs[0] is not None:
            return outs


def _rw_scan_body(rf_ref, kf_ref, vf_ref, kkf_ref, wf_ref, af_ref, rb_ref, kb_ref, vb_ref, kkb_ref, wb_ref,
                  ab_ref, ka_ref, of_ref, ob_ref, h_ref, *, n_chunks, n_pairs):
    C = RW_CHUNK

    @pl.when(pl.program_id(2) == 0)
    def _():
        h_ref[...] = jnp.zeros_like(h_ref)

    def chunk(step, carry):
        dirs = ((0, step, False, (rf_ref, kf_ref, vf_ref, kkf_ref, wf_ref, af_ref), of_ref),
                (1, n_chunks - 1 - step, True, (rb_ref, kb_ref, vb_ref, kkb_ref, wb_ref, ab_ref), ob_ref))
        work = []
        for d, c, reverse, refs, o_ref in dirs:
            rows = pl.ds(pl.multiple_of(c * C, C), C)
            for g in range(n_pairs):
                lanes = slice(g * LANES, (g + 1) * LANES)
                vals = [ref[0, rows, lanes] for ref in refs]
                work.append((vals, ka_ref[:, lanes], h_ref[d, g], reverse, o_ref, rows, lanes, d, g))
        done = _interleave([_rw_chunk(*vals, ka, h, reverse) for vals, ka, h, reverse, *_ in work])
        for (o, h_new), (_, _, _, _, o_ref, rows, lanes, d, g) in zip(done, work):
            o_ref[0, rows, lanes] = o
            h_ref[d, g] = h_new
        return carry

    lax.fori_loop(0, n_chunks, chunk, 0)


def rw_scan(r, k, v, kk, wpre, apre, k_a, tl=256, n_pairs=8):
    B, L, D = r.shape
    width = n_pairs * LANES
    nt = L // tl
    fspec = pl.BlockSpec((1, tl, width), lambda b, p, t: (b, t, p))
    bspec = pl.BlockSpec((1, tl, width), lambda b, p, t: (b, nt - 1 - t, p))
    out = jax.ShapeDtypeStruct((B, L, D), F32)
    return pl.pallas_call(
        functools.partial(_rw_scan_body, n_chunks=tl // RW_CHUNK, n_pairs=n_pairs),
        grid=(B, D // width, nt),
        in_specs=[fspec] * 6 + [bspec] * 6 + [pl.BlockSpec((1, width), lambda b, p, t: (0, p))],
        out_specs=[fspec, bspec],
        out_shape=[out, out],
        scratch_shapes=[pltpu.VMEM((2, n_pairs, LANES, LANES), F32)],
        compiler_params=_cparams("parallel", "parallel", "arbitrary"),
        name="rw_scan",
    )(r, k, v, kk, wpre[0], apre[0], r, k, v, kk, wpre[1], apre[1], k_a.reshape(1, D))


def _rw_post_body(of_ref, ob_ref, bon_ref, g_ref, gng_ref, gnb_ref, o_ref):
    ones = _head_ones()
    so = of_ref[...] + ob_ref[...]
    mean = _dot_mask(so, ones) * (1.0 / RW_HEAD)
    dlt = so - mean
    var = _dot_mask(dlt * dlt, ones) * (1.0 / RW_HEAD)
    y = dlt * lax.rsqrt(var + RW_GN_EPS) * gng_ref[...] + gnb_ref[...]
    o_ref[...] = ((y + bon_ref[...]) * g_ref[...]).astype(o_ref.dtype)


def rw_post(o_f, o_b, bonus, g, gn_g, gn_b, tm=2048):
    T, D = o_f.shape
    tm = min(tm, T)
    spec = pl.BlockSpec((tm, LANES), lambda i, j: (i, j))
    pspec = pl.BlockSpec((1, LANES), lambda i, j: (0, j))
    return pl.pallas_call(
        _rw_post_body,
        grid=(T // tm, D // LANES),
        in_specs=[spec, spec, spec, spec, pspec, pspec],
        out_specs=spec,
        out_shape=jax.ShapeDtypeStruct((T, D), BF16),
        compiler_params=_cparams("parallel", "parallel"),
        name="rw_post",
    )(o_f, o_b, bonus, g, gn_g.reshape(1, D), gn_b.reshape(1, D))


def _pad_cols(w, n):
    return jnp.pad(w, ((0, 0), (0, n - w.shape[1])))


def _pad_rows(w, n):
    return jnp.pad(w, ((0, n - w.shape[0]), (0, 0)))


def rwkv7_block(x, gate, g_norm, scale, shift, mu, w_r, w_k, w_v, w0, w1, w2, a0, a1, a2, g1, g2,
                k_k, k_a, r_k, gn_g, gn_b, w_o):
    B, L, D = x.shape
    T = B * L
    xr, xw, xk, xv, xa, xg = [a.reshape(T, D) for a in rw_mix(x, g_norm, scale, shift, mu)]
    r = matmul(xr, w_r.astype(BF16))
    k = matmul(xk, w_k.astype(BF16))
    v = matmul(xv, w_v.astype(BF16))
    w1c = jnp.concatenate([_pad_cols(w1[0], LANES), _pad_cols(w1[1], LANES)], axis=1).astype(BF16)
    a1c = jnp.concatenate([_pad_cols(a1[0], LANES), _pad_cols(a1[1], LANES)], axis=1).astype(BF16)
    wmid = matmul(xw, w1c)
    amid = matmul(xa, a1c)
    gmid = matmul(xg, g1.astype(BF16))
    wpre = [matmul(wmid, _pad_rows(w2[d], LANES).astype(BF16), a_act="tanh", bias=w0[d], a_col_block=d)
            for d in range(2)]
    apre = [matmul(amid, _pad_rows(a2[d], LANES).astype(BF16), bias=a0[d], a_col_block=d)
            for d in range(2)]
    g = matmul(gmid, g2.astype(BF16), a_act="sigmoid")
    kk, bonus = rw_prep(k, r, v, apre[0], apre[1], k_k, k_a, r_k.reshape(D))
    sh = lambda a: a.reshape(B, L, D)
    o_f, o_b = rw_scan(sh(r), sh(k), sh(v), sh(kk), [sh(a) for a in wpre], [sh(a) for a in apre], k_a)
    y = rw_post(o_f.reshape(T, D), o_b.reshape(T, D), bonus, g, gn_g, gn_b)
    out = matmul(y, w_o.astype(BF16), res=x.reshape(T, D), gate=gate, rows_per_batch=L)
    return out.reshape(B, L, D)


def ffn_block(x, c, norm_g, ada_w, ada_b, idx, w1, w2):
    B, L, D = x.shape
    T = B * L
    shift, scale, gate = ada_mod(c, ada_w, ada_b, idx)
    h = norm_mod(x, norm_g, scale, shift).reshape(T, D)
    hid = matmul(h, w1.astype(BF16), w_idx=idx, out_dtype=BF16, epi="sqrelu", tm=2048)
    out = matmul(hid, w2.astype(BF16), w_idx=idx, res=x.reshape(T, D), gate=gate, rows_per_batch=L,
                 tm=512, tn=512)
    return out.reshape(B, L, D)


def hyena_gla_block(x, gate, h, w_in, w_out, conv_w, conv_b, f_w1, f_b1, f_freq1, f_w2, f_b2, f_freq2,
                    f_w3, f_decay, hy_skip, gla_up, gla_up_b, gla_norm_g):
    B, L, D = x.shape
    T = B * L
    C = conv_w.shape[1] // 3
    n_main = w_in.shape[1] - 2 * GLA_RANK
    p = matmul(h, w_in[:, :n_main].astype(BF16), tm=2048).reshape(B, L, n_main)
    plr = matmul(h, _pad_cols(w_in[:, n_main:], LANES).astype(BF16)).reshape(B, L, LANES)
    y_hy = hyena_mixer(p, C, conv_w, conv_b, f_w1, f_b1, f_freq1, f_w2, f_b2, f_freq2, f_w3, f_decay, hy_skip)
    y_gla = gla_mixer(p, plr, 3 * C // LANES, gla_up, gla_up_b, gla_norm_g)
    out = matmul(y_hy.reshape(T, -1), w_out.astype(BF16), a2=y_gla.reshape(T, -1), res=x.reshape(T, D),
                 gate=gate, rows_per_batch=L)
    return out.reshape(B, L, D)


def kernel(x, c, mix_norm_g, mix_ada_w, mix_ada_b, ab_w_in, ab_w_out, hy_conv_w, hy_conv_b, hy_ffn_w1, hy_ffn_b1, hy_freq1, hy_ffn_w2, hy_ffn_b2, hy_freq2, hy_ffn_w3, hy_decay, hy_skip, gla_up, gla_up_b, gla_norm_g, tm_norm_g, tm_ada_w, tm_ada_b, rw_mu, rw_w_r, rw_w_k, rw_w_v, rw_w0, rw_w1, rw_w2, rw_a0, rw_a1, rw_a2, rw_g1, rw_g2, rw_k_k, rw_k_a, rw_r_k, rw_gn_g, rw_gn_b, rw_w_o, ffn_norm_g, ffn_ada_w, ffn_ada_b, ffn_w1, ffn_w2, final_norm_g):
    B, L, D = x.shape
    depth = ffn_w1.shape[0]
    for layer in range(depth):
        i = layer // 2
        if layer % 2 == 0:
            shift, scale, gate = ada_mod(c, mix_ada_w, mix_ada_b, i)
            h = norm_mod(x, mix_norm_g[i], scale, shift).reshape(B * L, D)
            x = hyena_gla_block(x, gate, h, ab_w_in[i], ab_w_out[i], hy_conv_w[i], hy_conv_b[i],
                                hy_ffn_w1[i], hy_ffn_b1[i], hy_freq1[i], hy_ffn_w2[i], hy_ffn_b2[i],
                                hy_freq2[i], hy_ffn_w3[i], hy_decay[i], hy_skip[i],
                                gla_up[i], gla_up_b[i], gla_norm_g[i])
        else:
            shift, scale, gate = ada_mod(c, tm_ada_w, tm_ada_b, i)
            x = rwkv7_block(x, gate, tm_norm_g[i], scale, shift, rw_mu[i], rw_w_r[i], rw_w_k[i], rw_w_v[i],
                            rw_w0[i], rw_w1[i], rw_w2[i], rw_a0[i], rw_a1[i], rw_a2[i], rw_g1[i], rw_g2[i],
                            rw_k_k[i], rw_k_a[i], rw_r_k[i], rw_gn_g[i], rw_gn_b[i], rw_w_o[i])
        x = ffn_block(x, c, ffn_norm_g[layer], ffn_ada_w, ffn_ada_b, layer, ffn_w1, ffn_w2)
    return rms_final(x, final_norm_g)
```

```python
import functools
import math

import numpy as np
import jax
import jax.numpy as jnp
from jax import lax
from jax.experimental import pallas as pl
from jax.experimental.pallas import tpu as pltpu

F32 = jnp.float32
BF16 = jnp.bfloat16
HI = lax.Precision.HIGHEST

LANES = 128
SUBLANES = 8
NORM_EPS = 1e-6
GLA_HEADS = 4
GLA_DK = 128
GLA_DV = 256
GLA_RANK = 16
GLA_TAU = 16.0
GLA_CHUNK = 64
RW_HEAD = 64
RW_CHUNK = 64
RW_LOG_DECAY_MAX = 0.606531
RW_GN_EPS = 64e-5
HY_ORDER = 2
HY_EMB = 33
HY_FFN = 64
DFT_N2 = 128
VMEM_LIMIT = 52 * 1024 * 1024


def _cparams(*sem):
    return pltpu.CompilerParams(dimension_semantics=sem, vmem_limit_bytes=VMEM_LIMIT)


def _dot(a, b, precision=None):
    return jnp.dot(a, b, preferred_element_type=F32, precision=precision)


def _dot_nt(a, b, precision=None):
    return lax.dot_general(a, b, (((1,), (1,)), ((), ())), preferred_element_type=F32, precision=precision)


def _dot_tn(a, b, precision=None):
    return lax.dot_general(a, b, (((0,), (0,)), ((), ())), preferred_element_type=F32, precision=precision)


def _iota2(shape, axis):
    return lax.broadcasted_iota(jnp.int32, shape, axis)


def _log_sigmoid(x):
    return jnp.minimum(x, 0.0) - jnp.log(1.0 + jnp.exp(-jnp.abs(x)))


def _split(x):
    hi = x.astype(BF16)
    return hi, (x - hi.astype(F32)).astype(BF16)


def _mask_dot(mask_bf16, x):
    hi, lo = _split(x)
    n = x.shape[1]
    y = _dot(mask_bf16, jnp.concatenate([hi, lo], axis=1))
    return y[:, :n] + y[:, n:]


def _stack3(a):
    a = np.asarray(a, np.float32)
    hi = a.astype(BF16)
    lo = (a - hi.astype(np.float32)).astype(BF16)
    return jnp.asarray(np.concatenate([hi, hi, lo], axis=1))


def _dot3(a3, x):
    hi, lo = _split(x)
    return _dot(a3, jnp.concatenate([hi, lo, hi], axis=0))


def _stack3_rows(w):
    hi, lo = _split(w)
    return jnp.concatenate([hi, hi, lo], axis=0)


def _dot3r(x, w3):
    hi, lo = _split(x)
    return _dot(jnp.concatenate([hi, lo, hi], axis=1), w3)


def _dot_mask(x, mask_bf16):
    hi, lo = _split(x)
    return _dot(hi, mask_bf16) + _dot(lo, mask_bf16)


def _ada_body(c_ref, w_ref, b_ref, o_ref):
    c = c_ref[...]
    s = c * jax.nn.sigmoid(c)
    o_ref[...] = _dot(s, w_ref[0], HI) + b_ref[0]


def ada_mod(c, w, b, idx):
    B, D = c.shape
    n3 = w.shape[2]
    tn = 512
    out = pl.pallas_call(
        _ada_body,
        grid=(n3 // tn,),
        in_specs=[pl.BlockSpec((B, D), lambda j: (0, 0)),
                  pl.BlockSpec((1, D, tn), lambda j: (idx, 0, j)),
                  pl.BlockSpec((1, 1, tn), lambda j: (idx, 0, j))],
        out_specs=pl.BlockSpec((B, tn), lambda j: (0, j)),
        out_shape=jax.ShapeDtypeStruct((B, n3), F32),
        compiler_params=_cparams("parallel"),
        name="ada_mod",
    )(c, w, b.reshape(b.shape[0], 1, n3))
    shift, scale, gate = jnp.split(out[:, None, :], 3, axis=-1)
    return shift, scale, gate


def _norm_mod_value(x, g, scale, shift):
    y = x * lax.rsqrt(jnp.mean(x * x, axis=-1, keepdims=True) + NORM_EPS)
    return (y * g) * (1.0 + scale) + shift


def _norm_mod_body(x_ref, g_ref, sc_ref, sh_ref, o_ref):
    o_ref[0] = _norm_mod_value(x_ref[0], g_ref[...], sc_ref[0], sh_ref[0]).astype(o_ref.dtype)


def norm_mod(x, g, scale, shift, out_dtype=BF16, tm=512):
    B, L, D = x.shape
    return pl.pallas_call(
        _norm_mod_body,
        grid=(B, L // tm),
        in_specs=[pl.BlockSpec((1, tm, D), lambda b, i: (b, i, 0)),
                  pl.BlockSpec((1, D), lambda b, i: (0, 0)),
                  pl.BlockSpec((1, 1, D), lambda b, i: (b, 0, 0)),
                  pl.BlockSpec((1, 1, D), lambda b, i: (b, 0, 0))],
        out_specs=pl.BlockSpec((1, tm, D), lambda b, i: (b, i, 0)),
        out_shape=jax.ShapeDtypeStruct((B, L, D), out_dtype),
        compiler_params=_cparams("parallel", "parallel"),
        name="norm_mod",
    )(x, g.reshape(1, D), scale, shift)


def _rms_body(x_ref, g_ref, o_ref):
    x = x_ref[0]
    y = x * lax.rsqrt(jnp.mean(x * x, axis=-1, keepdims=True) + NORM_EPS)
    o_ref[0] = y * g_ref[...]


def rms_final(x, g, tm=512):
    B, L, D = x.shape
    return pl.pallas_call(
        _rms_body,
        grid=(B, L // tm),
        in_specs=[pl.BlockSpec((1, tm, D), lambda b, i: (b, i, 0)),
                  pl.BlockSpec((1, D), lambda b, i: (0, 0))],
        out_specs=pl.BlockSpec((1, tm, D), lambda b, i: (b, i, 0)),
        out_shape=jax.ShapeDtypeStruct((B, L, D), F32),
        compiler_params=_cparams("parallel", "parallel"),
        name="rms_final",
    )(x, g.reshape(1, D))


def _mm_body(*refs, nk, a_act, epi, has_a2, has_bias, has_res):
    it = iter(refs)
    a_ref = next(it)
    w_ref = next(it)
    a2_ref = next(it) if has_a2 else None
    w2_ref = next(it) if has_a2 else None
    bias_ref = next(it) if has_bias else None
    res_ref = next(it) if has_res else None
    gate_ref = next(it) if has_res else None
    o_ref = next(it)
    acc_ref = next(it) if nk > 1 else None

    a = a_ref[...]
    if a_act == "tanh":
        a = jnp.tanh(a.astype(F32))
    elif a_act == "sigmoid":
        a = jax.nn.sigmoid(a.astype(F32))
    part = _dot(a.astype(BF16), w_ref[0].astype(BF16))
    if has_a2:
        part = part + _dot(a2_ref[...].astype(BF16), w2_ref[0].astype(BF16))

    def finish(acc):
        if has_bias:
            acc = acc + bias_ref[...]
        if epi == "sqrelu":
            acc = jnp.square(jnp.maximum(acc, 0.0))
        if has_res:
            acc = res_ref[...] + gate_ref[0] * acc
        o_ref[...] = acc.astype(o_ref.dtype)

    if nk == 1:
        finish(part)
    else:
        k = pl.program_id(2)

        @pl.when(k == 0)
        def _():
            acc_ref[...] = part

        @pl.when(k > 0)
        def _():
            acc_ref[...] += part

        @pl.when(k == nk - 1)
        def _():
            finish(acc_ref[...])


def matmul(a, w, *, a2=None, w_idx=0, n_cols=None, out_dtype=F32, a_act=None, epi=None, bias=None, res=None,
           gate=None, rows_per_batch=None, tm=1024, tn=1024, tk=None, a_col_block=0):
    M = a.shape[0]
    if w.ndim == 2:
        w = w[None]
    K = w.shape[1]
    N = w.shape[2] if n_cols is None else n_cols
    has_a2 = a2 is not None
    ka = K // 2 if has_a2 else K
    tm = min(tm, M if rows_per_batch is None else rows_per_batch)
    tn = min(tn, N)
    tk = ka if tk is None else min(tk, ka)
    nk = ka // tk
    assert M % tm == 0 and N % tn == 0 and ka % tk == 0 and not (has_a2 and nk > 1)
    has_bias = bias is not None
    has_res = res is not None
    in_specs = [pl.BlockSpec((tm, tk), lambda i, j, k: (i, k + a_col_block * nk)),
                pl.BlockSpec((1, tk, tn), lambda i, j, k: (w_idx, k, j))]
    args = [a, w]
    if has_a2:
        in_specs += [pl.BlockSpec((tm, tk), lambda i, j, k: (i, 0)),
                     pl.BlockSpec((1, tk, tn), lambda i, j, k: (w_idx, 1, j))]
        args += [a2, w]
    if has_bias:
        in_specs.append(pl.BlockSpec((1, tn), lambda i, j, k: (0, j)))
        args.append(bias.reshape(1, N).astype(F32))
    if has_res:
        tiles_per_batch = rows_per_batch // tm
        assert rows_per_batch % tm == 0
        in_specs.append(pl.BlockSpec((tm, tn), lambda i, j, k: (i, j)))
        in_specs.append(pl.BlockSpec((1, 1, tn), lambda i, j, k: (i // tiles_per_batch, 0, j)))
        args += [res, gate]
    scratch = [pltpu.VMEM((tm, tn), F32)] if nk > 1 else []
    body = functools.partial(_mm_body, nk=nk, a_act=a_act, epi=epi, has_a2=has_a2, has_bias=has_bias,
                             has_res=has_res)
    return pl.pallas_call(
        body,
        grid=(M // tm, N // tn, nk),
        in_specs=in_specs,
        out_specs=pl.BlockSpec((tm, tn), lambda i, j, k: (i, j)),
        out_shape=jax.ShapeDtypeStruct((M, N), out_dtype),
        scratch_shapes=scratch,
        compiler_params=_cparams("parallel", "parallel", "arbitrary"),
        name="matmul",
    )(*args)


def _neighbours(cur, prev8, next8, i, n_tiles):
    t = cur.shape[0]
    row = _iota2(cur.shape, 0)
    before = jnp.where(i > 0, prev8[7:8, :], 0.0)
    after = jnp.where(i < n_tiles - 1, next8[0:1, :], 0.0)
    xm1 = jnp.where(row == 0, before, pltpu.roll(cur, 1, 0))
    xp1 = jnp.where(row == t - 1, after, pltpu.roll(cur, t - 1, 0))
    return xm1, xp1


def _neighbour_specs(tl, width, n_rows8, col_map=None):
    r8 = tl // 8
    return [pl.BlockSpec((1, tl, width), lambda b, i: (b, i, 0)),
            pl.BlockSpec((1, 8, width), lambda b, i: (b, jnp.maximum(i * r8 - 1, 0), 0)),
            pl.BlockSpec((1, 8, width), lambda b, i: (b, jnp.minimum((i + 1) * r8, n_rows8 - 1), 0))]


def _hy_conv_body(p_ref, pm_ref, pp_ref, w_ref, b_ref, v_ref, x1_ref, x2_ref, *, n_tiles, C):
    i = pl.program_id(1)
    cur = p_ref[0]
    xm1, xp1 = _neighbours(cur, pm_ref[0], pp_ref[0], i, n_tiles)
    w = w_ref[...]
    u = xm1 * w[0:1] + cur * w[1:2] + xp1 * w[2:3] + b_ref[...]
    v_ref[0] = u[:, 0:C]
    x1_ref[0] = u[:, C:2 * C]
    x2_ref[0] = u[:, 2 * C:3 * C]


def hy_short_conv(p, conv_w, conv_b, C, tl=256):
    B, L, _ = p.shape
    W = 3 * C
    n_tiles = L // tl
    out = jax.ShapeDtypeStruct((B, L, C), F32)
    ospec = pl.BlockSpec((1, tl, C), lambda b, i: (b, i, 0))
    return pl.pallas_call(
        functools.partial(_hy_conv_body, n_tiles=n_tiles, C=C),
        grid=(B, n_tiles),
        in_specs=_neighbour_specs(tl, W, L // 8) + [pl.BlockSpec((3, W), lambda b, i: (0, 0)),
                                                    pl.BlockSpec((1, W), lambda b, i: (0, 0))],
        out_specs=[ospec, ospec, ospec],
        out_shape=[out, out, out],
        compiler_params=_cparams("parallel", "parallel"),
        name="hy_short_conv",
    )(p, p, p, conv_w, conv_b.reshape(1, W))


def _hy_filter_body(z_ref, w1_ref, b1_ref, f1_ref, w2_ref, b2_ref, f2_ref, w3_ref, dec_ref, o_ref, *, L, tm):
    i = pl.program_id(0)
    z = z_ref[...]
    f = jnp.sin(f1_ref[...] * (_dot3r(z, w1_ref[...]) + b1_ref[...]))
    f = jnp.sin(f2_ref[...] * (_dot3r(f, w2_ref[...]) + b2_ref[...]))
    t = z[:, 0:1]
    f = _dot3r(f, w3_ref[...]) * jnp.exp(-t * jnp.abs(dec_ref[...]))
    n = i * tm + _iota2(f.shape, 0)
    o_ref[...] = jnp.where(n == L, 0.0, f)


def hy_filter(L, C, w1, b1, freq1, w2, b2, freq2, w3, decay, tm=512):
    pos = np.concatenate([np.arange(L), [0], np.arange(L - 1, 0, -1)]).astype(np.float64)
    t = pos / (L - 1)
    ang = 2.0 * math.pi * pos / L
    nb = (HY_EMB - 1) // 2
    ba = np.linspace(1e-4, nb - 1, nb)[None, :] * ang[:, None]
    z = np.concatenate([t[:, None], np.cos(ba), -np.sin(ba)], axis=-1)
    zp = np.zeros((2 * L, 2 * HY_FFN), np.float32)
    zp[:, :HY_EMB] = z
    w1p = jnp.zeros((2 * HY_FFN, HY_FFN), F32).at[:HY_EMB].set(w1)
    nL = L // tm
    row = lambda a: a.reshape(1, -1)
    return pl.pallas_call(
        functools.partial(_hy_filter_body, L=L, tm=tm),
        grid=(2 * L // tm, HY_ORDER),
        in_specs=[pl.BlockSpec((tm, 2 * HY_FFN), lambda i, o: (i, 0)),
                  pl.BlockSpec((6 * HY_FFN, HY_FFN), lambda i, o: (0, 0)),
                  pl.BlockSpec((1, HY_FFN), lambda i, o: (0, 0)),
                  pl.BlockSpec((1, HY_FFN), lambda i, o: (0, 0)),
                  pl.BlockSpec((3 * HY_FFN, HY_FFN), lambda i, o: (0, 0)),
                  pl.BlockSpec((1, HY_FFN), lambda i, o: (0, 0)),
                  pl.BlockSpec((1, HY_FFN), lambda i, o: (0, 0)),
                  pl.BlockSpec((3 * HY_FFN, C), lambda i, o: (0, 2 * o + i // nL)),
                  pl.BlockSpec((1, C), lambda i, o: (0, 2 * o + i // nL))],
        out_specs=pl.BlockSpec((tm, C), lambda i, o: (i, o)),
        out_shape=jax.ShapeDtypeStruct((2 * L, HY_ORDER * C), F32),
        compiler_params=_cparams("parallel", "parallel"),
        name="hy_filter",
    )(jnp.asarray(zp), _stack3_rows(w1p), row(b1), row(freq1), _stack3_rows(w2), row(b2), row(freq2),
      _stack3_rows(w3), row(decay))


def _dft_constants(L):
    N = 2 * L
    N2 = DFT_N2
    N1 = N // N2
    K1 = N1 // 2 + 1
    k1 = np.arange(K1, dtype=np.float64)
    n1 = np.arange(N1, dtype=np.float64)
    ang = 2.0 * np.pi * np.outer(k1, n1) / N1
    a1 = np.stack([np.cos(ang), -np.sin(ang)], axis=1).reshape(2 * K1, N1)
    n1h = np.arange(N1 // 2, dtype=np.float64)
    angb = 2.0 * np.pi * np.outer(n1h, k1) / N1
    ck = np.where((np.arange(K1) == 0) | (np.arange(K1) == N1 // 2), 1.0, 2.0)[None, :]
    binv = np.stack([ck * np.cos(angb), -ck * np.sin(angb)], axis=2).reshape(N1 // 2, 2 * K1) / N
    n2 = np.arange(N2, dtype=np.float64)
    ph = -2.0 * np.pi * (np.outer(n2, n2) % N2) / N2
    dre, dim = np.cos(ph), np.sin(ph)
    dst = np.block([[dre, -dim], [dim, dre]])
    kron = lambda a: _stack3(np.kron(a, np.eye(SUBLANES)))
    return dict(N1=N1, K1=K1, a1_full=kron(a1), a1_half=kron(a1[:, :N1 // 2]), binv=kron(binv),
                d=_stack3(dst), dt=_stack3(dst.T))


def _twiddle(k1, n_total, width):
    n2 = _iota2((DFT_N2, LANES), 0).astype(F32)
    ang = (-2.0 * math.pi / n_total) * (k1.astype(F32) * n2)
    reps = width // LANES
    return jnp.tile(jnp.cos(ang), (1, reps)), jnp.tile(jnp.sin(ang), (1, reps))


def _inner_forward(d, y, k1, n_total):
    h = DFT_N2
    c, s = _twiddle(k1, n_total, y.shape[1])
    yr, yi = y[:h], y[h:]
    return _dot3(d, jnp.concatenate([yr * c - yi * s, yr * s + yi * c], axis=0))


def _dft_outer_body(a_ref, x_ref, o_ref):
    _, k, s, ct = x_ref.shape
    y = _dot3(a_ref[...], x_ref[0].reshape(k * s, ct))
    o_ref[0] = y.reshape(o_ref.shape[1], s, ct)


def dft_outer(a_kron, x, ct):
    B, K, N2, W = x.shape
    R = a_kron.shape[0] // SUBLANES
    ct = min(ct, W)
    return pl.pallas_call(
        _dft_outer_body,
        grid=(B, N2 // SUBLANES, W // ct),
        in_specs=[pl.BlockSpec(a_kron.shape, lambda b, s, j: (0, 0)),
                  pl.BlockSpec((1, K, SUBLANES, ct), lambda b, s, j: (b, 0, s, j))],
        out_specs=pl.BlockSpec((1, R, SUBLANES, ct), lambda b, s, j: (b, 0, s, j)),
        out_shape=jax.ShapeDtypeStruct((B, R, N2, W), F32),
        compiler_params=_cparams("parallel", "parallel", "parallel"),
        name="dft_outer",
    )(a_kron, x)


def _dft_inner_body(d_ref, y_ref, o_ref, *, n_total):
    o_ref[0, 0] = _inner_forward(d_ref[...], y_ref[0, 0], pl.program_id(0), n_total)


def dft_inner(d, y, ct, n_total):
    B, K1, R, W = y.shape
    return pl.pallas_call(
        functools.partial(_dft_inner_body, n_total=n_total),
        grid=(K1, B, W // ct),
        in_specs=[pl.BlockSpec(d.shape, lambda k, b, j: (0, 0)),
                  pl.BlockSpec((1, 1, R, ct), lambda k, b, j: (b, k, 0, j))],
        out_specs=pl.BlockSpec((1, 1, R, ct), lambda k, b, j: (b, k, 0, j)),
        out_shape=jax.ShapeDtypeStruct(y.shape, F32),
        compiler_params=_cparams("parallel", "parallel", "parallel"),
        name="dft_inner",
    )(d, y)


def _spec_mul_body(d_ref, dt_ref, y_ref, kf_ref, o_ref, *, n_total):
    k1 = pl.program_id(0)
    z = _inner_forward(d_ref[...], y_ref[0, 0], k1, n_total)
    h = DFT_N2
    zr, zi = z[:h], z[h:]
    kf = kf_ref[0, 0]
    kr, ki = kf[:h], kf[h:]
    q = _dot3(dt_ref[...], jnp.concatenate([zr * kr - zi * ki, zr * ki + zi * kr], axis=0))
    c, s = _twiddle(k1, n_total, q.shape[1])
    qr, qi = q[:h], q[h:]
    o_ref[0, 0] = jnp.concatenate([qr * c + qi * s, qi * c - qr * s], axis=0)


def spec_mul(d, dt, y, kf, order, C, ct, n_total):
    B, K1, R, _ = y.shape
    nc = C // ct
    return pl.pallas_call(
        functools.partial(_spec_mul_body, n_total=n_total),
        grid=(K1, B, nc),
        in_specs=[pl.BlockSpec(d.shape, lambda k, b, j: (0, 0)),
                  pl.BlockSpec(dt.shape, lambda k, b, j: (0, 0)),
                  pl.BlockSpec((1, 1, R, ct), lambda k, b, j: (b, k, 0, j)),
                  pl.BlockSpec((1, 1, R, ct), lambda k, b, j: (0, k, 0, order * nc + j))],
        out_specs=pl.BlockSpec((1, 1, R, ct), lambda k, b, j: (b, k, 0, j)),
        out_shape=jax.ShapeDtypeStruct(y.shape, F32),
        compiler_params=_cparams("parallel", "parallel", "parallel"),
        name="hy_spec_mul",
    )(d, dt, y, kf)


def _dft_out_body(a_ref, t_ref, u_ref, g_ref, s_ref, o_ref):
    _, k, s, ct = t_ref.shape
    y = _dot3(a_ref[...], t_ref[0].reshape(k * s, ct)).reshape(u_ref.shape[1], s, ct)
    o_ref[0] = g_ref[0] * (y + u_ref[0] * s_ref[...])


def dft_out(binv_kron, t, u, gate, skip):
    B, K, N2, C = t.shape
    R = u.shape[1]
    uspec = pl.BlockSpec((1, R, SUBLANES, C), lambda b, s: (b, 0, s, 0))
    return pl.pallas_call(
        _dft_out_body,
        grid=(B, N2 // SUBLANES),
        in_specs=[pl.BlockSpec(binv_kron.shape, lambda b, s: (0, 0)),
                  pl.BlockSpec((1, K, SUBLANES, C), lambda b, s: (b, 0, s, 0)),
                  uspec, uspec,
                  pl.BlockSpec((1, C), lambda b, s: (0, 0))],
        out_specs=uspec,
        out_shape=jax.ShapeDtypeStruct(u.shape, F32),
        compiler_params=_cparams("parallel", "parallel"),
        name="dft_out",
    )(binv_kron, t, u, gate, skip)


def hyena_mixer(p, C, conv_w, conv_b, f_w1, f_b1, f_freq1, f_w2, f_b2, f_freq2, f_w3, f_decay, skip):
    B, L, _ = p.shape
    cst = _dft_constants(L)
    N1, K1 = cst["N1"], cst["K1"]
    N2 = DFT_N2
    v, x1, x2 = hy_short_conv(p, conv_w, conv_b, C)
    kc = hy_filter(L, C, f_w1, f_b1, f_freq1, f_w2, f_b2, f_freq2, f_w3, f_decay)
    yf = dft_outer(cst["a1_full"], kc.reshape(1, N1, N2, HY_ORDER * C), C)
    kf = dft_inner(cst["d"], yf.reshape(1, K1, 2 * N2, HY_ORDER * C), C, 2 * L)

    def long_conv(u, order, gate):
        u4 = u.reshape(B, N1 // 2, N2, C)
        y1 = dft_outer(cst["a1_half"], u4, C)
        t = spec_mul(cst["d"], cst["dt"], y1.reshape(B, K1, 2 * N2, C), kf, order, C, C, 2 * L)
        out = dft_out(cst["binv"], t.reshape(B, 2 * K1, N2, C), u4, gate.reshape(B, N1 // 2, N2, C),
                      skip[order].reshape(1, C))
        return out.reshape(B, L, C)

    z = long_conv(v, 0, x1)
    return long_conv(z, 1, x2)


def _gla_chunk(q, k, v, lr3, up3, upb, st, reverse):
    C = GLA_CHUNK
    rr = _iota2((C, C), 0)
    cc = _iota2((C, C), 1)
    keep = (rr <= cc) if reverse else (rr >= cc)
    g = _log_sigmoid(_dot(lr3, up3) + upb) / GLA_TAU
    yield None
    bc = _mask_dot(keep.astype(BF16), g)
    yield None
    blast = bc[0:1] if reverse else bc[C - 1:C]
    q_in = (q * (GLA_DK ** -0.5) * jnp.exp(bc)).astype(BF16)
    k_in = (k * jnp.exp(-bc)).astype(BF16)
    k_end = (k * jnp.exp(blast - bc)).astype(BF16)
    vb = v.astype(BF16)
    att = _dot_nt(q_in, k_in)
    o_inter = _dot_nt(q_in, st.astype(BF16))
    st_new = st * jnp.exp(blast) + _dot_tn(vb, k_end)
    yield None
    o = _dot(jnp.where(keep, att, 0.0).astype(BF16), vb) + o_inter
    yield o, st_new


def _gla_body(qf_ref, kf_ref, vf_ref, lf_ref, qb_ref, kb_ref, vb_ref, lb_ref, up_ref, upb_ref,
              of_ref, ob_ref, st_ref, *, n_chunks):
    C = GLA_CHUNK

    @pl.when(pl.program_id(1) == 0)
    def _():
        st_ref[...] = jnp.zeros_like(st_ref)

    def chunk(step, carry):
        dirs = ((0, step, False, (qf_ref, kf_ref, vf_ref, lf_ref), of_ref),
                (1, n_chunks - 1 - step, True, (qb_ref, kb_ref, vb_ref, lb_ref), ob_ref))
        work = []
        for d, c, reverse, (q_ref, k_ref, v_ref, l_ref), o_ref in dirs:
            rows = pl.ds(pl.multiple_of(c * C, C), C)
            lr_hi, lr_lo = _split(l_ref[0, rows, :])
            lr3 = jnp.concatenate([lr_hi, lr_lo, lr_hi], axis=1)
            for h in range(GLA_HEADS):
                kl = slice(h * GLA_DK, (h + 1) * GLA_DK)
                vl = slice(h * GLA_DV, (h + 1) * GLA_DV)
                vals = (q_ref[0, rows, kl], k_ref[0, rows, kl], v_ref[0, rows, vl], lr3,
                        up_ref[d, :, kl], upb_ref[d, :, kl], st_ref[d, h])
                work.append((vals, reverse, o_ref, rows, vl, d, h))
        done = _interleave([_gla_chunk(*vals, reverse) for vals, reverse, *_ in work])
        for (o, st_new), (_, _, o_ref, rows, vl, d, h) in zip(done, work):
            o_ref[0, rows, vl] = o
            st_ref[d, h] = st_new
        return carry

    lax.fori_loop(0, n_chunks, chunk, 0)


def gla_scan(p, plr, up_pad, up_b, col0, tl=256):
    B, L, _ = p.shape
    H = GLA_HEADS
    nt = L // tl
    kw, vw = H * GLA_DK, H * GLA_DV
    qb = col0 * LANES // kw
    vb = (col0 * LANES + 2 * kw) // vw
    assert col0 * LANES % kw == 0 and (col0 * LANES + 2 * kw) % vw == 0
    specs = []
    for tmap in (lambda t: t, lambda t: nt - 1 - t):
        specs += [pl.BlockSpec((1, tl, kw), lambda b, t, tmap=tmap: (b, tmap(t), qb)),
                  pl.BlockSpec((1, tl, kw), lambda b, t, tmap=tmap: (b, tmap(t), qb + 1)),
                  pl.BlockSpec((1, tl, vw), lambda b, t, tmap=tmap: (b, tmap(t), vb)),
                  pl.BlockSpec((1, tl, LANES), lambda b, t, tmap=tmap: (b, tmap(t), 0))]
    specs += [pl.BlockSpec((2, 3 * LANES, kw), lambda b, t: (0, 0, 0)),
              pl.BlockSpec((2, 1, kw), lambda b, t: (0, 0, 0))]
    out = jax.ShapeDtypeStruct((B, L, vw), F32)
    return pl.pallas_call(
        functools.partial(_gla_body, n_chunks=tl // GLA_CHUNK),
        grid=(B, nt),
        in_specs=specs,
        out_specs=[pl.BlockSpec((1, tl, vw), lambda b, t: (b, t, 0)),
                   pl.BlockSpec((1, tl, vw), lambda b, t: (b, nt - 1 - t, 0))],
        out_shape=[out, out],
        scratch_shapes=[pltpu.VMEM((2, H, GLA_DV, GLA_DK), F32)],
        compiler_params=_cparams("parallel", "arbitrary"),
        name="gla_scan",
    )(p, p, p, plr, p, p, p, plr, up_pad, up_b)


def _gla_post_body(of_ref, ob_ref, r_ref, gn_ref, o_ref):
    for h in range(of_ref.shape[2] // GLA_DV):
        sl = slice(h * GLA_DV, (h + 1) * GLA_DV)
        o = of_ref[0, :, sl] + ob_ref[0, :, sl]
        o = o * lax.rsqrt(jnp.mean(o * o, axis=-1, keepdims=True) + NORM_EPS) * gn_ref[:, sl]
        r = r_ref[0, :, sl]
        o_ref[0, :, sl] = (o * (r * jax.nn.sigmoid(r))).astype(o_ref.dtype)


def gla_post(o_f, o_b, p, r_block, norm_g, tl=512):
    B, L, W = o_f.shape
    spec = pl.BlockSpec((1, tl, W), lambda b, t: (b, t, 0))
    return pl.pallas_call(
        _gla_post_body,
        grid=(B, L // tl),
        in_specs=[spec, spec,
                  pl.BlockSpec((1, tl, W), lambda b, t: (b, t, r_block)),
                  pl.BlockSpec((1, W), lambda b, t: (0, 0))],
        out_specs=spec,
        out_shape=jax.ShapeDtypeStruct((B, L, W), BF16),
        compiler_params=_cparams("parallel", "parallel"),
        name="gla_post",
    )(o_f, o_b, p, norm_g.reshape(1, W))


def gla_mixer(p, plr, col0, up, up_b, norm_g):
    H = GLA_HEADS
    up_pad = jnp.zeros((2, LANES, H * GLA_DK), F32)
    up_pad = up_pad.at[0, 0:GLA_RANK].set(up[0]).at[1, GLA_RANK:2 * GLA_RANK].set(up[1])
    up_hi = up_pad.astype(BF16)
    up_lo = (up_pad - up_hi.astype(F32)).astype(BF16)
    up3 = jnp.concatenate([up_hi, up_hi, up_lo], axis=1)
    o_f, o_b = gla_scan(p, plr, up3, up_b.reshape(2, 1, H * GLA_DK), col0)
    r_col = col0 * LANES + 2 * H * GLA_DK + H * GLA_DV
    assert r_col % (H * GLA_DV) == 0
    return gla_post(o_f, o_b, p, r_col // (H * GLA_DV), norm_g)


def _rw_mix_body(x_ref, xm_ref, xp_ref, g_ref, sc_ref, sh_ref, mu_ref, *o_refs, n_tiles):
    i = pl.program_id(1)
    g, sc, sh = g_ref[...], sc_ref[0], sh_ref[0]
    h = _norm_mod_value(x_ref[0], g, sc, sh)
    hm = _norm_mod_value(xm_ref[0], g, sc, sh)
    hp = _norm_mod_value(xp_ref[0], g, sc, sh)
    hm1, hp1 = _neighbours(h, hm, hp, i, n_tiles)
    xx = 0.5 * (hm1 + hp1) - h
    mu = mu_ref[...]
    for j, o_ref in enumerate(o_refs):
        o_ref[0] = (h + xx * mu[j:j + 1]).astype(o_ref.dtype)


def rw_mix(x, g, scale, shift, mu, tl=256):
    B, L, D = x.shape
    n_tiles = L // tl
    out = jax.ShapeDtypeStruct((B, L, D), BF16)
    ospec = pl.BlockSpec((1, tl, D), lambda b, i: (b, i, 0))
    mu8 = jnp.zeros((8, D), F32).at[:6].set(mu)
    return pl.pallas_call(
        functools.partial(_rw_mix_body, n_tiles=n_tiles),
        grid=(B, n_tiles),
        in_specs=_neighbour_specs(tl, D, L // 8) + [pl.BlockSpec((1, D), lambda b, i: (0, 0)),
                                                    pl.BlockSpec((1, 1, D), lambda b, i: (b, 0, 0)),
                                                    pl.BlockSpec((1, 1, D), lambda b, i: (b, 0, 0)),
                                                    pl.BlockSpec((8, D), lambda b, i: (0, 0))],
        out_specs=[ospec] * 6,
        out_shape=[out] * 6,
        compiler_params=_cparams("parallel", "parallel"),
        name="rw_mix",
    )(x, x, x, g.reshape(1, D), scale, shift, mu8)


def _head_ones():
    r = _iota2((LANES, LANES), 0) // RW_HEAD
    c = _iota2((LANES, LANES), 1) // RW_HEAD
    return (r == c).astype(BF16)


def _rw_prep_body(k_ref, r_ref, v_ref, a0_ref, a1_ref, kk_w_ref, ka_ref, rk_ref, kk_ref, bon_ref):
    ones = _head_ones()
    for j in range(k_ref.shape[1] // LANES):
        sl = slice(j * LANES, (j + 1) * LANES)
        k = k_ref[:, sl]
        kk = k * kk_w_ref[:, sl]
        ss = _dot_mask(kk * kk, ones)
        kk_ref[:, sl] = kk * lax.rsqrt(jnp.maximum(ss, 1e-24))
        a_sum = jax.nn.sigmoid(a0_ref[:, sl]) + jax.nn.sigmoid(a1_ref[:, sl])
        kd_sum = k * (2.0 + (a_sum - 2.0) * ka_ref[:, sl])
        bon_ref[:, sl] = _dot_mask(r_ref[:, sl] * kd_sum * rk_ref[:, sl], ones) * v_ref[:, sl]


def rw_prep(k, r, v, apre0, apre1, k_k, k_a, r_k, tm=256):
    T, D = k.shape
    tm = min(tm, T)
    spec = pl.BlockSpec((tm, D), lambda i: (i, 0))
    pspec = pl.BlockSpec((1, D), lambda i: (0, 0))
    out = jax.ShapeDtypeStruct((T, D), F32)
    return pl.pallas_call(
        _rw_prep_body,
        grid=(T // tm,),
        in_specs=[spec, spec, spec, spec, spec, pspec, pspec, pspec],
        out_specs=[spec, spec],
        out_shape=[out, out],
        compiler_params=_cparams("parallel"),
        name="rw_prep",
    )(k, r, v, apre0, apre1, k_k.reshape(1, D), k_a.reshape(1, D), r_k.reshape(1, D))


def _rw_chunk(r, k, v, kk, wpre, apre, ka, h, reverse):
    C = RW_CHUNK
    P = 2 * C
    rc = _iota2((C, C), 0)
    cc = _iota2((C, C), 1)
    tri = ((rc <= cc) if reverse else (rc >= cc)).astype(BF16)
    t_idx = _iota2((C, P), 0)
    s_idx = _iota2((C, P), 1) % C
    strict = (t_idx < s_idx) if reverse else (t_idx > s_idx)
    incl = (t_idx <= s_idx) if reverse else (t_idx >= s_idx)
    head0 = _iota2((C, LANES), 1) < RW_HEAD
    same_head = (_iota2((LANES, LANES), 0) // RW_HEAD) == (_iota2((LANES, LANES), 1) // RW_HEAD)

    def stack(x):
        return jnp.concatenate([jnp.where(head0, x, 0.0), jnp.where(head0, 0.0, x)], axis=0).astype(BF16)

    lw = -RW_LOG_DECAY_MAX * jax.nn.sigmoid(wpre)
    a = jax.nn.sigmoid(apre)
    kd = k * (1.0 + (a - 1.0) * ka)
    b = kk * a
    cum = _mask_dot(tri, lw)
    yield None
    cum_x = cum - lw
    total = cum[0:1] if reverse else cum[C - 1:C]
    e_neg = jnp.exp(-cum)
    rn = (r * jnp.exp(cum)).astype(BF16)
    xn = (kk * jnp.exp(cum_x)).astype(BF16)
    kn = kd * e_neg
    bn = b * e_neg
    vs = stack(v)
    coef = _dot_nt(jnp.concatenate([xn, rn], axis=0), jnp.concatenate([stack(kn), stack(bn)], axis=0))
    yield None
    a_k = jnp.where(strict, coef[:C, :P], 0.0).astype(BF16)
    pw = jnp.where(strict, -coef[:C, P:], 0.0)
    q_k = jnp.where(incl, coef[C:, :P], 0.0).astype(BF16)
    q_b = jnp.where(incl, coef[C:, P:], 0.0).astype(BF16)
    hv = jnp.concatenate([h.astype(BF16), vs], axis=0)
    x = _dot(jnp.concatenate([xn, a_k], axis=1), hv)
    o_hv = _dot(jnp.concatenate([rn, q_k], axis=1), hv)
    yield None
    for _ in range(int(math.log2(C)) - 1):
        both = _dot(pw.astype(BF16), jnp.concatenate([stack(x), stack(pw)], axis=1))
        x = x + both[:, :P]
        pw = both[:, P:]
        yield None
    u = x + _dot(pw.astype(BF16), stack(x))
    yield None
    o = o_hv - _dot(q_b, stack(u))
    upd = _dot_tn(jnp.concatenate([kn, bn], axis=0).astype(BF16),
                  jnp.concatenate([v, -u], axis=0).astype(BF16))
    gcol = jnp.transpose(jnp.broadcast_to(jnp.exp(total), (LANES, LANES)))
    h_new = gcol * (h + jnp.where(same_head, upd, 0.0))
    yield o, h_new


def _interleave(chains):
    while True:
        outs = [next(ch) for ch in chains]
        if outs[0] is not None:
            return outs


def _rw_scan_body(rf_ref, kf_ref, vf_ref, kkf_ref, wf_ref, af_ref, rb_ref, kb_ref, vb_ref, kkb_ref, wb_ref,
                  ab_ref, ka_ref, of_ref, ob_ref, h_ref, *, n_chunks, n_pairs):
    C = RW_CHUNK

    @pl.when(pl.program_id(2) == 0)
    def _():
        h_ref[...] = jnp.zeros_like(h_ref)

    def chunk(step, carry):
        dirs = ((0, step, False, (rf_ref, kf_ref, vf_ref, kkf_ref, wf_ref, af_ref), of_ref),
                (1, n_chunks - 1 - step, True, (rb_ref, kb_ref, vb_ref, kkb_ref, wb_ref, ab_ref), ob_ref))
        work = []
        for d, c, reverse, refs, o_ref in dirs:
            rows = pl.ds(pl.multiple_of(c * C, C), C)
            for g in range(n_pairs):
                lanes = slice(g * LANES, (g + 1) * LANES)
                vals = [ref[0, rows, lanes] for ref in refs]
                work.append((vals, ka_ref[:, lanes], h_ref[d, g], reverse, o_ref, rows, lanes, d, g))
        done = _interleave([_rw_chunk(*vals, ka, h, reverse) for vals, ka, h, reverse, *_ in work])
        for (o, h_new), (_, _, _, _, o_ref, rows, lanes, d, g) in zip(done, work):
            o_ref[0, rows, lanes] = o
            h_ref[d, g] = h_new
        return carry

    lax.fori_loop(0, n_chunks, chunk, 0)


def rw_scan(r, k, v, kk, wpre, apre, k_a, tl=256, n_pairs=8):
    B, L, D = r.shape
    width = n_pairs * LANES
    nt = L // tl
    fspec = pl.BlockSpec((1, tl, width), lambda b, p, t: (b, t, p))
    bspec = pl.BlockSpec((1, tl, width), lambda b, p, t: (b, nt - 1 - t, p))
    out = jax.ShapeDtypeStruct((B, L, D), F32)
    return pl.pallas_call(
        functools.partial(_rw_scan_body, n_chunks=tl // RW_CHUNK, n_pairs=n_pairs),
        grid=(B, D // width, nt),
        in_specs=[fspec] * 6 + [bspec] * 6 + [pl.BlockSpec((1, width), lambda b, p, t: (0, p))],
        out_specs=[fspec, bspec],
        out_shape=[out, out],
        scratch_shapes=[pltpu.VMEM((2, n_pairs, LANES, LANES), F32)],
        compiler_params=_cparams("parallel", "parallel", "arbitrary"),
        name="rw_scan",
    )(r, k, v, kk, wpre[0], apre[0], r, k, v, kk, wpre[1], apre[1], k_a.reshape(1, D))


def _rw_post_body(of_ref, ob_ref, bon_ref, g_ref, gng_ref, gnb_ref, o_ref):
    ones = _head_ones()
    for j in range(of_ref.shape[1] // LANES):
        sl = slice(j * LANES, (j + 1) * LANES)
        so = of_ref[:, sl] + ob_ref[:, sl]
        mean = _dot_mask(so, ones) * (1.0 / RW_HEAD)
        dlt = so - mean
        var = _dot_mask(dlt * dlt, ones) * (1.0 / RW_HEAD)
        y = dlt * lax.rsqrt(var + RW_GN_EPS) * gng_ref[:, sl] + gnb_ref[:, sl]
        o_ref[:, sl] = ((y + bon_ref[:, sl]) * g_ref[:, sl]).astype(o_ref.dtype)


def rw_post(o_f, o_b, bonus, g, gn_g, gn_b, tm=256):
    T, D = o_f.shape
    tm = min(tm, T)
    spec = pl.BlockSpec((tm, D), lambda i: (i, 0))
    pspec = pl.BlockSpec((1, D), lambda i: (0, 0))
    return pl.pallas_call(
        _rw_post_body,
        grid=(T // tm,),
        in_specs=[spec, spec, spec, spec, pspec, pspec],
        out_specs=spec,
        out_shape=jax.ShapeDtypeStruct((T, D), BF16),
        compiler_params=_cparams("parallel"),
        name="rw_post",
    )(o_f, o_b, bonus, g, gn_g.reshape(1, D), gn_b.reshape(1, D))


def _pad_cols(w, n):
    return jnp.pad(w, ((0, 0), (0, n - w.shape[1])))


def _pad_rows(w, n):
    return jnp.pad(w, ((0, n - w.shape[0]), (0, 0)))


def rwkv7_block(x, gate, g_norm, scale, shift, mu, w_r, w_k, w_v, w0, w1, w2, a0, a1, a2, g1, g2,
                k_k, k_a, r_k, gn_g, gn_b, w_o):
    B, L, D = x.shape
    T = B * L
    xr, xw, xk, xv, xa, xg = [a.reshape(T, D) for a in rw_mix(x, g_norm, scale, shift, mu)]
    r = matmul(xr, w_r)
    k = matmul(xk, w_k)
    v = matmul(xv, w_v)
    w1c = jnp.concatenate([_pad_cols(w1[0], LANES), _pad_cols(w1[1], LANES)], axis=1).astype(BF16)
    a1c = jnp.concatenate([_pad_cols(a1[0], LANES), _pad_cols(a1[1], LANES)], axis=1).astype(BF16)
    wmid = matmul(xw, w1c)
    amid = matmul(xa, a1c)
    gmid = matmul(xg, g1.astype(BF16))
    wpre = [matmul(wmid, _pad_rows(w2[d], LANES).astype(BF16), a_act="tanh", bias=w0[d], a_col_block=d)
            for d in range(2)]
    apre = [matmul(amid, _pad_rows(a2[d], LANES).astype(BF16), bias=a0[d], a_col_block=d)
            for d in range(2)]
    g = matmul(gmid, g2.astype(BF16), a_act="sigmoid")
    kk, bonus = rw_prep(k, r, v, apre[0], apre[1], k_k, k_a, r_k.reshape(D))
    sh = lambda a: a.reshape(B, L, D)
    o_f, o_b = rw_scan(sh(r), sh(k), sh(v), sh(kk), [sh(a) for a in wpre], [sh(a) for a in apre], k_a)
    y = rw_post(o_f.reshape(T, D), o_b.reshape(T, D), bonus, g, gn_g, gn_b)
    out = matmul(y, w_o, res=x.reshape(T, D), gate=gate, rows_per_batch=L)
    return out.reshape(B, L, D)


def ffn_block(x, c, norm_g, ada_w, ada_b, idx, w1, w2):
    B, L, D = x.shape
    T = B * L
    shift, scale, gate = ada_mod(c, ada_w, ada_b, idx)
    h = norm_mod(x, norm_g, scale, shift).reshape(T, D)
    hid = matmul(h, w1, w_idx=idx, out_dtype=BF16, epi="sqrelu")
    out = matmul(hid, w2.astype(BF16), w_idx=idx, res=x.reshape(T, D), gate=gate, rows_per_batch=L,
                 tm=512, tn=512)
    return out.reshape(B, L, D)


def hyena_gla_block(x, gate, h, w_in, w_out, conv_w, conv_b, f_w1, f_b1, f_freq1, f_w2, f_b2, f_freq2,
                    f_w3, f_decay, hy_skip, gla_up, gla_up_b, gla_norm_g):
    B, L, D = x.shape
    T = B * L
    C = conv_w.shape[1] // 3
    n_main = w_in.shape[1] - 2 * GLA_RANK
    p = matmul(h, w_in, n_cols=n_main).reshape(B, L, n_main)
    plr = matmul(h, _pad_cols(w_in[:, n_main:], LANES).astype(BF16)).reshape(B, L, LANES)
    y_hy = hyena_mixer(p, C, conv_w, conv_b, f_w1, f_b1, f_freq1, f_w2, f_b2, f_freq2, f_w3, f_decay, hy_skip)
    y_gla = gla_mixer(p, plr, 3 * C // LANES, gla_up, gla_up_b, gla_norm_g)
    out = matmul(y_hy.reshape(T, -1), w_out, a2=y_gla.reshape(T, -1), res=x.reshape(T, D),
                 gate=gate, rows_per_batch=L, tn=512)
    return out.reshape(B, L, D)


def kernel(x, c, mix_norm_g, mix_ada_w, mix_ada_b, ab_w_in, ab_w_out, hy_conv_w, hy_conv_b, hy_ffn_w1, hy_ffn_b1, hy_freq1, hy_ffn_w2, hy_ffn_b2, hy_freq2, hy_ffn_w3, hy_decay, hy_skip, gla_up, gla_up_b, gla_norm_g, tm_norm_g, tm_ada_w, tm_ada_b, rw_mu, rw_w_r, rw_w_k, rw_w_v, rw_w0, rw_w1, rw_w2, rw_a0, rw_a1, rw_a2, rw_g1, rw_g2, rw_k_k, rw_k_a, rw_r_k, rw_gn_g, rw_gn_b, rw_w_o, ffn_norm_g, ffn_ada_w, ffn_ada_b, ffn_w1, ffn_w2, final_norm_g):
    B, L, D = x.shape
    depth = ffn_w1.shape[0]
    for layer in range(depth):
        i = layer // 2
        if layer % 2 == 0:
            shift, scale, gate = ada_mod(c, mix_ada_w, mix_ada_b, i)
            h = norm_mod(x, mix_norm_g[i], scale, shift).reshape(B * L, D)
            x = hyena_gla_block(x, gate, h, ab_w_in[i], ab_w_out[i], hy_conv_w[i], hy_conv_b[i],
                                hy_ffn_w1[i], hy_ffn_b1[i], hy_freq1[i], hy_ffn_w2[i], hy_ffn_b2[i],
                                hy_freq2[i], hy_ffn_w3[i], hy_decay[i], hy_skip[i],
                                gla_up[i], gla_up_b[i], gla_norm_g[i])
        else:
            shift, scale, gate = ada_mod(c, tm_ada_w, tm_ada_b, i)
            x = rwkv7_block(x, gate, tm_norm_g[i], scale, shift, rw_mu[i], rw_w_r[i], rw_w_k[i], rw_w_v[i],
                            rw_w0[i], rw_w1[i], rw_w2[i], rw_a0[i], rw_a1[i], rw_a2[i], rw_g1[i], rw_g2[i],
                            rw_k_k[i], rw_k_a[i], rw_r_k[i], rw_gn_g[i], rw_gn_b[i], rw_w_o[i])
        x = ffn_block(x, c, ffn_norm_g[layer], ffn_ada_w, ffn_ada_b, layer, ffn_w1, ffn_w2)
    return rms_final(x, final_norm_g)
```

```python
import functools
import math

import numpy as np
import jax
import jax.numpy as jnp
from jax import lax
from jax.experimental import pallas as pl
from jax.experimental.pallas import tpu as pltpu

F32 = jnp.float32
BF16 = jnp.bfloat16
HI = lax.Precision.HIGHEST

LANES = 128
SUBLANES = 8
NORM_EPS = 1e-6
GLA_HEADS = 4
GLA_DK = 128
GLA_DV = 256
GLA_RANK = 16
GLA_TAU = 16.0
GLA_CHUNK = 64
RW_HEAD = 64
RW_CHUNK = 64
RW_LOG_DECAY_MAX = 0.606531
RW_GN_EPS = 64e-5
HY_ORDER = 2
HY_EMB = 33
HY_FFN = 64
DFT_N2 = 128
VMEM_LIMIT = 52 * 1024 * 1024


def _cparams(*sem):
    return pltpu.CompilerParams(dimension_semantics=sem, vmem_limit_bytes=VMEM_LIMIT)


def _dot(a, b, precision=None):
    return jnp.dot(a, b, preferred_element_type=F32, precision=precision)


def _dot_nt(a, b, precision=None):
    return lax.dot_general(a, b, (((1,), (1,)), ((), ())), preferred_element_type=F32, precision=precision)


def _dot_tn(a, b, precision=None):
    return lax.dot_general(a, b, (((0,), (0,)), ((), ())), preferred_element_type=F32, precision=precision)


def _iota2(shape, axis):
    return lax.broadcasted_iota(jnp.int32, shape, axis)


def _log_sigmoid(x):
    return jnp.minimum(x, 0.0) - jnp.log(1.0 + jnp.exp(-jnp.abs(x)))


def _split(x):
    hi = x.astype(BF16)
    return hi, (x - hi.astype(F32)).astype(BF16)


def _mask_dot(mask_bf16, x):
    hi, lo = _split(x)
    n = x.shape[1]
    y = _dot(mask_bf16, jnp.concatenate([hi, lo], axis=1))
    return y[:, :n] + y[:, n:]


def _stack3(a):
    a = np.asarray(a, np.float32)
    hi = a.astype(BF16)
    lo = (a - hi.astype(np.float32)).astype(BF16)
    return jnp.asarray(np.concatenate([hi, hi, lo], axis=1))


def _dot3(a3, x):
    hi, lo = _split(x)
    return _dot(a3, jnp.concatenate([hi, lo, hi], axis=0))


def _split_bits(w):
    bits = lax.bitcast_convert_type(w, jnp.uint32) & jnp.uint32(0xFFFF0000)
    hi = lax.bitcast_convert_type(bits, F32)
    return hi.astype(BF16), (w - hi).astype(BF16)


def _stack3_rows(w):
    hi, lo = _split_bits(w)
    return jnp.concatenate([hi, hi, lo], axis=0)


def _dot3r(x, w3):
    hi, lo = _split(x)
    return _dot(jnp.concatenate([hi, lo, hi], axis=1), w3)


def _dot_mask(x, mask_bf16):
    hi, lo = _split(x)
    return _dot(hi, mask_bf16) + _dot(lo, mask_bf16)


def _ada_body(c_ref, w_ref, b_ref, o_ref):
    c = c_ref[...]
    s = c * jax.nn.sigmoid(c)
    o_ref[...] = _dot(s, w_ref[0], HI) + b_ref[0]


def ada_mod(c, w, b, idx):
    B, D = c.shape
    n3 = w.shape[2]
    tn = 512
    out = pl.pallas_call(
        _ada_body,
        grid=(n3 // tn,),
        in_specs=[pl.BlockSpec((B, D), lambda j: (0, 0)),
                  pl.BlockSpec((1, D, tn), lambda j: (idx, 0, j)),
                  pl.BlockSpec((1, 1, tn), lambda j: (idx, 0, j))],
        out_specs=pl.BlockSpec((B, tn), lambda j: (0, j)),
        out_shape=jax.ShapeDtypeStruct((B, n3), F32),
        compiler_params=_cparams("parallel"),
        name="ada_mod",
    )(c, w, b.reshape(b.shape[0], 1, n3))
    shift, scale, gate = jnp.split(out[:, None, :], 3, axis=-1)
    return shift, scale, gate


def _norm_mod_value(x, g, scale, shift):
    y = x * lax.rsqrt(jnp.mean(x * x, axis=-1, keepdims=True) + NORM_EPS)
    return (y * g) * (1.0 + scale) + shift


def _norm_mod_body(x_ref, g_ref, sc_ref, sh_ref, o_ref):
    o_ref[0] = _norm_mod_value(x_ref[0], g_ref[...], sc_ref[0], sh_ref[0]).astype(o_ref.dtype)


def norm_mod(x, g, scale, shift, out_dtype=BF16, tm=512):
    B, L, D = x.shape
    return pl.pallas_call(
        _norm_mod_body,
        grid=(B, L // tm),
        in_specs=[pl.BlockSpec((1, tm, D), lambda b, i: (b, i, 0)),
                  pl.BlockSpec((1, D), lambda b, i: (0, 0)),
                  pl.BlockSpec((1, 1, D), lambda b, i: (b, 0, 0)),
                  pl.BlockSpec((1, 1, D), lambda b, i: (b, 0, 0))],
        out_specs=pl.BlockSpec((1, tm, D), lambda b, i: (b, i, 0)),
        out_shape=jax.ShapeDtypeStruct((B, L, D), out_dtype),
        compiler_params=_cparams("parallel", "parallel"),
        name="norm_mod",
    )(x, g.reshape(1, D), scale, shift)


def _rms_body(x_ref, g_ref, o_ref):
    x = x_ref[0]
    y = x * lax.rsqrt(jnp.mean(x * x, axis=-1, keepdims=True) + NORM_EPS)
    o_ref[0] = y * g_ref[...]


def rms_final(x, g, tm=512):
    B, L, D = x.shape
    return pl.pallas_call(
        _rms_body,
        grid=(B, L // tm),
        in_specs=[pl.BlockSpec((1, tm, D), lambda b, i: (b, i, 0)),
                  pl.BlockSpec((1, D), lambda b, i: (0, 0))],
        out_specs=pl.BlockSpec((1, tm, D), lambda b, i: (b, i, 0)),
        out_shape=jax.ShapeDtypeStruct((B, L, D), F32),
        compiler_params=_cparams("parallel", "parallel"),
        name="rms_final",
    )(x, g.reshape(1, D))


def _mm_body(*refs, nk, a_act, epi, has_a2, has_bias, has_res):
    it = iter(refs)
    a_ref = next(it)
    w_ref = next(it)
    a2_ref = next(it) if has_a2 else None
    w2_ref = next(it) if has_a2 else None
    bias_ref = next(it) if has_bias else None
    res_ref = next(it) if has_res else None
    gate_ref = next(it) if has_res else None
    o_ref = next(it)
    acc_ref = next(it) if nk > 1 else None

    a = a_ref[...]
    if a_act == "tanh":
        a = jnp.tanh(a.astype(F32))
    elif a_act == "sigmoid":
        a = jax.nn.sigmoid(a.astype(F32))
    part = _dot(a.astype(BF16), w_ref[0])
    if has_a2:
        part = part + _dot(a2_ref[...].astype(BF16), w2_ref[0])

    def finish(acc):
        if has_bias:
            acc = acc + bias_ref[...]
        if epi == "sqrelu":
            acc = jnp.square(jnp.maximum(acc, 0.0))
        if has_res:
            acc = res_ref[...] + gate_ref[0] * acc
        o_ref[...] = acc.astype(o_ref.dtype)

    if nk == 1:
        finish(part)
    else:
        k = pl.program_id(2)

        @pl.when(k == 0)
        def _():
            acc_ref[...] = part

        @pl.when(k > 0)
        def _():
            acc_ref[...] += part

        @pl.when(k == nk - 1)
        def _():
            finish(acc_ref[...])


def matmul(a, w, *, a2=None, w_idx=0, n_cols=None, out_dtype=F32, a_act=None, epi=None, bias=None, res=None,
           gate=None, rows_per_batch=None, tm=1024, tn=1024, tk=None, a_col_block=0):
    M = a.shape[0]
    if w.ndim == 2:
        w = w[None]
    K = w.shape[1]
    N = w.shape[2] if n_cols is None else n_cols
    has_a2 = a2 is not None
    ka = K // 2 if has_a2 else K
    tm = min(tm, M if rows_per_batch is None else rows_per_batch)
    tn = min(tn, N)
    tk = ka if tk is None else min(tk, ka)
    nk = ka // tk
    assert M % tm == 0 and N % tn == 0 and ka % tk == 0 and not (has_a2 and nk > 1)
    has_bias = bias is not None
    has_res = res is not None
    in_specs = [pl.BlockSpec((tm, tk), lambda i, j, k: (i, k + a_col_block * nk)),
                pl.BlockSpec((1, tk, tn), lambda i, j, k: (w_idx, k, j))]
    args = [a, w]
    if has_a2:
        in_specs += [pl.BlockSpec((tm, tk), lambda i, j, k: (i, 0)),
                     pl.BlockSpec((1, tk, tn), lambda i, j, k: (w_idx, 1, j))]
        args += [a2, w]
    if has_bias:
        in_specs.append(pl.BlockSpec((1, tn), lambda i, j, k: (0, j)))
        args.append(bias.reshape(1, N).astype(F32))
    if has_res:
        tiles_per_batch = rows_per_batch // tm
        assert rows_per_batch % tm == 0
        in_specs.append(pl.BlockSpec((tm, tn), lambda i, j, k: (i, j)))
        in_specs.append(pl.BlockSpec((1, 1, tn), lambda i, j, k: (i // tiles_per_batch, 0, j)))
        args += [res, gate]
    scratch = [pltpu.VMEM((tm, tn), F32)] if nk > 1 else []
    body = functools.partial(_mm_body, nk=nk, a_act=a_act, epi=epi, has_a2=has_a2, has_bias=has_bias,
                             has_res=has_res)
    return pl.pallas_call(
        body,
        grid=(M // tm, N // tn, nk),
        in_specs=in_specs,
        out_specs=pl.BlockSpec((tm, tn), lambda i, j, k: (i, j)),
        out_shape=jax.ShapeDtypeStruct((M, N), out_dtype),
        scratch_shapes=scratch,
        compiler_params=_cparams("parallel", "parallel", "arbitrary"),
        name="matmul",
    )(*args)


def _neighbours(cur, prev8, next8, i, n_tiles):
    t = cur.shape[0]
    row = _iota2(cur.shape, 0)
    before = jnp.where(i > 0, prev8[7:8, :], 0.0)
    after = jnp.where(i < n_tiles - 1, next8[0:1, :], 0.0)
    xm1 = jnp.where(row == 0, before, pltpu.roll(cur, 1, 0))
    xp1 = jnp.where(row == t - 1, after, pltpu.roll(cur, t - 1, 0))
    return xm1, xp1


def _neighbour_specs(tl, width, n_rows8, col_map=None):
    r8 = tl // 8
    return [pl.BlockSpec((1, tl, width), lambda b, i: (b, i, 0)),
            pl.BlockSpec((1, 8, width), lambda b, i: (b, jnp.maximum(i * r8 - 1, 0), 0)),
            pl.BlockSpec((1, 8, width), lambda b, i: (b, jnp.minimum((i + 1) * r8, n_rows8 - 1), 0))]


def _hy_conv_body(p_ref, pm_ref, pp_ref, w_ref, b_ref, v_ref, x1_ref, x2_ref, *, n_tiles, C):
    i = pl.program_id(1)
    cur = p_ref[0]
    xm1, xp1 = _neighbours(cur, pm_ref[0], pp_ref[0], i, n_tiles)
    w = w_ref[...]
    u = xm1 * w[0:1] + cur * w[1:2] + xp1 * w[2:3] + b_ref[...]
    v_ref[0] = u[:, 0:C]
    x1_ref[0] = u[:, C:2 * C]
    x2_ref[0] = u[:, 2 * C:3 * C]


def hy_short_conv(p, conv_w, conv_b, C, tl=256):
    B, L, _ = p.shape
    W = 3 * C
    n_tiles = L // tl
    out = jax.ShapeDtypeStruct((B, L, C), F32)
    ospec = pl.BlockSpec((1, tl, C), lambda b, i: (b, i, 0))
    return pl.pallas_call(
        functools.partial(_hy_conv_body, n_tiles=n_tiles, C=C),
        grid=(B, n_tiles),
        in_specs=_neighbour_specs(tl, W, L // 8) + [pl.BlockSpec((3, W), lambda b, i: (0, 0)),
                                                    pl.BlockSpec((1, W), lambda b, i: (0, 0))],
        out_specs=[ospec, ospec, ospec],
        out_shape=[out, out, out],
        compiler_params=_cparams("parallel", "parallel"),
        name="hy_short_conv",
    )(p, p, p, conv_w, conv_b.reshape(1, W))


def _hy_filter_body(z_ref, w1_ref, b1_ref, f1_ref, w2_ref, b2_ref, f2_ref, w3_ref, dec_ref, o_ref, *, L, tm):
    i = pl.program_id(0)
    z = z_ref[...]
    f = jnp.sin(f1_ref[...] * (_dot3r(z, w1_ref[...]) + b1_ref[...]))
    f = jnp.sin(f2_ref[...] * (_dot3r(f, w2_ref[...]) + b2_ref[...]))
    t = z[:, 0:1]
    f = _dot3r(f, w3_ref[...]) * jnp.exp(-t * jnp.abs(dec_ref[...]))
    n = i * tm + _iota2(f.shape, 0)
    o_ref[...] = jnp.where(n == L, 0.0, f)


def hy_filter(L, C, w1, b1, freq1, w2, b2, freq2, w3, decay, tm=512):
    pos = np.concatenate([np.arange(L), [0], np.arange(L - 1, 0, -1)]).astype(np.float64)
    t = pos / (L - 1)
    ang = 2.0 * math.pi * pos / L
    nb = (HY_EMB - 1) // 2
    ba = np.linspace(1e-4, nb - 1, nb)[None, :] * ang[:, None]
    z = np.concatenate([t[:, None], np.cos(ba), -np.sin(ba)], axis=-1)
    zp = np.zeros((2 * L, 2 * HY_FFN), np.float32)
    zp[:, :HY_EMB] = z
    w1p = jnp.zeros((2 * HY_FFN, HY_FFN), F32).at[:HY_EMB].set(w1)
    nL = L // tm
    row = lambda a: a.reshape(1, -1)
    return pl.pallas_call(
        functools.partial(_hy_filter_body, L=L, tm=tm),
        grid=(2 * L // tm, HY_ORDER),
        in_specs=[pl.BlockSpec((tm, 2 * HY_FFN), lambda i, o: (i, 0)),
                  pl.BlockSpec((6 * HY_FFN, HY_FFN), lambda i, o: (0, 0)),
                  pl.BlockSpec((1, HY_FFN), lambda i, o: (0, 0)),
                  pl.BlockSpec((1, HY_FFN), lambda i, o: (0, 0)),
                  pl.BlockSpec((3 * HY_FFN, HY_FFN), lambda i, o: (0, 0)),
                  pl.BlockSpec((1, HY_FFN), lambda i, o: (0, 0)),
                  pl.BlockSpec((1, HY_FFN), lambda i, o: (0, 0)),
                  pl.BlockSpec((3 * HY_FFN, C), lambda i, o: (0, 2 * o + i // nL)),
                  pl.BlockSpec((1, C), lambda i, o: (0, 2 * o + i // nL))],
        out_specs=pl.BlockSpec((tm, C), lambda i, o: (i, o)),
        out_shape=jax.ShapeDtypeStruct((2 * L, HY_ORDER * C), F32),
        compiler_params=_cparams("parallel", "parallel"),
        name="hy_filter",
    )(jnp.asarray(zp), _stack3_rows(w1p), row(b1), row(freq1), _stack3_rows(w2), row(b2), row(freq2),
      _stack3_rows(w3), row(decay))


def _dft_constants(L):
    N = 2 * L
    N2 = DFT_N2
    N1 = N // N2
    K1 = N1 // 2 + 1
    k1 = np.arange(K1, dtype=np.float64)
    n1 = np.arange(N1, dtype=np.float64)
    ang = 2.0 * np.pi * np.outer(k1, n1) / N1
    a1 = np.stack([np.cos(ang), -np.sin(ang)], axis=1).reshape(2 * K1, N1)
    n1h = np.arange(N1 // 2, dtype=np.float64)
    angb = 2.0 * np.pi * np.outer(n1h, k1) / N1
    ck = np.where((np.arange(K1) == 0) | (np.arange(K1) == N1 // 2), 1.0, 2.0)[None, :]
    binv = np.stack([ck * np.cos(angb), -ck * np.sin(angb)], axis=2).reshape(N1 // 2, 2 * K1) / N
    n2 = np.arange(N2, dtype=np.float64)
    ph = -2.0 * np.pi * (np.outer(n2, n2) % N2) / N2
    dre, dim = np.cos(ph), np.sin(ph)
    dst = np.block([[dre, -dim], [dim, dre]])
    kron = lambda a: _stack3(np.kron(a, np.eye(SUBLANES)))
    return dict(N1=N1, K1=K1, a1_full=kron(a1), a1_half=kron(a1[:, :N1 // 2]), binv=kron(binv),
                d=_stack3(dst), dt=_stack3(dst.T))


def _twiddle(k1, n_total, width):
    n2 = _iota2((DFT_N2, LANES), 0).astype(F32)
    ang = (-2.0 * math.pi / n_total) * (k1.astype(F32) * n2)
    reps = width // LANES
    return jnp.tile(jnp.cos(ang), (1, reps)), jnp.tile(jnp.sin(ang), (1, reps))


def _inner_forward(d, y, k1, n_total):
    h = DFT_N2
    c, s = _twiddle(k1, n_total, y.shape[1])
    yr, yi = y[:h], y[h:]
    return _dot3(d, jnp.concatenate([yr * c - yi * s, yr * s + yi * c], axis=0))


def _dft_outer_body(a_ref, x_ref, o_ref):
    _, k, s, ct = x_ref.shape
    y = _dot3(a_ref[...], x_ref[0].reshape(k * s, ct))
    o_ref[0] = y.reshape(o_ref.shape[1], s, ct)


def dft_outer(a_kron, x, ct):
    B, K, N2, W = x.shape
    R = a_kron.shape[0] // SUBLANES
    ct = min(ct, W)
    return pl.pallas_call(
        _dft_outer_body,
        grid=(B, N2 // SUBLANES, W // ct),
        in_specs=[pl.BlockSpec(a_kron.shape, lambda b, s, j: (0, 0)),
                  pl.BlockSpec((1, K, SUBLANES, ct), lambda b, s, j: (b, 0, s, j))],
        out_specs=pl.BlockSpec((1, R, SUBLANES, ct), lambda b, s, j: (b, 0, s, j)),
        out_shape=jax.ShapeDtypeStruct((B, R, N2, W), F32),
        compiler_params=_cparams("parallel", "parallel", "parallel"),
        name="dft_outer",
    )(a_kron, x)


def _dft_inner_body(d_ref, y_ref, o_ref, *, n_total):
    o_ref[0, 0] = _inner_forward(d_ref[...], y_ref[0, 0], pl.program_id(0), n_total)


def dft_inner(d, y, ct, n_total):
    B, K1, R, W = y.shape
    return pl.pallas_call(
        functools.partial(_dft_inner_body, n_total=n_total),
        grid=(K1, B, W // ct),
        in_specs=[pl.BlockSpec(d.shape, lambda k, b, j: (0, 0)),
                  pl.BlockSpec((1, 1, R, ct), lambda k, b, j: (b, k, 0, j))],
        out_specs=pl.BlockSpec((1, 1, R, ct), lambda k, b, j: (b, k, 0, j)),
        out_shape=jax.ShapeDtypeStruct(y.shape, F32),
        compiler_params=_cparams("parallel", "parallel", "parallel"),
        name="dft_inner",
    )(d, y)


def _spec_mul_body(d_ref, dt_ref, y_ref, kf_ref, o_ref, *, n_total):
    k1 = pl.program_id(0)
    z = _inner_forward(d_ref[...], y_ref[0, 0], k1, n_total)
    h = DFT_N2
    zr, zi = z[:h], z[h:]
    kf = kf_ref[0, 0]
    kr, ki = kf[:h], kf[h:]
    q = _dot3(dt_ref[...], jnp.concatenate([zr * kr - zi * ki, zr * ki + zi * kr], axis=0))
    c, s = _twiddle(k1, n_total, q.shape[1])
    qr, qi = q[:h], q[h:]
    o_ref[0, 0] = jnp.concatenate([qr * c + qi * s, qi * c - qr * s], axis=0)


def spec_mul(d, dt, y, kf, order, C, ct, n_total):
    B, K1, R, _ = y.shape
    nc = C // ct
    return pl.pallas_call(
        functools.partial(_spec_mul_body, n_total=n_total),
        grid=(K1, B, nc),
        in_specs=[pl.BlockSpec(d.shape, lambda k, b, j: (0, 0)),
                  pl.BlockSpec(dt.shape, lambda k, b, j: (0, 0)),
                  pl.BlockSpec((1, 1, R, ct), lambda k, b, j: (b, k, 0, j)),
                  pl.BlockSpec((1, 1, R, ct), lambda k, b, j: (0, k, 0, order * nc + j))],
        out_specs=pl.BlockSpec((1, 1, R, ct), lambda k, b, j: (b, k, 0, j)),
        out_shape=jax.ShapeDtypeStruct(y.shape, F32),
        compiler_params=_cparams("parallel", "parallel", "parallel"),
        name="hy_spec_mul",
    )(d, dt, y, kf)


def _dft_out_body(a_ref, t_ref, u_ref, g_ref, s_ref, o_ref):
    _, k, s, ct = t_ref.shape
    y = _dot3(a_ref[...], t_ref[0].reshape(k * s, ct)).reshape(u_ref.shape[1], s, ct)
    o_ref[0] = g_ref[0] * (y + u_ref[0] * s_ref[...])


def dft_out(binv_kron, t, u, gate, skip):
    B, K, N2, C = t.shape
    R = u.shape[1]
    uspec = pl.BlockSpec((1, R, SUBLANES, C), lambda b, s: (b, 0, s, 0))
    return pl.pallas_call(
        _dft_out_body,
        grid=(B, N2 // SUBLANES),
        in_specs=[pl.BlockSpec(binv_kron.shape, lambda b, s: (0, 0)),
                  pl.BlockSpec((1, K, SUBLANES, C), lambda b, s: (b, 0, s, 0)),
                  uspec, uspec,
                  pl.BlockSpec((1, C), lambda b, s: (0, 0))],
        out_specs=uspec,
        out_shape=jax.ShapeDtypeStruct(u.shape, F32),
        compiler_params=_cparams("parallel", "parallel"),
        name="dft_out",
    )(binv_kron, t, u, gate, skip)


def hyena_mixer(p, C, conv_w, conv_b, f_w1, f_b1, f_freq1, f_w2, f_b2, f_freq2, f_w3, f_decay, skip):
    B, L, _ = p.shape
    cst = _dft_constants(L)
    N1, K1 = cst["N1"], cst["K1"]
    N2 = DFT_N2
    v, x1, x2 = hy_short_conv(p, conv_w, conv_b, C)
    kc = hy_filter(L, C, f_w1, f_b1, f_freq1, f_w2, f_b2, f_freq2, f_w3, f_decay)
    yf = dft_outer(cst["a1_full"], kc.reshape(1, N1, N2, HY_ORDER * C), C)
    kf = dft_inner(cst["d"], yf.reshape(1, K1, 2 * N2, HY_ORDER * C), C, 2 * L)

    def long_conv(u, order, gate):
        u4 = u.reshape(B, N1 // 2, N2, C)
        y1 = dft_outer(cst["a1_half"], u4, C)
        t = spec_mul(cst["d"], cst["dt"], y1.reshape(B, K1, 2 * N2, C), kf, order, C, C, 2 * L)
        out = dft_out(cst["binv"], t.reshape(B, 2 * K1, N2, C), u4, gate.reshape(B, N1 // 2, N2, C),
                      skip[order].reshape(1, C))
        return out.reshape(B, L, C)

    z = long_conv(v, 0, x1)
    return long_conv(z, 1, x2)


def _gla_chunk(q, k, v, lr3, up3, upb, st, reverse):
    C = GLA_CHUNK
    rr = _iota2((C, C), 0)
    cc = _iota2((C, C), 1)
    keep = (rr <= cc) if reverse else (rr >= cc)
    g = _log_sigmoid(_dot(lr3, up3) + upb) / GLA_TAU
    yield None
    bc = _mask_dot(keep.astype(BF16), g)
    yield None
    blast = bc[0:1] if reverse else bc[C - 1:C]
    q_in = (q * (GLA_DK ** -0.5) * jnp.exp(bc)).astype(BF16)
    k_in = (k * jnp.exp(-bc)).astype(BF16)
    k_end = (k * jnp.exp(blast - bc)).astype(BF16)
    vb = v.astype(BF16)
    att = _dot_nt(q_in, k_in)
    o_inter = _dot_nt(q_in, st.astype(BF16))
    st_new = st * jnp.exp(blast) + _dot_tn(vb, k_end)
    yield None
    o = _dot(jnp.where(keep, att, 0.0).astype(BF16), vb) + o_inter
    yield o, st_new


def _gla_body(qf_ref, kf_ref, vf_ref, lf_ref, qb_ref, kb_ref, vb_ref, lb_ref, up_ref, upb_ref,
              of_ref, ob_ref, st_ref, *, n_chunks):
    C = GLA_CHUNK

    @pl.when(pl.program_id(1) == 0)
    def _():
        st_ref[...] = jnp.zeros_like(st_ref)

    def chunk(step, carry):
        dirs = ((0, step, False, (qf_ref, kf_ref, vf_ref, lf_ref), of_ref),
                (1, n_chunks - 1 - step, True, (qb_ref, kb_ref, vb_ref, lb_ref), ob_ref))
        work = []
        for d, c, reverse, (q_ref, k_ref, v_ref, l_ref), o_ref in dirs:
            rows = pl.ds(pl.multiple_of(c * C, C), C)
            lr_hi, lr_lo = _split(l_ref[0, rows, :])
            lr3 = jnp.concatenate([lr_hi, lr_lo, lr_hi], axis=1)
            for h in range(GLA_HEADS):
                kl = slice(h * GLA_DK, (h + 1) * GLA_DK)
                vl = slice(h * GLA_DV, (h + 1) * GLA_DV)
                vals = (q_ref[0, rows, kl], k_ref[0, rows, kl], v_ref[0, rows, vl], lr3,
                        up_ref[d, :, kl], upb_ref[d, :, kl], st_ref[d, h])
                work.append((vals, reverse, o_ref, rows, vl, d, h))
        done = _interleave([_gla_chunk(*vals, reverse) for vals, reverse, *_ in work])
        for (o, st_new), (_, _, o_ref, rows, vl, d, h) in zip(done, work):
            o_ref[0, rows, vl] = o
            st_ref[d, h] = st_new
        return carry

    lax.fori_loop(0, n_chunks, chunk, 0)


def gla_scan(p, plr, up_pad, up_b, col0, tl=256):
    B, L, _ = p.shape
    H = GLA_HEADS
    nt = L // tl
    kw, vw = H * GLA_DK, H * GLA_DV
    qb = col0 * LANES // kw
    vb = (col0 * LANES + 2 * kw) // vw
    assert col0 * LANES % kw == 0 and (col0 * LANES + 2 * kw) % vw == 0
    specs = []
    for tmap in (lambda t: t, lambda t: nt - 1 - t):
        specs += [pl.BlockSpec((1, tl, kw), lambda b, t, tmap=tmap: (b, tmap(t), qb)),
                  pl.BlockSpec((1, tl, kw), lambda b, t, tmap=tmap: (b, tmap(t), qb + 1)),
                  pl.BlockSpec((1, tl, vw), lambda b, t, tmap=tmap: (b, tmap(t), vb)),
                  pl.BlockSpec((1, tl, LANES), lambda b, t, tmap=tmap: (b, tmap(t), 0))]
    specs += [pl.BlockSpec((2, 3 * LANES, kw), lambda b, t: (0, 0, 0)),
              pl.BlockSpec((2, 1, kw), lambda b, t: (0, 0, 0))]
    out = jax.ShapeDtypeStruct((B, L, vw), F32)
    return pl.pallas_call(
        functools.partial(_gla_body, n_chunks=tl // GLA_CHUNK),
        grid=(B, nt),
        in_specs=specs,
        out_specs=[pl.BlockSpec((1, tl, vw), lambda b, t: (b, t, 0)),
                   pl.BlockSpec((1, tl, vw), lambda b, t: (b, nt - 1 - t, 0))],
        out_shape=[out, out],
        scratch_shapes=[pltpu.VMEM((2, H, GLA_DV, GLA_DK), F32)],
        compiler_params=_cparams("parallel", "arbitrary"),
        name="gla_scan",
    )(p, p, p, plr, p, p, p, plr, up_pad, up_b)


def _gla_post_body(of_ref, ob_ref, r_ref, gn_ref, o_ref):
    for h in range(of_ref.shape[2] // GLA_DV):
        sl = slice(h * GLA_DV, (h + 1) * GLA_DV)
        o = of_ref[0, :, sl] + ob_ref[0, :, sl]
        o = o * lax.rsqrt(jnp.mean(o * o, axis=-1, keepdims=True) + NORM_EPS) * gn_ref[:, sl]
        r = r_ref[0, :, sl]
        o_ref[0, :, sl] = (o * (r * jax.nn.sigmoid(r))).astype(o_ref.dtype)


def gla_post(o_f, o_b, p, r_block, norm_g, tl=512):
    B, L, W = o_f.shape
    spec = pl.BlockSpec((1, tl, W), lambda b, t: (b, t, 0))
    return pl.pallas_call(
        _gla_post_body,
        grid=(B, L // tl),
        in_specs=[spec, spec,
                  pl.BlockSpec((1, tl, W), lambda b, t: (b, t, r_block)),
                  pl.BlockSpec((1, W), lambda b, t: (0, 0))],
        out_specs=spec,
        out_shape=jax.ShapeDtypeStruct((B, L, W), BF16),
        compiler_params=_cparams("parallel", "parallel"),
        name="gla_post",
    )(o_f, o_b, p, norm_g.reshape(1, W))


def gla_mixer(p, plr, col0, up, up_b, norm_g):
    H = GLA_HEADS
    up_pad = jnp.zeros((2, LANES, H * GLA_DK), F32)
    up_pad = up_pad.at[0, 0:GLA_RANK].set(up[0]).at[1, GLA_RANK:2 * GLA_RANK].set(up[1])
    up_hi, up_lo = _split_bits(up_pad)
    up3 = jnp.concatenate([up_hi, up_hi, up_lo], axis=1)
    o_f, o_b = gla_scan(p, plr, up3, up_b.reshape(2, 1, H * GLA_DK), col0)
    r_col = col0 * LANES + 2 * H * GLA_DK + H * GLA_DV
    assert r_col % (H * GLA_DV) == 0
    return gla_post(o_f, o_b, p, r_col // (H * GLA_DV), norm_g)


def _rw_mix_body(x_ref, xm_ref, xp_ref, g_ref, sc_ref, sh_ref, mu_ref, *o_refs, n_tiles):
    i = pl.program_id(1)
    g, sc, sh = g_ref[...], sc_ref[0], sh_ref[0]
    h = _norm_mod_value(x_ref[0], g, sc, sh)
    hm = _norm_mod_value(xm_ref[0], g, sc, sh)
    hp = _norm_mod_value(xp_ref[0], g, sc, sh)
    hm1, hp1 = _neighbours(h, hm, hp, i, n_tiles)
    xx = 0.5 * (hm1 + hp1) - h
    mu = mu_ref[...]
    for j, o_ref in enumerate(o_refs):
        o_ref[0] = (h + xx * mu[j:j + 1]).astype(o_ref.dtype)


def rw_mix(x, g, scale, shift, mu, tl=256):
    B, L, D = x.shape
    n_tiles = L // tl
    out = jax.ShapeDtypeStruct((B, L, D), BF16)
    ospec = pl.BlockSpec((1, tl, D), lambda b, i: (b, i, 0))
    mu8 = jnp.zeros((8, D), F32).at[:6].set(mu)
    return pl.pallas_call(
        functools.partial(_rw_mix_body, n_tiles=n_tiles),
        grid=(B, n_tiles),
        in_specs=_neighbour_specs(tl, D, L // 8) + [pl.BlockSpec((1, D), lambda b, i: (0, 0)),
                                                    pl.BlockSpec((1, 1, D), lambda b, i: (b, 0, 0)),
                                                    pl.BlockSpec((1, 1, D), lambda b, i: (b, 0, 0)),
                                                    pl.BlockSpec((8, D), lambda b, i: (0, 0))],
        out_specs=[ospec] * 6,
        out_shape=[out] * 6,
        compiler_params=_cparams("parallel", "parallel"),
        name="rw_mix",
    )(x, x, x, g.reshape(1, D), scale, shift, mu8)


def _head_ones():
    r = _iota2((LANES, LANES), 0) // RW_HEAD
    c = _iota2((LANES, LANES), 1) // RW_HEAD
    return (r == c).astype(BF16)


def _rw_prep_body(k_ref, r_ref, v_ref, a0_ref, a1_ref, kk_w_ref, ka_ref, rk_ref, kk_ref, bon_ref):
    ones = _head_ones()
    for j in range(k_ref.shape[1] // LANES):
        sl = slice(j * LANES, (j + 1) * LANES)
        k = k_ref[:, sl]
        kk = k * kk_w_ref[:, sl]
        ss = _dot_mask(kk * kk, ones)
        kk_ref[:, sl] = kk * lax.rsqrt(jnp.maximum(ss, 1e-24))
        a_sum = jax.nn.sigmoid(a0_ref[:, sl]) + jax.nn.sigmoid(a1_ref[:, sl])
        kd_sum = k * (2.0 + (a_sum - 2.0) * ka_ref[:, sl])
        bon_ref[:, sl] = _dot_mask(r_ref[:, sl] * kd_sum * rk_ref[:, sl], ones) * v_ref[:, sl]


def rw_prep(k, r, v, apre0, apre1, k_k, k_a, r_k, tm=256):
    T, D = k.shape
    tm = min(tm, T)
    spec = pl.BlockSpec((tm, D), lambda i: (i, 0))
    pspec = pl.BlockSpec((1, D), lambda i: (0, 0))
    out = jax.ShapeDtypeStruct((T, D), F32)
    return pl.pallas_call(
        _rw_prep_body,
        grid=(T // tm,),
        in_specs=[spec, spec, spec, spec, spec, pspec, pspec, pspec],
        out_specs=[spec, spec],
        out_shape=[out, out],
        compiler_params=_cparams("parallel"),
        name="rw_prep",
    )(k, r, v, apre0, apre1, k_k.reshape(1, D), k_a.reshape(1, D), r_k.reshape(1, D))


def _rw_chunk(r, k, v, kk, wpre, apre, ka, h, reverse):
    C = RW_CHUNK
    P = 2 * C
    rc = _iota2((C, C), 0)
    cc = _iota2((C, C), 1)
    tri = ((rc <= cc) if reverse else (rc >= cc)).astype(BF16)
    t_idx = _iota2((C, P), 0)
    s_idx = _iota2((C, P), 1) % C
    strict = (t_idx < s_idx) if reverse else (t_idx > s_idx)
    incl = (t_idx <= s_idx) if reverse else (t_idx >= s_idx)
    head0 = _iota2((C, LANES), 1) < RW_HEAD
    same_head = (_iota2((LANES, LANES), 0) // RW_HEAD) == (_iota2((LANES, LANES), 1) // RW_HEAD)

    def stack(x):
        return jnp.concatenate([jnp.where(head0, x, 0.0), jnp.where(head0, 0.0, x)], axis=0).astype(BF16)

    lw = -RW_LOG_DECAY_MAX * jax.nn.sigmoid(wpre)
    a = jax.nn.sigmoid(apre)
    kd = k * (1.0 + (a - 1.0) * ka)
    b = kk * a
    cum = _mask_dot(tri, lw)
    yield None
    cum_x = cum - lw
    total = cum[0:1] if reverse else cum[C - 1:C]
    e_neg = jnp.exp(-cum)
    rn = (r * jnp.exp(cum)).astype(BF16)
    xn = (kk * jnp.exp(cum_x)).astype(BF16)
    kn = kd * e_neg
    bn = b * e_neg
    vs = stack(v)
    coef = _dot_nt(jnp.concatenate([xn, rn], axis=0), jnp.concatenate([stack(kn), stack(bn)], axis=0))
    yield None
    a_k = jnp.where(strict, coef[:C, :P], 0.0).astype(BF16)
    pw = jnp.where(strict, -coef[:C, P:], 0.0)
    q_k = jnp.where(incl, coef[C:, :P], 0.0).astype(BF16)
    q_b = jnp.where(incl, coef[C:, P:], 0.0).astype(BF16)
    hv = jnp.concatenate([h.astype(BF16), vs], axis=0)
    x = _dot(jnp.concatenate([xn, a_k], axis=1), hv)
    o_hv = _dot(jnp.concatenate([rn, q_k], axis=1), hv)
    yield None
    for _ in range(int(math.log2(C)) - 1):
        both = _dot(pw.astype(BF16), jnp.concatenate([stack(x), stack(pw)], axis=1))
        x = x + both[:, :P]
        pw = both[:, P:]
        yield None
    u = x + _dot(pw.astype(BF16), stack(x))
    yield None
    o = o_hv - _dot(q_b, stack(u))
    upd = _dot_tn(jnp.concatenate([kn, bn], axis=0).astype(BF16),
                  jnp.concatenate([v, -u], axis=0).astype(BF16))
    gcol = jnp.transpose(jnp.broadcast_to(jnp.exp(total), (LANES, LANES)))
    h_new = gcol * (h + jnp.where(same_head, upd, 0.0))
    yield o, h_new


def _interleave(chains):
    while True:
        outs = [next(ch) for ch in chains]
        if outs[0] is not None:
            return outs


def _rw_scan_body(rf_ref, kf_ref, vf_ref, kkf_ref, wf_ref, af_ref, rb_ref, kb_ref, vb_ref, kkb_ref, wb_ref,
                  ab_ref, ka_ref, of_ref, ob_ref, h_ref, *, n_chunks, n_pairs):
    C = RW_CHUNK

    @pl.when(pl.program_id(2) == 0)
    def _():
        h_ref[...] = jnp.zeros_like(h_ref)

    def chunk(step, carry):
        dirs = ((0, step, False, (rf_ref, kf_ref, vf_ref, kkf_ref, wf_ref, af_ref), of_ref),
                (1, n_chunks - 1 - step, True, (rb_ref, kb_ref, vb_ref, kkb_ref, wb_ref, ab_ref), ob_ref))
        work = []
        for d, c, reverse, refs, o_ref in dirs:
            rows = pl.ds(pl.multiple_of(c * C, C), C)
            for g in range(n_pairs):
                lanes = slice(g * LANES, (g + 1) * LANES)
                vals = [ref[0, rows, lanes] for ref in refs]
                work.append((vals, ka_ref[:, lanes], h_ref[d, g], reverse, o_ref, rows, lanes, d, g))
        done = _interleave([_rw_chunk(*vals, ka, h, reverse) for vals, ka, h, reverse, *_ in work])
        for (o, h_new), (_, _, _, _, o_ref, rows, lanes, d, g) in zip(done, work):
            o_ref[0, rows, lanes] = o
            h_ref[d, g] = h_new
        return carry

    lax.fori_loop(0, n_chunks, chunk, 0)


def rw_scan(r, k, v, kk, wpre, apre, k_a, tl=256, n_pairs=8):
    B, L, D = r.shape
    width = n_pairs * LANES
    nt = L // tl
    fspec = pl.BlockSpec((1, tl, width), lambda b, p, t: (b, t, p))
    bspec = pl.BlockSpec((1, tl, width), lambda b, p, t: (b, nt - 1 - t, p))
    out = jax.ShapeDtypeStruct((B, L, D), F32)
    return pl.pallas_call(
        functools.partial(_rw_scan_body, n_chunks=tl // RW_CHUNK, n_pairs=n_pairs),
        grid=(B, D // width, nt),
        in_specs=[fspec] * 6 + [bspec] * 6 + [pl.BlockSpec((1, width), lambda b, p, t: (0, p))],
        out_specs=[fspec, bspec],
        out_shape=[out, out],
        scratch_shapes=[pltpu.VMEM((2, n_pairs, LANES, LANES), F32)],
        compiler_params=_cparams("parallel", "parallel", "arbitrary"),
        name="rw_scan",
    )(r, k, v, kk, wpre[0], apre[0], r, k, v, kk, wpre[1], apre[1], k_a.reshape(1, D))


def _rw_post_body(of_ref, ob_ref, bon_ref, g_ref, gng_ref, gnb_ref, o_ref):
    ones = _head_ones()
    for j in range(of_ref.shape[1] // LANES):
        sl = slice(j * LANES, (j + 1) * LANES)
        so = of_ref[:, sl] + ob_ref[:, sl]
        mean = _dot_mask(so, ones) * (1.0 / RW_HEAD)
        dlt = so - mean
        var = _dot_mask(dlt * dlt, ones) * (1.0 / RW_HEAD)
        y = dlt * lax.rsqrt(var + RW_GN_EPS) * gng_ref[:, sl] + gnb_ref[:, sl]
        o_ref[:, sl] = ((y + bon_ref[:, sl]) * g_ref[:, sl]).astype(o_ref.dtype)


def rw_post(o_f, o_b, bonus, g, gn_g, gn_b, tm=256):
    T, D = o_f.shape
    tm = min(tm, T)
    spec = pl.BlockSpec((tm, D), lambda i: (i, 0))
    pspec = pl.BlockSpec((1, D), lambda i: (0, 0))
    return pl.pallas_call(
        _rw_post_body,
        grid=(T // tm,),
        in_specs=[spec, spec, spec, spec, pspec, pspec],
        out_specs=spec,
        out_shape=jax.ShapeDtypeStruct((T, D), BF16),
        compiler_params=_cparams("parallel"),
        name="rw_post",
    )(o_f, o_b, bonus, g, gn_g.reshape(1, D), gn_b.reshape(1, D))


def _pad_cols(w, n):
    return jnp.pad(w, ((0, 0), (0, n - w.shape[1])))


def _pad_rows(w, n):
    return jnp.pad(w, ((0, n - w.shape[0]), (0, 0)))


def rwkv7_block(x, gate, g_norm, scale, shift, mu, w_r, w_k, w_v, w0, w1, w2, a0, a1, a2, g1, g2,
                k_k, k_a, r_k, gn_g, gn_b, w_o):
    B, L, D = x.shape
    T = B * L
    xr, xw, xk, xv, xa, xg = [a.reshape(T, D) for a in rw_mix(x, g_norm, scale, shift, mu)]
    r = matmul(xr, w_r.astype(BF16))
    k = matmul(xk, w_k.astype(BF16))
    v = matmul(xv, w_v.astype(BF16))
    w1c = jnp.concatenate([_pad_cols(w1[0], LANES), _pad_cols(w1[1], LANES)], axis=1).astype(BF16)
    a1c = jnp.concatenate([_pad_cols(a1[0], LANES), _pad_cols(a1[1], LANES)], axis=1).astype(BF16)
    wmid = matmul(xw, w1c)
    amid = matmul(xa, a1c)
    gmid = matmul(xg, g1.astype(BF16))
    wpre = [matmul(wmid, _pad_rows(w2[d], LANES).astype(BF16), a_act="tanh", bias=w0[d], a_col_block=d)
            for d in range(2)]
    apre = [matmul(amid, _pad_rows(a2[d], LANES).astype(BF16), bias=a0[d], a_col_block=d)
            for d in range(2)]
    g = matmul(gmid, g2.astype(BF16), a_act="sigmoid")
    kk, bonus = rw_prep(k, r, v, apre[0], apre[1], k_k, k_a, r_k.reshape(D))
    sh = lambda a: a.reshape(B, L, D)
    o_f, o_b = rw_scan(sh(r), sh(k), sh(v), sh(kk), [sh(a) for a in wpre], [sh(a) for a in apre], k_a)
    y = rw_post(o_f.reshape(T, D), o_b.reshape(T, D), bonus, g, gn_g, gn_b)
    out = matmul(y, w_o.astype(BF16), res=x.reshape(T, D), gate=gate, rows_per_batch=L)
    return out.reshape(B, L, D)


def ffn_block(x, c, norm_g, ada_w, ada_b, idx, w1, w2):
    B, L, D = x.shape
    T = B * L
    shift, scale, gate = ada_mod(c, ada_w, ada_b, idx)
    h = norm_mod(x, norm_g, scale, shift).reshape(T, D)
    hid = matmul(h, w1.astype(BF16), w_idx=idx, out_dtype=BF16, epi="sqrelu")
    out = matmul(hid, w2.astype(BF16), w_idx=idx, res=x.reshape(T, D), gate=gate, rows_per_batch=L,
                 tm=1024, tn=256)
    return out.reshape(B, L, D)


def hyena_gla_block(x, gate, h, w_in, w_out, conv_w, conv_b, f_w1, f_b1, f_freq1, f_w2, f_b2, f_freq2,
                    f_w3, f_decay, hy_skip, gla_up, gla_up_b, gla_norm_g):
    B, L, D = x.shape
    T = B * L
    C = conv_w.shape[1] // 3
    n_main = w_in.shape[1] - 2 * GLA_RANK
    w_in = w_in.astype(BF16)
    p = matmul(h, w_in, n_cols=n_main).reshape(B, L, n_main)
    plr = matmul(h, _pad_cols(w_in[:, n_main:], LANES)).reshape(B, L, LANES)
    y_hy = hyena_mixer(p, C, conv_w, conv_b, f_w1, f_b1, f_freq1, f_w2, f_b2, f_freq2, f_w3, f_decay, hy_skip)
    y_gla = gla_mixer(p, plr, 3 * C // LANES, gla_up, gla_up_b, gla_norm_g)
    out = matmul(y_hy.reshape(T, -1), w_out.astype(BF16), a2=y_gla.reshape(T, -1), res=x.reshape(T, D),
                 gate=gate, rows_per_batch=L)
    return out.reshape(B, L, D)


def kernel(x, c, mix_norm_g, mix_ada_w, mix_ada_b, ab_w_in, ab_w_out, hy_conv_w, hy_conv_b, hy_ffn_w1, hy_ffn_b1, hy_freq1, hy_ffn_w2, hy_ffn_b2, hy_freq2, hy_ffn_w3, hy_decay, hy_skip, gla_up, gla_up_b, gla_norm_g, tm_norm_g, tm_ada_w, tm_ada_b, rw_mu, rw_w_r, rw_w_k, rw_w_v, rw_w0, rw_w1, rw_w2, rw_a0, rw_a1, rw_a2, rw_g1, rw_g2, rw_k_k, rw_k_a, rw_r_k, rw_gn_g, rw_gn_b, rw_w_o, ffn_norm_g, ffn_ada_w, ffn_ada_b, ffn_w1, ffn_w2, final_norm_g):
    B, L, D = x.shape
    depth = ffn_w1.shape[0]
    for layer in range(depth):
        i = layer // 2
        if layer % 2 == 0:
            shift, scale, gate = ada_mod(c, mix_ada_w, mix_ada_b, i)
            h = norm_mod(x, mix_norm_g[i], scale, shift).reshape(B * L, D)
            x = hyena_gla_block(x, gate, h, ab_w_in[i], ab_w_out[i], hy_conv_w[i], hy_conv_b[i],
                                hy_ffn_w1[i], hy_ffn_b1[i], hy_freq1[i], hy_ffn_w2[i], hy_ffn_b2[i],
                                hy_freq2[i], hy_ffn_w3[i], hy_decay[i], hy_skip[i],
                                gla_up[i], gla_up_b[i], gla_norm_g[i])
        else:
            shift, scale, gate = ada_mod(c, tm_ada_w, tm_ada_b, i)
            x = rwkv7_block(x, gate, tm_norm_g[i], scale, shift, rw_mu[i], rw_w_r[i], rw_w_k[i], rw_w_v[i],
                            rw_w0[i], rw_w1[i], rw_w2[i], rw_a0[i], rw_a1[i], rw_a2[i], rw_g1[i], rw_g2[i],
                            rw_k_k[i], rw_k_a[i], rw_r_k[i], rw_gn_g[i], rw_gn_b[i], rw_w_o[i])
        x = ffn_block(x, c, ffn_norm_g[layer], ffn_ada_w, ffn_ada_b, layer, ffn_w1, ffn_w2)
    return rms_final(x, final_norm_g)
```

```python
import functools
import math

import numpy as np
import jax
import jax.numpy as jnp
from jax import lax
from jax.experimental import pallas as pl
from jax.experimental.pallas import tpu as pltpu

F32 = jnp.float32
BF16 = jnp.bfloat16
HI = lax.Precision.HIGHEST

LANES = 128
SUBLANES = 8
NORM_EPS = 1e-6
GLA_HEADS = 4
GLA_DK = 128
GLA_DV = 256
GLA_RANK = 16
GLA_TAU = 16.0
GLA_CHUNK = 64
RW_HEAD = 64
RW_CHUNK = 64
RW_LOG_DECAY_MAX = 0.606531
RW_GN_EPS = 64e-5
HY_ORDER = 2
HY_EMB = 33
HY_FFN = 64
DFT_N2 = 128
VMEM_LIMIT = 52 * 1024 * 1024


def _cparams(*sem):
    return pltpu.CompilerParams(dimension_semantics=sem, vmem_limit_bytes=VMEM_LIMIT)


def _dot(a, b, precision=None):
    return jnp.dot(a, b, preferred_element_type=F32, precision=precision)


def _dot_nt(a, b, precision=None):
    return lax.dot_general(a, b, (((1,), (1,)), ((), ())), preferred_element_type=F32, precision=precision)


def _dot_tn(a, b, precision=None):
    return lax.dot_general(a, b, (((0,), (0,)), ((), ())), preferred_element_type=F32, precision=precision)


def _iota2(shape, axis):
    return lax.broadcasted_iota(jnp.int32, shape, axis)


def _log_sigmoid(x):
    return jnp.minimum(x, 0.0) - jnp.log(1.0 + jnp.exp(-jnp.abs(x)))


def _split(x):
    hi = x.astype(BF16)
    return hi, (x - hi.astype(F32)).astype(BF16)


def _mask_dot(mask_bf16, x):
    hi, lo = _split(x)
    n = x.shape[1]
    y = _dot(mask_bf16, jnp.concatenate([hi, lo], axis=1))
    return y[:, :n] + y[:, n:]


def _stack3(a):
    a = np.asarray(a, np.float32)
    hi = a.astype(BF16)
    lo = (a - hi.astype(np.float32)).astype(BF16)
    return jnp.asarray(np.concatenate([hi, hi, lo], axis=1))


def _dot3(a3, x):
    hi, lo = _split(x)
    return _dot(a3, jnp.concatenate([hi, lo, hi], axis=0))


def _split_bits(w):
    bits = lax.bitcast_convert_type(w, jnp.uint32) & jnp.uint32(0xFFFF0000)
    hi = lax.bitcast_convert_type(bits, F32)
    return hi.astype(BF16), (w - hi).astype(BF16)


def _stack3_rows(w):
    hi, lo = _split_bits(w)
    return jnp.concatenate([hi, hi, lo], axis=0)


def _dot3r(x, w3):
    hi, lo = _split(x)
    return _dot(jnp.concatenate([hi, lo, hi], axis=1), w3)


def _dot_mask(x, mask_bf16):
    hi, lo = _split(x)
    return _dot(hi, mask_bf16) + _dot(lo, mask_bf16)


def _ada_body(c_ref, w_ref, b_ref, o_ref):
    c = c_ref[...]
    s = c * jax.nn.sigmoid(c)
    o_ref[...] = _dot(s, w_ref[0], HI) + b_ref[0]


def ada_mod(c, w, b, idx):
    B, D = c.shape
    n3 = w.shape[2]
    tn = 512
    out = pl.pallas_call(
        _ada_body,
        grid=(n3 // tn,),
        in_specs=[pl.BlockSpec((B, D), lambda j: (0, 0)),
                  pl.BlockSpec((1, D, tn), lambda j: (idx, 0, j)),
                  pl.BlockSpec((1, 1, tn), lambda j: (idx, 0, j))],
        out_specs=pl.BlockSpec((B, tn), lambda j: (0, j)),
        out_shape=jax.ShapeDtypeStruct((B, n3), F32),
        compiler_params=_cparams("parallel"),
        name="ada_mod",
    )(c, w, b.reshape(b.shape[0], 1, n3))
    shift, scale, gate = jnp.split(out[:, None, :], 3, axis=-1)
    return shift, scale, gate


def _norm_mod_value(x, g, scale, shift):
    y = x * lax.rsqrt(jnp.mean(x * x, axis=-1, keepdims=True) + NORM_EPS)
    return (y * g) * (1.0 + scale) + shift


def _norm_mod_body(x_ref, g_ref, sc_ref, sh_ref, o_ref):
    o_ref[0] = _norm_mod_value(x_ref[0], g_ref[...], sc_ref[0], sh_ref[0]).astype(o_ref.dtype)


def norm_mod(x, g, scale, shift, out_dtype=BF16, tm=512):
    B, L, D = x.shape
    return pl.pallas_call(
        _norm_mod_body,
        grid=(B, L // tm),
        in_specs=[pl.BlockSpec((1, tm, D), lambda b, i: (b, i, 0)),
                  pl.BlockSpec((1, D), lambda b, i: (0, 0)),
                  pl.BlockSpec((1, 1, D), lambda b, i: (b, 0, 0)),
                  pl.BlockSpec((1, 1, D), lambda b, i: (b, 0, 0))],
        out_specs=pl.BlockSpec((1, tm, D), lambda b, i: (b, i, 0)),
        out_shape=jax.ShapeDtypeStruct((B, L, D), out_dtype),
        compiler_params=_cparams("parallel", "parallel"),
        name="norm_mod",
    )(x, g.reshape(1, D), scale, shift)


def _rms_body(x_ref, g_ref, o_ref):
    x = x_ref[0]
    y = x * lax.rsqrt(jnp.mean(x * x, axis=-1, keepdims=True) + NORM_EPS)
    o_ref[0] = y * g_ref[...]


def rms_final(x, g, tm=512):
    B, L, D = x.shape
    return pl.pallas_call(
        _rms_body,
        grid=(B, L // tm),
        in_specs=[pl.BlockSpec((1, tm, D), lambda b, i: (b, i, 0)),
                  pl.BlockSpec((1, D), lambda b, i: (0, 0))],
        out_specs=pl.BlockSpec((1, tm, D), lambda b, i: (b, i, 0)),
        out_shape=jax.ShapeDtypeStruct((B, L, D), F32),
        compiler_params=_cparams("parallel", "parallel"),
        name="rms_final",
    )(x, g.reshape(1, D))


def _mm_body(*refs, nk, a_act, epi, has_a2, has_bias, has_res):
    it = iter(refs)
    a_ref = next(it)
    w_ref = next(it)
    a2_ref = next(it) if has_a2 else None
    w2_ref = next(it) if has_a2 else None
    bias_ref = next(it) if has_bias else None
    res_ref = next(it) if has_res else None
    gate_ref = next(it) if has_res else None
    o_ref = next(it)
    acc_ref = next(it) if nk > 1 else None

    a = a_ref[...]
    if a_act == "tanh":
        a = jnp.tanh(a.astype(F32))
    elif a_act == "sigmoid":
        a = jax.nn.sigmoid(a.astype(F32))
    part = _dot(a.astype(BF16), w_ref[0])
    if has_a2:
        part = part + _dot(a2_ref[...].astype(BF16), w2_ref[0])

    def finish(acc):
        if has_bias:
            acc = acc + bias_ref[...]
        if epi == "sqrelu":
            acc = jnp.square(jnp.maximum(acc, 0.0))
        if has_res:
            acc = res_ref[...] + gate_ref[0] * acc
        o_ref[...] = acc.astype(o_ref.dtype)

    if nk == 1:
        finish(part)
    else:
        k = pl.program_id(2)

        @pl.when(k == 0)
        def _():
            acc_ref[...] = part

        @pl.when(k > 0)
        def _():
            acc_ref[...] += part

        @pl.when(k == nk - 1)
        def _():
            finish(acc_ref[...])


def matmul(a, w, *, a2=None, w_idx=0, n_cols=None, out_dtype=F32, a_act=None, epi=None, bias=None, res=None,
           gate=None, rows_per_batch=None, tm=1024, tn=1024, tk=None, a_col_block=0):
    M = a.shape[0]
    if w.ndim == 2:
        w = w[None]
    K = w.shape[1]
    N = w.shape[2] if n_cols is None else n_cols
    has_a2 = a2 is not None
    ka = K // 2 if has_a2 else K
    tm = min(tm, M if rows_per_batch is None else rows_per_batch)
    tn = min(tn, N)
    tk = ka if tk is None else min(tk, ka)
    nk = ka // tk
    assert M % tm == 0 and N % tn == 0 and ka % tk == 0 and not (has_a2 and nk > 1)
    has_bias = bias is not None
    has_res = res is not None
    in_specs = [pl.BlockSpec((tm, tk), lambda i, j, k: (i, k + a_col_block * nk)),
                pl.BlockSpec((1, tk, tn), lambda i, j, k: (w_idx, k, j))]
    args = [a, w]
    if has_a2:
        in_specs += [pl.BlockSpec((tm, tk), lambda i, j, k: (i, 0)),
                     pl.BlockSpec((1, tk, tn), lambda i, j, k: (w_idx, 1, j))]
        args += [a2, w]
    if has_bias:
        in_specs.append(pl.BlockSpec((1, tn), lambda i, j, k: (0, j)))
        args.append(bias.reshape(1, N).astype(F32))
    if has_res:
        tiles_per_batch = rows_per_batch // tm
        assert rows_per_batch % tm == 0
        in_specs.append(pl.BlockSpec((tm, tn), lambda i, j, k: (i, j)))
        in_specs.append(pl.BlockSpec((1, 1, tn), lambda i, j, k: (i // tiles_per_batch, 0, j)))
        args += [res, gate]
    scratch = [pltpu.VMEM((tm, tn), F32)] if nk > 1 else []
    body = functools.partial(_mm_body, nk=nk, a_act=a_act, epi=epi, has_a2=has_a2, has_bias=has_bias,
                             has_res=has_res)
    return pl.pallas_call(
        body,
        grid=(M // tm, N // tn, nk),
        in_specs=in_specs,
        out_specs=pl.BlockSpec((tm, tn), lambda i, j, k: (i, j)),
        out_shape=jax.ShapeDtypeStruct((M, N), out_dtype),
        scratch_shapes=scratch,
        compiler_params=_cparams("parallel", "parallel", "arbitrary"),
        name="matmul",
    )(*args)


def _neighbours(cur, prev8, next8, i, n_tiles):
    t = cur.shape[0]
    row = _iota2(cur.shape, 0)
    before = jnp.where(i > 0, prev8[7:8, :], 0.0)
    after = jnp.where(i < n_tiles - 1, next8[0:1, :], 0.0)
    xm1 = jnp.where(row == 0, before, pltpu.roll(cur, 1, 0))
    xp1 = jnp.where(row == t - 1, after, pltpu.roll(cur, t - 1, 0))
    return xm1, xp1


def _neighbour_specs(tl, width, n_rows8, col_map=None):
    r8 = tl // 8
    return [pl.BlockSpec((1, tl, width), lambda b, i: (b, i, 0)),
            pl.BlockSpec((1, 8, width), lambda b, i: (b, jnp.maximum(i * r8 - 1, 0), 0)),
            pl.BlockSpec((1, 8, width), lambda b, i: (b, jnp.minimum((i + 1) * r8, n_rows8 - 1), 0))]


def _hy_conv_body(p_ref, pm_ref, pp_ref, w_ref, b_ref, v_ref, x1_ref, x2_ref, *, n_tiles, C):
    i = pl.program_id(1)
    cur = p_ref[0]
    xm1, xp1 = _neighbours(cur, pm_ref[0], pp_ref[0], i, n_tiles)
    w = w_ref[...]
    u = xm1 * w[0:1] + cur * w[1:2] + xp1 * w[2:3] + b_ref[...]
    v_ref[0] = u[:, 0:C]
    x1_ref[0] = u[:, C:2 * C]
    x2_ref[0] = u[:, 2 * C:3 * C]


def hy_short_conv(p, conv_w, conv_b, C, tl=256):
    B, L, _ = p.shape
    W = 3 * C
    n_tiles = L // tl
    out = jax.ShapeDtypeStruct((B, L, C), F32)
    ospec = pl.BlockSpec((1, tl, C), lambda b, i: (b, i, 0))
    return pl.pallas_call(
        functools.partial(_hy_conv_body, n_tiles=n_tiles, C=C),
        grid=(B, n_tiles),
        in_specs=_neighbour_specs(tl, W, L // 8) + [pl.BlockSpec((3, W), lambda b, i: (0, 0)),
                                                    pl.BlockSpec((1, W), lambda b, i: (0, 0))],
        out_specs=[ospec, ospec, ospec],
        out_shape=[out, out, out],
        compiler_params=_cparams("parallel", "parallel"),
        name="hy_short_conv",
    )(p, p, p, conv_w, conv_b.reshape(1, W))


def _hy_filter_body(z_ref, w1_ref, b1_ref, f1_ref, w2_ref, b2_ref, f2_ref, w3_ref, dec_ref, o_ref, *, L, tm):
    i = pl.program_id(0)
    z = z_ref[...]
    f = jnp.sin(f1_ref[...] * (_dot3r(z, w1_ref[...]) + b1_ref[...]))
    f = jnp.sin(f2_ref[...] * (_dot3r(f, w2_ref[...]) + b2_ref[...]))
    t = z[:, 0:1]
    f = _dot3r(f, w3_ref[...]) * jnp.exp(-t * jnp.abs(dec_ref[...]))
    n = i * tm + _iota2(f.shape, 0)
    o_ref[...] = jnp.where(n == L, 0.0, f)


def hy_filter(L, C, w1, b1, freq1, w2, b2, freq2, w3, decay, tm=512):
    pos = np.concatenate([np.arange(L), [0], np.arange(L - 1, 0, -1)]).astype(np.float64)
    t = pos / (L - 1)
    ang = 2.0 * math.pi * pos / L
    nb = (HY_EMB - 1) // 2
    ba = np.linspace(1e-4, nb - 1, nb)[None, :] * ang[:, None]
    z = np.concatenate([t[:, None], np.cos(ba), -np.sin(ba)], axis=-1)
    zp = np.zeros((2 * L, 2 * HY_FFN), np.float32)
    zp[:, :HY_EMB] = z
    w1p = jnp.zeros((2 * HY_FFN, HY_FFN), F32).at[:HY_EMB].set(w1)
    nL = L // tm
    row = lambda a: a.reshape(1, -1)
    return pl.pallas_call(
        functools.partial(_hy_filter_body, L=L, tm=tm),
        grid=(2 * L // tm, HY_ORDER),
        in_specs=[pl.BlockSpec((tm, 2 * HY_FFN), lambda i, o: (i, 0)),
                  pl.BlockSpec((6 * HY_FFN, HY_FFN), lambda i, o: (0, 0)),
                  pl.BlockSpec((1, HY_FFN), lambda i, o: (0, 0)),
                  pl.BlockSpec((1, HY_FFN), lambda i, o: (0, 0)),
                  pl.BlockSpec((3 * HY_FFN, HY_FFN), lambda i, o: (0, 0)),
                  pl.BlockSpec((1, HY_FFN), lambda i, o: (0, 0)),
                  pl.BlockSpec((1, HY_FFN), lambda i, o: (0, 0)),
                  pl.BlockSpec((3 * HY_FFN, C), lambda i, o: (0, 2 * o + i // nL)),
                  pl.BlockSpec((1, C), lambda i, o: (0, 2 * o + i // nL))],
        out_specs=pl.BlockSpec((tm, C), lambda i, o: (i, o)),
        out_shape=jax.ShapeDtypeStruct((2 * L, HY_ORDER * C), F32),
        compiler_params=_cparams("parallel", "parallel"),
        name="hy_filter",
    )(jnp.asarray(zp), _stack3_rows(w1p), row(b1), row(freq1), _stack3_rows(w2), row(b2), row(freq2),
      _stack3_rows(w3), row(decay))


def _dft_constants(L):
    N = 2 * L
    N2 = DFT_N2
    N1 = N // N2
    K1 = N1 // 2 + 1
    k1 = np.arange(K1, dtype=np.float64)
    n1 = np.arange(N1, dtype=np.float64)
    ang = 2.0 * np.pi * np.outer(k1, n1) / N1
    a1 = np.stack([np.cos(ang), -np.sin(ang)], axis=1).reshape(2 * K1, N1)
    n1h = np.arange(N1 // 2, dtype=np.float64)
    angb = 2.0 * np.pi * np.outer(n1h, k1) / N1
    ck = np.where((np.arange(K1) == 0) | (np.arange(K1) == N1 // 2), 1.0, 2.0)[None, :]
    binv = np.stack([ck * np.cos(angb), -ck * np.sin(angb)], axis=2).reshape(N1 // 2, 2 * K1) / N
    n2 = np.arange(N2, dtype=np.float64)
    ph = -2.0 * np.pi * (np.outer(n2, n2) % N2) / N2
    dre, dim = np.cos(ph), np.sin(ph)
    dst = np.block([[dre, -dim], [dim, dre]])
    kron = lambda a: _stack3(np.kron(a, np.eye(SUBLANES)))
    return dict(N1=N1, K1=K1, a1_full=kron(a1), a1_half=kron(a1[:, :N1 // 2]), binv=kron(binv),
                d=_stack3(dst), dt=_stack3(dst.T))


def _twiddle(k1, n_total, width):
    n2 = _iota2((DFT_N2, LANES), 0).astype(F32)
    ang = (-2.0 * math.pi / n_total) * (k1.astype(F32) * n2)
    reps = width // LANES
    return jnp.tile(jnp.cos(ang), (1, reps)), jnp.tile(jnp.sin(ang), (1, reps))


def _inner_forward(d, y, k1, n_total):
    h = DFT_N2
    c, s = _twiddle(k1, n_total, y.shape[1])
    yr, yi = y[:h], y[h:]
    return _dot3(d, jnp.concatenate([yr * c - yi * s, yr * s + yi * c], axis=0))


def _dft_outer_body(a_ref, x_ref, o_ref):
    _, k, s, ct = x_ref.shape
    y = _dot3(a_ref[...], x_ref[0].reshape(k * s, ct))
    o_ref[0] = y.reshape(o_ref.shape[1], s, ct)


def dft_outer(a_kron, x, ct):
    B, K, N2, W = x.shape
    R = a_kron.shape[0] // SUBLANES
    ct = min(ct, W)
    return pl.pallas_call(
        _dft_outer_body,
        grid=(B, N2 // SUBLANES, W // ct),
        in_specs=[pl.BlockSpec(a_kron.shape, lambda b, s, j: (0, 0)),
                  pl.BlockSpec((1, K, SUBLANES, ct), lambda b, s, j: (b, 0, s, j))],
        out_specs=pl.BlockSpec((1, R, SUBLANES, ct), lambda b, s, j: (b, 0, s, j)),
        out_shape=jax.ShapeDtypeStruct((B, R, N2, W), F32),
        compiler_params=_cparams("parallel", "parallel", "parallel"),
        name="dft_outer",
    )(a_kron, x)


def _dft_inner_body(d_ref, y_ref, o_ref, *, n_total):
    o_ref[0, 0] = _inner_forward(d_ref[...], y_ref[0, 0], pl.program_id(0), n_total)


def dft_inner(d, y, ct, n_total):
    B, K1, R, W = y.shape
    return pl.pallas_call(
        functools.partial(_dft_inner_body, n_total=n_total),
        grid=(K1, B, W // ct),
        in_specs=[pl.BlockSpec(d.shape, lambda k, b, j: (0, 0)),
                  pl.BlockSpec((1, 1, R, ct), lambda k, b, j: (b, k, 0, j))],
        out_specs=pl.BlockSpec((1, 1, R, ct), lambda k, b, j: (b, k, 0, j)),
        out_shape=jax.ShapeDtypeStruct(y.shape, F32),
        compiler_params=_cparams("parallel", "parallel", "parallel"),
        name="dft_inner",
    )(d, y)


def _spec_mul_body(d_ref, dt_ref, y_ref, kf_ref, o_ref, *, n_total):
    k1 = pl.program_id(0)
    z = _inner_forward(d_ref[...], y_ref[0, 0], k1, n_total)
    h = DFT_N2
    zr, zi = z[:h], z[h:]
    kf = kf_ref[0, 0]
    kr, ki = kf[:h], kf[h:]
    q = _dot3(dt_ref[...], jnp.concatenate([zr * kr - zi * ki, zr * ki + zi * kr], axis=0))
    c, s = _twiddle(k1, n_total, q.shape[1])
    qr, qi = q[:h], q[h:]
    o_ref[0, 0] = jnp.concatenate([qr * c + qi * s, qi * c - qr * s], axis=0)


def spec_mul(d, dt, y, kf, order, C, ct, n_total):
    B, K1, R, _ = y.shape
    nc = C // ct
    return pl.pallas_call(
        functools.partial(_spec_mul_body, n_total=n_total),
        grid=(K1, B, nc),
        in_specs=[pl.BlockSpec(d.shape, lambda k, b, j: (0, 0)),
                  pl.BlockSpec(dt.shape, lambda k, b, j: (0, 0)),
                  pl.BlockSpec((1, 1, R, ct), lambda k, b, j: (b, k, 0, j)),
                  pl.BlockSpec((1, 1, R, ct), lambda k, b, j: (0, k, 0, order * nc + j))],
        out_specs=pl.BlockSpec((1, 1, R, ct), lambda k, b, j: (b, k, 0, j)),
        out_shape=jax.ShapeDtypeStruct(y.shape, F32),
        compiler_params=_cparams("parallel", "parallel", "parallel"),
        name="hy_spec_mul",
    )(d, dt, y, kf)


def _dft_out_body(a_ref, t_ref, u_ref, g_ref, s_ref, o_ref):
    _, k, s, ct = t_ref.shape
    y = _dot3(a_ref[...], t_ref[0].reshape(k * s, ct)).reshape(u_ref.shape[1], s, ct)
    o_ref[0] = g_ref[0] * (y + u_ref[0] * s_ref[...])


def dft_out(binv_kron, t, u, gate, skip):
    B, K, N2, C = t.shape
    R = u.shape[1]
    uspec = pl.BlockSpec((1, R, SUBLANES, C), lambda b, s: (b, 0, s, 0))
    return pl.pallas_call(
        _dft_out_body,
        grid=(B, N2 // SUBLANES),
        in_specs=[pl.BlockSpec(binv_kron.shape, lambda b, s: (0, 0)),
                  pl.BlockSpec((1, K, SUBLANES, C), lambda b, s: (b, 0, s, 0)),
                  uspec, uspec,
                  pl.BlockSpec((1, C), lambda b, s: (0, 0))],
        out_specs=uspec,
        out_shape=jax.ShapeDtypeStruct(u.shape, F32),
        compiler_params=_cparams("parallel", "parallel"),
        name="dft_out",
    )(binv_kron, t, u, gate, skip)


def hyena_mixer(p, C, conv_w, conv_b, f_w1, f_b1, f_freq1, f_w2, f_b2, f_freq2, f_w3, f_decay, skip):
    B, L, _ = p.shape
    cst = _dft_constants(L)
    N1, K1 = cst["N1"], cst["K1"]
    N2 = DFT_N2
    v, x1, x2 = hy_short_conv(p, conv_w, conv_b, C)
    kc = hy_filter(L, C, f_w1, f_b1, f_freq1, f_w2, f_b2, f_freq2, f_w3, f_decay)
    yf = dft_outer(cst["a1_full"], kc.reshape(1, N1, N2, HY_ORDER * C), C)
    kf = dft_inner(cst["d"], yf.reshape(1, K1, 2 * N2, HY_ORDER * C), C, 2 * L)

    def long_conv(u, order, gate):
        u4 = u.reshape(B, N1 // 2, N2, C)
        y1 = dft_outer(cst["a1_half"], u4, C)
        t = spec_mul(cst["d"], cst["dt"], y1.reshape(B, K1, 2 * N2, C), kf, order, C, C, 2 * L)
        out = dft_out(cst["binv"], t.reshape(B, 2 * K1, N2, C), u4, gate.reshape(B, N1 // 2, N2, C),
                      skip[order].reshape(1, C))
        return out.reshape(B, L, C)

    z = long_conv(v, 0, x1)
    return long_conv(z, 1, x2)


def _gla_chunk(q, k, v, lr3, up3, upb, st, reverse):
    C = GLA_CHUNK
    rr = _iota2((C, C), 0)
    cc = _iota2((C, C), 1)
    keep = (rr <= cc) if reverse else (rr >= cc)
    g = _log_sigmoid(_dot(lr3, up3) + upb) / GLA_TAU
    yield None
    bc = _mask_dot(keep.astype(BF16), g)
    yield None
    blast = bc[0:1] if reverse else bc[C - 1:C]
    q_in = (q * (GLA_DK ** -0.5) * jnp.exp(bc)).astype(BF16)
    k_in = (k * jnp.exp(-bc)).astype(BF16)
    k_end = (k * jnp.exp(blast - bc)).astype(BF16)
    vb = v.astype(BF16)
    att = _dot_nt(q_in, k_in)
    o_inter = _dot_nt(q_in, st.astype(BF16))
    st_new = st * jnp.exp(blast) + _dot_tn(vb, k_end)
    yield None
    o = _dot(jnp.where(keep, att, 0.0).astype(BF16), vb) + o_inter
    yield o, st_new


def _gla_body(qf_ref, kf_ref, vf_ref, lf_ref, qb_ref, kb_ref, vb_ref, lb_ref, up_ref, upb_ref,
              of_ref, ob_ref, st_ref, *, n_chunks):
    C = GLA_CHUNK

    @pl.when(pl.program_id(1) == 0)
    def _():
        st_ref[...] = jnp.zeros_like(st_ref)

    def chunk(step, carry):
        dirs = ((0, step, False, (qf_ref, kf_ref, vf_ref, lf_ref), of_ref),
                (1, n_chunks - 1 - step, True, (qb_ref, kb_ref, vb_ref, lb_ref), ob_ref))
        work = []
        for d, c, reverse, (q_ref, k_ref, v_ref, l_ref), o_ref in dirs:
            rows = pl.ds(pl.multiple_of(c * C, C), C)
            lr_hi, lr_lo = _split(l_ref[0, rows, :])
            lr3 = jnp.concatenate([lr_hi, lr_lo, lr_hi], axis=1)
            for h in range(GLA_HEADS):
                kl = slice(h * GLA_DK, (h + 1) * GLA_DK)
                vl = slice(h * GLA_DV, (h + 1) * GLA_DV)
                vals = (q_ref[0, rows, kl], k_ref[0, rows, kl], v_ref[0, rows, vl], lr3,
                        up_ref[d, :, kl], upb_ref[d, :, kl], st_ref[d, h])
                work.append((vals, reverse, o_ref, rows, vl, d, h))
        done = _interleave([_gla_chunk(*vals, reverse) for vals, reverse, *_ in work])
        for (o, st_new), (_, _, o_ref, rows, vl, d, h) in zip(done, work):
            o_ref[0, rows, vl] = o
            st_ref[d, h] = st_new
        return carry

    lax.fori_loop(0, n_chunks, chunk, 0)


def gla_scan(p, plr, up_pad, up_b, col0, tl=256):
    B, L, _ = p.shape
    H = GLA_HEADS
    nt = L // tl
    kw, vw = H * GLA_DK, H * GLA_DV
    qb = col0 * LANES // kw
    vb = (col0 * LANES + 2 * kw) // vw
    assert col0 * LANES % kw == 0 and (col0 * LANES + 2 * kw) % vw == 0
    specs = []
    for tmap in (lambda t: t, lambda t: nt - 1 - t):
        specs += [pl.BlockSpec((1, tl, kw), lambda b, t, tmap=tmap: (b, tmap(t), qb)),
                  pl.BlockSpec((1, tl, kw), lambda b, t, tmap=tmap: (b, tmap(t), qb + 1)),
                  pl.BlockSpec((1, tl, vw), lambda b, t, tmap=tmap: (b, tmap(t), vb)),
                  pl.BlockSpec((1, tl, LANES), lambda b, t, tmap=tmap: (b, tmap(t), 0))]
    specs += [pl.BlockSpec((2, 3 * LANES, kw), lambda b, t: (0, 0, 0)),
              pl.BlockSpec((2, 1, kw), lambda b, t: (0, 0, 0))]
    out = jax.ShapeDtypeStruct((B, L, vw), F32)
    return pl.pallas_call(
        functools.partial(_gla_body, n_chunks=tl // GLA_CHUNK),
        grid=(B, nt),
        in_specs=specs,
        out_specs=[pl.BlockSpec((1, tl, vw), lambda b, t: (b, t, 0)),
                   pl.BlockSpec((1, tl, vw), lambda b, t: (b, nt - 1 - t, 0))],
        out_shape=[out, out],
        scratch_shapes=[pltpu.VMEM((2, H, GLA_DV, GLA_DK), F32)],
        compiler_params=_cparams("parallel", "arbitrary"),
        name="gla_scan",
    )(p, p, p, plr, p, p, p, plr, up_pad, up_b)


def _gla_post_body(of_ref, ob_ref, r_ref, gn_ref, o_ref):
    for h in range(of_ref.shape[2] // GLA_DV):
        sl = slice(h * GLA_DV, (h + 1) * GLA_DV)
        o = of_ref[0, :, sl] + ob_ref[0, :, sl]
        o = o * lax.rsqrt(jnp.mean(o * o, axis=-1, keepdims=True) + NORM_EPS) * gn_ref[:, sl]
        r = r_ref[0, :, sl]
        o_ref[0, :, sl] = (o * (r * jax.nn.sigmoid(r))).astype(o_ref.dtype)


def gla_post(o_f, o_b, p, r_block, norm_g, tl=512):
    B, L, W = o_f.shape
    spec = pl.BlockSpec((1, tl, W), lambda b, t: (b, t, 0))
    return pl.pallas_call(
        _gla_post_body,
        grid=(B, L // tl),
        in_specs=[spec, spec,
                  pl.BlockSpec((1, tl, W), lambda b, t: (b, t, r_block)),
                  pl.BlockSpec((1, W), lambda b, t: (0, 0))],
        out_specs=spec,
        out_shape=jax.ShapeDtypeStruct((B, L, W), BF16),
        compiler_params=_cparams("parallel", "parallel"),
        name="gla_post",
    )(o_f, o_b, p, norm_g.reshape(1, W))


def gla_mixer(p, plr, col0, up, up_b, norm_g):
    H = GLA_HEADS
    up_pad = jnp.zeros((2, LANES, H * GLA_DK), F32)
    up_pad = up_pad.at[0, 0:GLA_RANK].set(up[0]).at[1, GLA_RANK:2 * GLA_RANK].set(up[1])
    up_hi, up_lo = _split_bits(up_pad)
    up3 = jnp.concatenate([up_hi, up_hi, up_lo], axis=1)
    o_f, o_b = gla_scan(p, plr, up3, up_b.reshape(2, 1, H * GLA_DK), col0)
    r_col = col0 * LANES + 2 * H * GLA_DK + H * GLA_DV
    assert r_col % (H * GLA_DV) == 0
    return gla_post(o_f, o_b, p, r_col // (H * GLA_DV), norm_g)


def _rw_mix_body(x_ref, xm_ref, xp_ref, g_ref, sc_ref, sh_ref, mu_ref, *o_refs, n_tiles):
    i = pl.program_id(1)
    g, sc, sh = g_ref[...], sc_ref[0], sh_ref[0]
    h = _norm_mod_value(x_ref[0], g, sc, sh)
    hm = _norm_mod_value(xm_ref[0], g, sc, sh)
    hp = _norm_mod_value(xp_ref[0], g, sc, sh)
    hm1, hp1 = _neighbours(h, hm, hp, i, n_tiles)
    xx = 0.5 * (hm1 + hp1) - h
    mu = mu_ref[...]
    for j, o_ref in enumerate(o_refs):
        o_ref[0] = (h + xx * mu[j:j + 1]).astype(o_ref.dtype)


def rw_mix(x, g, scale, shift, mu, tl=256):
    B, L, D = x.shape
    n_tiles = L // tl
    out = jax.ShapeDtypeStruct((B, L, D), BF16)
    ospec = pl.BlockSpec((1, tl, D), lambda b, i: (b, i, 0))
    mu8 = jnp.zeros((8, D), F32).at[:6].set(mu)
    return pl.pallas_call(
        functools.partial(_rw_mix_body, n_tiles=n_tiles),
        grid=(B, n_tiles),
        in_specs=_neighbour_specs(tl, D, L // 8) + [pl.BlockSpec((1, D), lambda b, i: (0, 0)),
                                                    pl.BlockSpec((1, 1, D), lambda b, i: (b, 0, 0)),
                                                    pl.BlockSpec((1, 1, D), lambda b, i: (b, 0, 0)),
                                                    pl.BlockSpec((8, D), lambda b, i: (0, 0))],
        out_specs=[ospec] * 6,
        out_shape=[out] * 6,
        compiler_params=_cparams("parallel", "parallel"),
        name="rw_mix",
    )(x, x, x, g.reshape(1, D), scale, shift, mu8)


def _head_ones():
    r = _iota2((LANES, LANES), 0) // RW_HEAD
    c = _iota2((LANES, LANES), 1) // RW_HEAD
    return (r == c).astype(BF16)


def _rw_prep_body(k_ref, r_ref, v_ref, a0_ref, a1_ref, kk_w_ref, ka_ref, rk_ref, kk_ref, bon_ref):
    ones = _head_ones()
    for j in range(k_ref.shape[1] // LANES):
        sl = slice(j * LANES, (j + 1) * LANES)
        f32 = lambda ref: ref[:, sl].astype(F32)
        k = f32(k_ref)
        kk = k * kk_w_ref[:, sl]
        ss = _dot_mask(kk * kk, ones)
        kk_ref[:, sl] = (kk * lax.rsqrt(jnp.maximum(ss, 1e-24))).astype(kk_ref.dtype)
        a_sum = jax.nn.sigmoid(f32(a0_ref)) + jax.nn.sigmoid(f32(a1_ref))
        kd_sum = k * (2.0 + (a_sum - 2.0) * ka_ref[:, sl])
        bonus = _dot_mask(f32(r_ref) * kd_sum * rk_ref[:, sl], ones) * f32(v_ref)
        bon_ref[:, sl] = bonus.astype(bon_ref.dtype)


def rw_prep(k, r, v, apre0, apre1, k_k, k_a, r_k, tm=256):
    T, D = k.shape
    tm = min(tm, T)
    spec = pl.BlockSpec((tm, D), lambda i: (i, 0))
    pspec = pl.BlockSpec((1, D), lambda i: (0, 0))
    out = jax.ShapeDtypeStruct((T, D), BF16)
    return pl.pallas_call(
        _rw_prep_body,
        grid=(T // tm,),
        in_specs=[spec, spec, spec, spec, spec, pspec, pspec, pspec],
        out_specs=[spec, spec],
        out_shape=[out, out],
        compiler_params=_cparams("parallel"),
        name="rw_prep",
    )(k, r, v, apre0, apre1, k_k.reshape(1, D), k_a.reshape(1, D), r_k.reshape(1, D))


def _rw_chunk(r, k, v, kk, wpre, apre, ka, h, reverse):
    C = RW_CHUNK
    P = 2 * C
    rc = _iota2((C, C), 0)
    cc = _iota2((C, C), 1)
    tri = ((rc <= cc) if reverse else (rc >= cc)).astype(BF16)
    t_idx = _iota2((C, P), 0)
    s_idx = _iota2((C, P), 1) % C
    strict = (t_idx < s_idx) if reverse else (t_idx > s_idx)
    incl = (t_idx <= s_idx) if reverse else (t_idx >= s_idx)
    head0 = _iota2((C, LANES), 1) < RW_HEAD
    same_head = (_iota2((LANES, LANES), 0) // RW_HEAD) == (_iota2((LANES, LANES), 1) // RW_HEAD)

    def stack(x):
        return jnp.concatenate([jnp.where(head0, x, 0.0), jnp.where(head0, 0.0, x)], axis=0).astype(BF16)

    lw = -RW_LOG_DECAY_MAX * jax.nn.sigmoid(wpre)
    a = jax.nn.sigmoid(apre)
    kd = k * (1.0 + (a - 1.0) * ka)
    b = kk * a
    cum = _mask_dot(tri, lw)
    yield None
    cum_x = cum - lw
    total = cum[0:1] if reverse else cum[C - 1:C]
    e_neg = jnp.exp(-cum)
    rn = (r * jnp.exp(cum)).astype(BF16)
    xn = (kk * jnp.exp(cum_x)).astype(BF16)
    kn = kd * e_neg
    bn = b * e_neg
    vs = stack(v)
    coef = _dot_nt(jnp.concatenate([xn, rn], axis=0), jnp.concatenate([stack(kn), stack(bn)], axis=0))
    yield None
    a_k = jnp.where(strict, coef[:C, :P], 0.0).astype(BF16)
    pw = jnp.where(strict, -coef[:C, P:], 0.0)
    q_k = jnp.where(incl, coef[C:, :P], 0.0).astype(BF16)
    q_b = jnp.where(incl, coef[C:, P:], 0.0).astype(BF16)
    hv = jnp.concatenate([h.astype(BF16), vs], axis=0)
    x = _dot(jnp.concatenate([xn, a_k], axis=1), hv)
    o_hv = _dot(jnp.concatenate([rn, q_k], axis=1), hv)
    yield None
    for _ in range(int(math.log2(C)) - 1):
        both = _dot(pw.astype(BF16), jnp.concatenate([stack(x), stack(pw)], axis=1))
        x = x + both[:, :P]
        pw = both[:, P:]
        yield None
    u = x + _dot(pw.astype(BF16), stack(x))
    yield None
    o = o_hv - _dot(q_b, stack(u))
    upd = _dot_tn(jnp.concatenate([kn, bn], axis=0).astype(BF16),
                  jnp.concatenate([v, -u], axis=0).astype(BF16))
    gcol = jnp.transpose(jnp.broadcast_to(jnp.exp(total), (LANES, LANES)))
    h_new = gcol * (h + jnp.where(same_head, upd, 0.0))
    yield o, h_new


def _interleave(chains):
    while True:
        outs = [next(ch) for ch in chains]
        if outs[0] is not None:
            return outs


def _rw_scan_body(rf_ref, kf_ref, vf_ref, kkf_ref, wf_ref, af_ref, rb_ref, kb_ref, vb_ref, kkb_ref, wb_ref,
                  ab_ref, ka_ref, of_ref, ob_ref, h_ref, *, n_chunks, n_pairs):
    C = RW_CHUNK

    @pl.when(pl.program_id(2) == 0)
    def _():
        h_ref[...] = jnp.zeros_like(h_ref)

    def chunk(step, carry):
        dirs = ((0, step, False, (rf_ref, kf_ref, vf_ref, kkf_ref, wf_ref, af_ref), of_ref),
                (1, n_chunks - 1 - step, True, (rb_ref, kb_ref, vb_ref, kkb_ref, wb_ref, ab_ref), ob_ref))
        work = []
        for d, c, reverse, refs, o_ref in dirs:
            rows = pl.ds(pl.multiple_of(c * C, C), C)
            for g in range(n_pairs):
                lanes = slice(g * LANES, (g + 1) * LANES)
                vals = [ref[0, rows, lanes].astype(F32) for ref in refs]
                work.append((vals, ka_ref[:, lanes], h_ref[d, g], reverse, o_ref, rows, lanes, d, g))
        done = _interleave([_rw_chunk(*vals, ka, h, reverse) for vals, ka, h, reverse, *_ in work])
        for (o, h_new), (_, _, _, _, o_ref, rows, lanes, d, g) in zip(done, work):
            o_ref[0, rows, lanes] = o
            h_ref[d, g] = h_new
        return carry

    lax.fori_loop(0, n_chunks, chunk, 0)


def rw_scan(r, k, v, kk, wpre, apre, k_a, tl=512, n_pairs=8):
    B, L, D = r.shape
    width = n_pairs * LANES
    nt = L // tl
    fspec = pl.BlockSpec((1, tl, width), lambda b, p, t: (b, t, p))
    bspec = pl.BlockSpec((1, tl, width), lambda b, p, t: (b, nt - 1 - t, p))
    out = jax.ShapeDtypeStruct((B, L, D), F32)
    return pl.pallas_call(
        functools.partial(_rw_scan_body, n_chunks=tl // RW_CHUNK, n_pairs=n_pairs),
        grid=(B, D // width, nt),
        in_specs=[fspec] * 6 + [bspec] * 6 + [pl.BlockSpec((1, width), lambda b, p, t: (0, p))],
        out_specs=[fspec, bspec],
        out_shape=[out, out],
        scratch_shapes=[pltpu.VMEM((2, n_pairs, LANES, LANES), F32)],
        compiler_params=_cparams("parallel", "parallel", "arbitrary"),
        name="rw_scan",
    )(r, k, v, kk, wpre[0], apre[0], r, k, v, kk, wpre[1], apre[1], k_a.reshape(1, D))


def _rw_post_body(of_ref, ob_ref, bon_ref, g_ref, gng_ref, gnb_ref, o_ref):
    ones = _head_ones()
    for j in range(of_ref.shape[1] // LANES):
        sl = slice(j * LANES, (j + 1) * LANES)
        so = of_ref[:, sl] + ob_ref[:, sl]
        mean = _dot_mask(so, ones) * (1.0 / RW_HEAD)
        dlt = so - mean
        var = _dot_mask(dlt * dlt, ones) * (1.0 / RW_HEAD)
        y = dlt * lax.rsqrt(var + RW_GN_EPS) * gng_ref[:, sl] + gnb_ref[:, sl]
        o_ref[:, sl] = ((y + bon_ref[:, sl].astype(F32)) * g_ref[:, sl].astype(F32)).astype(o_ref.dtype)


def rw_post(o_f, o_b, bonus, g, gn_g, gn_b, tm=256):
    T, D = o_f.shape
    tm = min(tm, T)
    spec = pl.BlockSpec((tm, D), lambda i: (i, 0))
    pspec = pl.BlockSpec((1, D), lambda i: (0, 0))
    return pl.pallas_call(
        _rw_post_body,
        grid=(T // tm,),
        in_specs=[spec, spec, spec, spec, pspec, pspec],
        out_specs=spec,
        out_shape=jax.ShapeDtypeStruct((T, D), BF16),
        compiler_params=_cparams("parallel"),
        name="rw_post",
    )(o_f, o_b, bonus, g, gn_g.reshape(1, D), gn_b.reshape(1, D))


def _pad_cols(w, n):
    return jnp.pad(w, ((0, 0), (0, n - w.shape[1])))


def _pad_rows(w, n):
    return jnp.pad(w, ((0, n - w.shape[0]), (0, 0)))


def rwkv7_block(x, gate, g_norm, scale, shift, mu, w_r, w_k, w_v, w0, w1, w2, a0, a1, a2, g1, g2,
                k_k, k_a, r_k, gn_g, gn_b, w_o):
    B, L, D = x.shape
    T = B * L
    xr, xw, xk, xv, xa, xg = [a.reshape(T, D) for a in rw_mix(x, g_norm, scale, shift, mu)]
    r = matmul(xr, w_r.astype(BF16), out_dtype=BF16)
    k = matmul(xk, w_k.astype(BF16), out_dtype=BF16)
    v = matmul(xv, w_v.astype(BF16), out_dtype=BF16)
    w1c = jnp.concatenate([_pad_cols(w1[0], LANES), _pad_cols(w1[1], LANES)], axis=1).astype(BF16)
    a1c = jnp.concatenate([_pad_cols(a1[0], LANES), _pad_cols(a1[1], LANES)], axis=1).astype(BF16)
    wmid = matmul(xw, w1c)
    amid = matmul(xa, a1c)
    gmid = matmul(xg, g1.astype(BF16))
    wpre = [matmul(wmid, _pad_rows(w2[d], LANES).astype(BF16), a_act="tanh", bias=w0[d], a_col_block=d)
            for d in range(2)]
    apre = [matmul(amid, _pad_rows(a2[d], LANES).astype(BF16), bias=a0[d], a_col_block=d, out_dtype=BF16)
            for d in range(2)]
    g = matmul(gmid, g2.astype(BF16), a_act="sigmoid", out_dtype=BF16)
    kk, bonus = rw_prep(k, r, v, apre[0], apre[1], k_k, k_a, r_k.reshape(D))
    sh = lambda a: a.reshape(B, L, D)
    o_f, o_b = rw_scan(sh(r), sh(k), sh(v), sh(kk), [sh(a) for a in wpre], [sh(a) for a in apre], k_a)
    y = rw_post(o_f.reshape(T, D), o_b.reshape(T, D), bonus, g, gn_g, gn_b)
    out = matmul(y, w_o.astype(BF16), res=x.reshape(T, D), gate=gate, rows_per_batch=L)
    return out.reshape(B, L, D)


def ffn_block(x, c, norm_g, ada_w, ada_b, idx, w1, w2):
    B, L, D = x.shape
    T = B * L
    shift, scale, gate = ada_mod(c, ada_w, ada_b, idx)
    h = norm_mod(x, norm_g, scale, shift).reshape(T, D)
    hid = matmul(h, w1.astype(BF16), w_idx=idx, out_dtype=BF16, epi="sqrelu")
    out = matmul(hid, w2.astype(BF16), w_idx=idx, res=x.reshape(T, D), gate=gate, rows_per_batch=L,
                 tm=1024, tn=256)
    return out.reshape(B, L, D)


def hyena_gla_block(x, gate, h, w_in, w_out, conv_w, conv_b, f_w1, f_b1, f_freq1, f_w2, f_b2, f_freq2,
                    f_w3, f_decay, hy_skip, gla_up, gla_up_b, gla_norm_g):
    B, L, D = x.shape
    T = B * L
    C = conv_w.shape[1] // 3
    n_main = w_in.shape[1] - 2 * GLA_RANK
    w_in = w_in.astype(BF16)
    p = matmul(h, w_in, n_cols=n_main).reshape(B, L, n_main)
    plr = matmul(h, _pad_cols(w_in[:, n_main:], LANES)).reshape(B, L, LANES)
    y_hy = hyena_mixer(p, C, conv_w, conv_b, f_w1, f_b1, f_freq1, f_w2, f_b2, f_freq2, f_w3, f_decay, hy_skip)
    y_gla = gla_mixer(p, plr, 3 * C // LANES, gla_up, gla_up_b, gla_norm_g)
    out = matmul(y_hy.reshape(T, -1), w_out.astype(BF16), a2=y_gla.reshape(T, -1), res=x.reshape(T, D),
                 gate=gate, rows_per_batch=L)
    return out.reshape(B, L, D)


def kernel(x, c, mix_norm_g, mix_ada_w, mix_ada_b, ab_w_in, ab_w_out, hy_conv_w, hy_conv_b, hy_ffn_w1, hy_ffn_b1, hy_freq1, hy_ffn_w2, hy_ffn_b2, hy_freq2, hy_ffn_w3, hy_decay, hy_skip, gla_up, gla_up_b, gla_norm_g, tm_norm_g, tm_ada_w, tm_ada_b, rw_mu, rw_w_r, rw_w_k, rw_w_v, rw_w0, rw_w1, rw_w2, rw_a0, rw_a1, rw_a2, rw_g1, rw_g2, rw_k_k, rw_k_a, rw_r_k, rw_gn_g, rw_gn_b, rw_w_o, ffn_norm_g, ffn_ada_w, ffn_ada_b, ffn_w1, ffn_w2, final_norm_g):
    B, L, D = x.shape
    depth = ffn_w1.shape[0]
    for layer in range(depth):
        i = layer // 2
        if layer % 2 == 0:
            shift, scale, gate = ada_mod(c, mix_ada_w, mix_ada_b, i)
            h = norm_mod(x, mix_norm_g[i], scale, shift).reshape(B * L, D)
            x = hyena_gla_block(x, gate, h, ab_w_in[i], ab_w_out[i], hy_conv_w[i], hy_conv_b[i],
                                hy_ffn_w1[i], hy_ffn_b1[i], hy_freq1[i], hy_ffn_w2[i], hy_ffn_b2[i],
                                hy_freq2[i], hy_ffn_w3[i], hy_decay[i], hy_skip[i],
                                gla_up[i], gla_up_b[i], gla_norm_g[i])
        else:
            shift, scale, gate = ada_mod(c, tm_ada_w, tm_ada_b, i)
            x = rwkv7_block(x, gate, tm_norm_g[i], scale, shift, rw_mu[i], rw_w_r[i], rw_w_k[i], rw_w_v[i],
                            rw_w0[i], rw_w1[i], rw_w2[i], rw_a0[i], rw_a1[i], rw_a2[i], rw_g1[i], rw_g2[i],
                            rw_k_k[i], rw_k_a[i], rw_r_k[i], rw_gn_g[i], rw_gn_b[i], rw_w_o[i])
        x = ffn_block(x, c, ffn_norm_g[layer], ffn_ada_w, ffn_ada_b, layer, ffn_w1, ffn_w2)
    return rms_final(x, final_norm_g)
```

```python
import functools
import math

import numpy as np
import jax
import jax.numpy as jnp
from jax import lax
from jax.experimental import pallas as pl
from jax.experimental.pallas import tpu as pltpu

F32 = jnp.float32
BF16 = jnp.bfloat16
HI = lax.Precision.HIGHEST

LANES = 128
SUBLANES = 8
NORM_EPS = 1e-6
GLA_HEADS = 4
GLA_DK = 128
GLA_DV = 256
GLA_RANK = 16
GLA_TAU = 16.0
GLA_CHUNK = 64
RW_HEAD = 64
RW_CHUNK = 64
RW_LOG_DECAY_MAX = 0.606531
RW_GN_EPS = 64e-5
HY_ORDER = 2
HY_EMB = 33
HY_FFN = 64
DFT_N2 = 128
VMEM_LIMIT = 52 * 1024 * 1024


def _cparams(*sem):
    return pltpu.CompilerParams(dimension_semantics=sem, vmem_limit_bytes=VMEM_LIMIT)


def _dot(a, b, precision=None):
    return jnp.dot(a, b, preferred_element_type=F32, precision=precision)


def _dot_nt(a, b, precision=None):
    return lax.dot_general(a, b, (((1,), (1,)), ((), ())), preferred_element_type=F32, precision=precision)


def _dot_tn(a, b, precision=None):
    return lax.dot_general(a, b, (((0,), (0,)), ((), ())), preferred_element_type=F32, precision=precision)


def _iota2(shape, axis):
    return lax.broadcasted_iota(jnp.int32, shape, axis)


def _log_sigmoid(x):
    return jnp.minimum(x, 0.0) - jnp.log(1.0 + jnp.exp(-jnp.abs(x)))


def _split(x):
    hi = x.astype(BF16)
    return hi, (x - hi.astype(F32)).astype(BF16)


def _mask_dot(mask_bf16, x):
    hi, lo = _split(x)
    n = x.shape[1]
    y = _dot(mask_bf16, jnp.concatenate([hi, lo], axis=1))
    return y[:, :n] + y[:, n:]


def _stack3(a):
    a = np.asarray(a, np.float32)
    hi = a.astype(BF16)
    lo = (a - hi.astype(np.float32)).astype(BF16)
    return jnp.asarray(np.concatenate([hi, hi, lo], axis=1))


def _dot3(a3, x):
    hi, lo = _split(x)
    return _dot(a3, jnp.concatenate([hi, lo, hi], axis=0))


def _split_bits(w):
    bits = lax.bitcast_convert_type(w, jnp.uint32) & jnp.uint32(0xFFFF0000)
    hi = lax.bitcast_convert_type(bits, F32)
    return hi.astype(BF16), (w - hi).astype(BF16)


def _stack3_rows(w):
    hi, lo = _split_bits(w)
    return jnp.concatenate([hi, hi, lo], axis=0)


def _dot3r(x, w3):
    hi, lo = _split(x)
    return _dot(jnp.concatenate([hi, lo, hi], axis=1), w3)


def _dot_mask(x, mask_bf16):
    hi, lo = _split(x)
    return _dot(hi, mask_bf16) + _dot(lo, mask_bf16)


def _ada_body(c_ref, w_ref, b_ref, o_ref):
    c = c_ref[...]
    s = c * jax.nn.sigmoid(c)
    o_ref[...] = _dot(s, w_ref[0], HI) + b_ref[0]


def ada_mod(c, w, b, idx):
    B, D = c.shape
    n3 = w.shape[2]
    tn = 512
    out = pl.pallas_call(
        _ada_body,
        grid=(n3 // tn,),
        in_specs=[pl.BlockSpec((B, D), lambda j: (0, 0)),
                  pl.BlockSpec((1, D, tn), lambda j: (idx, 0, j)),
                  pl.BlockSpec((1, 1, tn), lambda j: (idx, 0, j))],
        out_specs=pl.BlockSpec((B, tn), lambda j: (0, j)),
        out_shape=jax.ShapeDtypeStruct((B, n3), F32),
        compiler_params=_cparams("parallel"),
        name="ada_mod",
    )(c, w, b.reshape(b.shape[0], 1, n3))
    shift, scale, gate = jnp.split(out[:, None, :], 3, axis=-1)
    return shift, scale, gate


def _norm_mod_value(x, g, scale, shift):
    y = x * lax.rsqrt(jnp.mean(x * x, axis=-1, keepdims=True) + NORM_EPS)
    return (y * g) * (1.0 + scale) + shift


def _norm_mod_body(x_ref, g_ref, sc_ref, sh_ref, o_ref):
    o_ref[0] = _norm_mod_value(x_ref[0], g_ref[...], sc_ref[0], sh_ref[0]).astype(o_ref.dtype)


def norm_mod(x, g, scale, shift, out_dtype=BF16, tm=512):
    B, L, D = x.shape
    return pl.pallas_call(
        _norm_mod_body,
        grid=(B, L // tm),
        in_specs=[pl.BlockSpec((1, tm, D), lambda b, i: (b, i, 0)),
                  pl.BlockSpec((1, D), lambda b, i: (0, 0)),
                  pl.BlockSpec((1, 1, D), lambda b, i: (b, 0, 0)),
                  pl.BlockSpec((1, 1, D), lambda b, i: (b, 0, 0))],
        out_specs=pl.BlockSpec((1, tm, D), lambda b, i: (b, i, 0)),
        out_shape=jax.ShapeDtypeStruct((B, L, D), out_dtype),
        compiler_params=_cparams("parallel", "parallel"),
        name="norm_mod",
    )(x, g.reshape(1, D), scale, shift)


def _rms_body(x_ref, g_ref, o_ref):
    x = x_ref[0]
    y = x * lax.rsqrt(jnp.mean(x * x, axis=-1, keepdims=True) + NORM_EPS)
    o_ref[0] = y * g_ref[...]


def rms_final(x, g, tm=512):
    B, L, D = x.shape
    return pl.pallas_call(
        _rms_body,
        grid=(B, L // tm),
        in_specs=[pl.BlockSpec((1, tm, D), lambda b, i: (b, i, 0)),
                  pl.BlockSpec((1, D), lambda b, i: (0, 0))],
        out_specs=pl.BlockSpec((1, tm, D), lambda b, i: (b, i, 0)),
        out_shape=jax.ShapeDtypeStruct((B, L, D), F32),
        compiler_params=_cparams("parallel", "parallel"),
        name="rms_final",
    )(x, g.reshape(1, D))


def _mm_body(*refs, nk, a_act, epi, has_a2, has_bias, has_res):
    it = iter(refs)
    a_ref = next(it)
    w_ref = next(it)
    a2_ref = next(it) if has_a2 else None
    w2_ref = next(it) if has_a2 else None
    bias_ref = next(it) if has_bias else None
    res_ref = next(it) if has_res else None
    gate_ref = next(it) if has_res else None
    o_ref = next(it)
    acc_ref = next(it) if nk > 1 else None

    a = a_ref[...]
    if a_act == "tanh":
        a = jnp.tanh(a.astype(F32))
    elif a_act == "sigmoid":
        a = jax.nn.sigmoid(a.astype(F32))
    part = _dot(a.astype(BF16), w_ref[0])
    if has_a2:
        part = part + _dot(a2_ref[...].astype(BF16), w2_ref[0])

    def finish(acc):
        if has_bias:
            acc = acc + bias_ref[...]
        if epi == "sqrelu":
            acc = jnp.square(jnp.maximum(acc, 0.0))
        if has_res:
            acc = res_ref[...] + gate_ref[0] * acc
        o_ref[...] = acc.astype(o_ref.dtype)

    if nk == 1:
        finish(part)
    else:
        k = pl.program_id(2)

        @pl.when(k == 0)
        def _():
            acc_ref[...] = part

        @pl.when(k > 0)
        def _():
            acc_ref[...] += part

        @pl.when(k == nk - 1)
        def _():
            finish(acc_ref[...])


def matmul(a, w, *, a2=None, w_idx=0, n_cols=None, out_dtype=F32, a_act=None, epi=None, bias=None, res=None,
           gate=None, rows_per_batch=None, tm=1024, tn=1024, tk=None, a_col_block=0, cols_outer=False):
    M = a.shape[0]
    if w.ndim == 2:
        w = w[None]
    K = w.shape[1]
    N = w.shape[2] if n_cols is None else n_cols
    has_a2 = a2 is not None
    ka = K // 2 if has_a2 else K
    tm = min(tm, M if rows_per_batch is None else rows_per_batch)
    tn = min(tn, N)
    tk = ka if tk is None else min(tk, ka)
    nk = ka // tk
    assert M % tm == 0 and N % tn == 0 and ka % tk == 0 and not (has_a2 and nk > 1)
    has_bias = bias is not None
    has_res = res is not None

    def spec(shape, index):
        if cols_outer:
            return pl.BlockSpec(shape, lambda j, i, k: index(i, j, k))
        return pl.BlockSpec(shape, index)

    in_specs = [spec((tm, tk), lambda i, j, k: (i, k + a_col_block * nk)),
                spec((1, tk, tn), lambda i, j, k: (w_idx, k, j))]
    args = [a, w]
    if has_a2:
        in_specs += [spec((tm, tk), lambda i, j, k: (i, 0)),
                     spec((1, tk, tn), lambda i, j, k: (w_idx, 1, j))]
        args += [a2, w]
    if has_bias:
        in_specs.append(spec((1, tn), lambda i, j, k: (0, j)))
        args.append(bias.reshape(1, N).astype(F32))
    if has_res:
        tiles_per_batch = rows_per_batch // tm
        assert rows_per_batch % tm == 0
        in_specs.append(spec((tm, tn), lambda i, j, k: (i, j)))
        in_specs.append(spec((1, 1, tn), lambda i, j, k: (i // tiles_per_batch, 0, j)))
        args += [res, gate]
    scratch = [pltpu.VMEM((tm, tn), F32)] if nk > 1 else []
    body = functools.partial(_mm_body, nk=nk, a_act=a_act, epi=epi, has_a2=has_a2, has_bias=has_bias,
                             has_res=has_res)
    return pl.pallas_call(
        body,
        grid=(N // tn, M // tm, nk) if cols_outer else (M // tm, N // tn, nk),
        in_specs=in_specs,
        out_specs=spec((tm, tn), lambda i, j, k: (i, j)),
        out_shape=jax.ShapeDtypeStruct((M, N), out_dtype),
        scratch_shapes=scratch,
        compiler_params=_cparams("parallel", "parallel", "arbitrary"),
        name="matmul",
    )(*args)


def _neighbours(cur, prev8, next8, i, n_tiles):
    t = cur.shape[0]
    row = _iota2(cur.shape, 0)
    before = jnp.where(i > 0, prev8[7:8, :], 0.0)
    after = jnp.where(i < n_tiles - 1, next8[0:1, :], 0.0)
    xm1 = jnp.where(row == 0, before, pltpu.roll(cur, 1, 0))
    xp1 = jnp.where(row == t - 1, after, pltpu.roll(cur, t - 1, 0))
    return xm1, xp1


def _neighbour_specs(tl, width, n_rows8, col_map=None):
    r8 = tl // 8
    return [pl.BlockSpec((1, tl, width), lambda b, i: (b, i, 0)),
            pl.BlockSpec((1, 8, width), lambda b, i: (b, jnp.maximum(i * r8 - 1, 0), 0)),
            pl.BlockSpec((1, 8, width), lambda b, i: (b, jnp.minimum((i + 1) * r8, n_rows8 - 1), 0))]


def _hy_conv_body(p_ref, pm_ref, pp_ref, w_ref, b_ref, v_ref, x1_ref, x2_ref, *, n_tiles, C):
    i = pl.program_id(1)
    cur = p_ref[0]
    xm1, xp1 = _neighbours(cur, pm_ref[0], pp_ref[0], i, n_tiles)
    w = w_ref[...]
    u = xm1 * w[0:1] + cur * w[1:2] + xp1 * w[2:3] + b_ref[...]
    v_ref[0] = u[:, 0:C]
    x1_ref[0] = u[:, C:2 * C]
    x2_ref[0] = u[:, 2 * C:3 * C]


def hy_short_conv(p, conv_w, conv_b, C, tl=256):
    B, L, _ = p.shape
    W = 3 * C
    n_tiles = L // tl
    out = jax.ShapeDtypeStruct((B, L, C), F32)
    ospec = pl.BlockSpec((1, tl, C), lambda b, i: (b, i, 0))
    return pl.pallas_call(
        functools.partial(_hy_conv_body, n_tiles=n_tiles, C=C),
        grid=(B, n_tiles),
        in_specs=_neighbour_specs(tl, W, L // 8) + [pl.BlockSpec((3, W), lambda b, i: (0, 0)),
                                                    pl.BlockSpec((1, W), lambda b, i: (0, 0))],
        out_specs=[ospec, ospec, ospec],
        out_shape=[out, out, out],
        compiler_params=_cparams("parallel", "parallel"),
        name="hy_short_conv",
    )(p, p, p, conv_w, conv_b.reshape(1, W))


def _hy_filter_body(z_ref, w1_ref, b1_ref, f1_ref, w2_ref, b2_ref, f2_ref, w3_ref, dec_ref, o_ref, *, L, tm):
    i = pl.program_id(0)
    z = z_ref[...]
    f = jnp.sin(f1_ref[...] * (_dot3r(z, w1_ref[...]) + b1_ref[...]))
    f = jnp.sin(f2_ref[...] * (_dot3r(f, w2_ref[...]) + b2_ref[...]))
    t = z[:, 0:1]
    f = _dot3r(f, w3_ref[...]) * jnp.exp(-t * jnp.abs(dec_ref[...]))
    n = i * tm + _iota2(f.shape, 0)
    o_ref[...] = jnp.where(n == L, 0.0, f)


def hy_filter(L, C, w1, b1, freq1, w2, b2, freq2, w3, decay, tm=512):
    pos = np.concatenate([np.arange(L), [0], np.arange(L - 1, 0, -1)]).astype(np.float64)
    t = pos / (L - 1)
    ang = 2.0 * math.pi * pos / L
    nb = (HY_EMB - 1) // 2
    ba = np.linspace(1e-4, nb - 1, nb)[None, :] * ang[:, None]
    z = np.concatenate([t[:, None], np.cos(ba), -np.sin(ba)], axis=-1)
    zp = np.zeros((2 * L, 2 * HY_FFN), np.float32)
    zp[:, :HY_EMB] = z
    w1p = jnp.zeros((2 * HY_FFN, HY_FFN), F32).at[:HY_EMB].set(w1)
    nL = L // tm
    row = lambda a: a.reshape(1, -1)
    return pl.pallas_call(
        functools.partial(_hy_filter_body, L=L, tm=tm),
        grid=(2 * L // tm, HY_ORDER),
        in_specs=[pl.BlockSpec((tm, 2 * HY_FFN), lambda i, o: (i, 0)),
                  pl.BlockSpec((6 * HY_FFN, HY_FFN), lambda i, o: (0, 0)),
                  pl.BlockSpec((1, HY_FFN), lambda i, o: (0, 0)),
                  pl.BlockSpec((1, HY_FFN), lambda i, o: (0, 0)),
                  pl.BlockSpec((3 * HY_FFN, HY_FFN), lambda i, o: (0, 0)),
                  pl.BlockSpec((1, HY_FFN), lambda i, o: (0, 0)),
                  pl.BlockSpec((1, HY_FFN), lambda i, o: (0, 0)),
                  pl.BlockSpec((3 * HY_FFN, C), lambda i, o: (0, 2 * o + i // nL)),
                  pl.BlockSpec((1, C), lambda i, o: (0, 2 * o + i // nL))],
        out_specs=pl.BlockSpec((tm, C), lambda i, o: (i, o)),
        out_shape=jax.ShapeDtypeStruct((2 * L, HY_ORDER * C), F32),
        compiler_params=_cparams("parallel", "parallel"),
        name="hy_filter",
    )(jnp.asarray(zp), _stack3_rows(w1p), row(b1), row(freq1), _stack3_rows(w2), row(b2), row(freq2),
      _stack3_rows(w3), row(decay))


def _dft_constants(L):
    N = 2 * L
    N2 = DFT_N2
    N1 = N // N2
    K1 = N1 // 2 + 1
    k1 = np.arange(K1, dtype=np.float64)
    n1 = np.arange(N1, dtype=np.float64)
    ang = 2.0 * np.pi * np.outer(k1, n1) / N1
    a1 = np.stack([np.cos(ang), -np.sin(ang)], axis=1).reshape(2 * K1, N1)
    n1h = np.arange(N1 // 2, dtype=np.float64)
    angb = 2.0 * np.pi * np.outer(n1h, k1) / N1
    ck = np.where((np.arange(K1) == 0) | (np.arange(K1) == N1 // 2), 1.0, 2.0)[None, :]
    binv = np.stack([ck * np.cos(angb), -ck * np.sin(angb)], axis=2).reshape(N1 // 2, 2 * K1) / N
    n2 = np.arange(N2, dtype=np.float64)
    ph = -2.0 * np.pi * (np.outer(n2, n2) % N2) / N2
    dre, dim = np.cos(ph), np.sin(ph)
    dst = np.block([[dre, -dim], [dim, dre]])
    kron = lambda a: _stack3(np.kron(a, np.eye(SUBLANES)))
    return dict(N1=N1, K1=K1, a1_full=kron(a1), a1_half=kron(a1[:, :N1 // 2]), binv=kron(binv),
                d=_stack3(dst), dt=_stack3(dst.T))


def _twiddle(k1, n_total, width):
    n2 = _iota2((DFT_N2, LANES), 0).astype(F32)
    ang = (-2.0 * math.pi / n_total) * (k1.astype(F32) * n2)
    reps = width // LANES
    return jnp.tile(jnp.cos(ang), (1, reps)), jnp.tile(jnp.sin(ang), (1, reps))


def _inner_forward(d, y, k1, n_total):
    h = DFT_N2
    c, s = _twiddle(k1, n_total, y.shape[1])
    yr, yi = y[:h], y[h:]
    return _dot3(d, jnp.concatenate([yr * c - yi * s, yr * s + yi * c], axis=0))


def _dft_outer_body(a_ref, x_ref, o_ref):
    _, k, s, ct = x_ref.shape
    y = _dot3(a_ref[...], x_ref[0].reshape(k * s, ct))
    o_ref[0] = y.reshape(o_ref.shape[1], s, ct)


def dft_outer(a_kron, x, ct):
    B, K, N2, W = x.shape
    R = a_kron.shape[0] // SUBLANES
    ct = min(ct, W)
    return pl.pallas_call(
        _dft_outer_body,
        grid=(B, N2 // SUBLANES, W // ct),
        in_specs=[pl.BlockSpec(a_kron.shape, lambda b, s, j: (0, 0)),
                  pl.BlockSpec((1, K, SUBLANES, ct), lambda b, s, j: (b, 0, s, j))],
        out_specs=pl.BlockSpec((1, R, SUBLANES, ct), lambda b, s, j: (b, 0, s, j)),
        out_shape=jax.ShapeDtypeStruct((B, R, N2, W), F32),
        compiler_params=_cparams("parallel", "parallel", "parallel"),
        name="dft_outer",
    )(a_kron, x)


def _dft_inner_body(d_ref, y_ref, o_ref, *, n_total):
    o_ref[0, 0] = _inner_forward(d_ref[...], y_ref[0, 0], pl.program_id(0), n_total)


def dft_inner(d, y, ct, n_total):
    B, K1, R, W = y.shape
    return pl.pallas_call(
        functools.partial(_dft_inner_body, n_total=n_total),
        grid=(K1, B, W // ct),
        in_specs=[pl.BlockSpec(d.shape, lambda k, b, j: (0, 0)),
                  pl.BlockSpec((1, 1, R, ct), lambda k, b, j: (b, k, 0, j))],
        out_specs=pl.BlockSpec((1, 1, R, ct), lambda k, b, j: (b, k, 0, j)),
        out_shape=jax.ShapeDtypeStruct(y.shape, F32),
        compiler_params=_cparams("parallel", "parallel", "parallel"),
        name="dft_inner",
    )(d, y)


def _spec_mul_body(d_ref, dt_ref, y_ref, kf_ref, o_ref, *, n_total):
    k1 = pl.program_id(0)
    z = _inner_forward(d_ref[...], y_ref[0, 0], k1, n_total)
    h = DFT_N2
    zr, zi = z[:h], z[h:]
    kf = kf_ref[0, 0]
    kr, ki = kf[:h], kf[h:]
    q = _dot3(dt_ref[...], jnp.concatenate([zr * kr - zi * ki, zr * ki + zi * kr], axis=0))
    c, s = _twiddle(k1, n_total, q.shape[1])
    qr, qi = q[:h], q[h:]
    o_ref[0, 0] = jnp.concatenate([qr * c + qi * s, qi * c - qr * s], axis=0)


def spec_mul(d, dt, y, kf, order, C, ct, n_total):
    B, K1, R, _ = y.shape
    nc = C // ct
    return pl.pallas_call(
        functools.partial(_spec_mul_body, n_total=n_total),
        grid=(K1, B, nc),
        in_specs=[pl.BlockSpec(d.shape, lambda k, b, j: (0, 0)),
                  pl.BlockSpec(dt.shape, lambda k, b, j: (0, 0)),
                  pl.BlockSpec((1, 1, R, ct), lambda k, b, j: (b, k, 0, j)),
                  pl.BlockSpec((1, 1, R, ct), lambda k, b, j: (0, k, 0, order * nc + j))],
        out_specs=pl.BlockSpec((1, 1, R, ct), lambda k, b, j: (b, k, 0, j)),
        out_shape=jax.ShapeDtypeStruct(y.shape, F32),
        compiler_params=_cparams("parallel", "parallel", "parallel"),
        name="hy_spec_mul",
    )(d, dt, y, kf)


def _dft_out_body(a_ref, t_ref, u_ref, g_ref, s_ref, o_ref):
    _, k, s, ct = t_ref.shape
    y = _dot3(a_ref[...], t_ref[0].reshape(k * s, ct)).reshape(u_ref.shape[1], s, ct)
    o_ref[0] = g_ref[0] * (y + u_ref[0] * s_ref[...])


def dft_out(binv_kron, t, u, gate, skip):
    B, K, N2, C = t.shape
    R = u.shape[1]
    uspec = pl.BlockSpec((1, R, SUBLANES, C), lambda b, s: (b, 0, s, 0))
    return pl.pallas_call(
        _dft_out_body,
        grid=(B, N2 // SUBLANES),
        in_specs=[pl.BlockSpec(binv_kron.shape, lambda b, s: (0, 0)),
                  pl.BlockSpec((1, K, SUBLANES, C), lambda b, s: (b, 0, s, 0)),
                  uspec, uspec,
                  pl.BlockSpec((1, C), lambda b, s: (0, 0))],
        out_specs=uspec,
        out_shape=jax.ShapeDtypeStruct(u.shape, F32),
        compiler_params=_cparams("parallel", "parallel"),
        name="dft_out",
    )(binv_kron, t, u, gate, skip)


def hyena_mixer(p, C, conv_w, conv_b, f_w1, f_b1, f_freq1, f_w2, f_b2, f_freq2, f_w3, f_decay, skip):
    B, L, _ = p.shape
    cst = _dft_constants(L)
    N1, K1 = cst["N1"], cst["K1"]
    N2 = DFT_N2
    v, x1, x2 = hy_short_conv(p, conv_w, conv_b, C)
    kc = hy_filter(L, C, f_w1, f_b1, f_freq1, f_w2, f_b2, f_freq2, f_w3, f_decay)
    yf = dft_outer(cst["a1_full"], kc.reshape(1, N1, N2, HY_ORDER * C), C)
    kf = dft_inner(cst["d"], yf.reshape(1, K1, 2 * N2, HY_ORDER * C), C, 2 * L)

    def long_conv(u, order, gate):
        u4 = u.reshape(B, N1 // 2, N2, C)
        y1 = dft_outer(cst["a1_half"], u4, C)
        t = spec_mul(cst["d"], cst["dt"], y1.reshape(B, K1, 2 * N2, C), kf, order, C, C, 2 * L)
        out = dft_out(cst["binv"], t.reshape(B, 2 * K1, N2, C), u4, gate.reshape(B, N1 // 2, N2, C),
                      skip[order].reshape(1, C))
        return out.reshape(B, L, C)

    z = long_conv(v, 0, x1)
    return long_conv(z, 1, x2)


def _gla_chunk(q, k, v, lr3, up3, upb, st, reverse):
    C = GLA_CHUNK
    rr = _iota2((C, C), 0)
    cc = _iota2((C, C), 1)
    keep = (rr <= cc) if reverse else (rr >= cc)
    g = _log_sigmoid(_dot(lr3, up3) + upb) / GLA_TAU
    yield None
    bc = _mask_dot(keep.astype(BF16), g)
    yield None
    blast = bc[0:1] if reverse else bc[C - 1:C]
    q_in = (q * (GLA_DK ** -0.5) * jnp.exp(bc)).astype(BF16)
    k_in = (k * jnp.exp(-bc)).astype(BF16)
    k_end = (k * jnp.exp(blast - bc)).astype(BF16)
    vb = v.astype(BF16)
    att = _dot_nt(q_in, k_in)
    o_inter = _dot_nt(q_in, st.astype(BF16))
    st_new = st * jnp.exp(blast) + _dot_tn(vb, k_end)
    yield None
    o = _dot(jnp.where(keep, att, 0.0).astype(BF16), vb) + o_inter
    yield o, st_new


def _gla_body(qf_ref, kf_ref, vf_ref, lf_ref, qb_ref, kb_ref, vb_ref, lb_ref, up_ref, upb_ref,
              of_ref, ob_ref, st_ref, *, n_chunks):
    C = GLA_CHUNK

    @pl.when(pl.program_id(1) == 0)
    def _():
        st_ref[...] = jnp.zeros_like(st_ref)

    def chunk(step, carry):
        dirs = ((0, step, False, (qf_ref, kf_ref, vf_ref, lf_ref), of_ref),
                (1, n_chunks - 1 - step, True, (qb_ref, kb_ref, vb_ref, lb_ref), ob_ref))
        work = []
        for d, c, reverse, (q_ref, k_ref, v_ref, l_ref), o_ref in dirs:
            rows = pl.ds(pl.multiple_of(c * C, C), C)
            lr_hi, lr_lo = _split(l_ref[0, rows, :])
            lr3 = jnp.concatenate([lr_hi, lr_lo, lr_hi], axis=1)
            for h in range(GLA_HEADS):
                kl = slice(h * GLA_DK, (h + 1) * GLA_DK)
                vl = slice(h * GLA_DV, (h + 1) * GLA_DV)
                vals = (q_ref[0, rows, kl], k_ref[0, rows, kl], v_ref[0, rows, vl], lr3,
                        up_ref[d, :, kl], upb_ref[d, :, kl], st_ref[d, h])
                work.append((vals, reverse, o_ref, rows, vl, d, h))
        done = _interleave([_gla_chunk(*vals, reverse) for vals, reverse, *_ in work])
        for (o, st_new), (_, _, o_ref, rows, vl, d, h) in zip(done, work):
            o_ref[0, rows, vl] = o
            st_ref[d, h] = st_new
        return carry

    lax.fori_loop(0, n_chunks, chunk, 0)


def gla_scan(p, plr, up_pad, up_b, col0, tl=256):
    B, L, _ = p.shape
    H = GLA_HEADS
    nt = L // tl
    kw, vw = H * GLA_DK, H * GLA_DV
    qb = col0 * LANES // kw
    vb = (col0 * LANES + 2 * kw) // vw
    assert col0 * LANES % kw == 0 and (col0 * LANES + 2 * kw) % vw == 0
    specs = []
    for tmap in (lambda t: t, lambda t: nt - 1 - t):
        specs += [pl.BlockSpec((1, tl, kw), lambda b, t, tmap=tmap: (b, tmap(t), qb)),
                  pl.BlockSpec((1, tl, kw), lambda b, t, tmap=tmap: (b, tmap(t), qb + 1)),
                  pl.BlockSpec((1, tl, vw), lambda b, t, tmap=tmap: (b, tmap(t), vb)),
                  pl.BlockSpec((1, tl, LANES), lambda b, t, tmap=tmap: (b, tmap(t), 0))]
    specs += [pl.BlockSpec((2, 3 * LANES, kw), lambda b, t: (0, 0, 0)),
              pl.BlockSpec((2, 1, kw), lambda b, t: (0, 0, 0))]
    out = jax.ShapeDtypeStruct((B, L, vw), F32)
    return pl.pallas_call(
        functools.partial(_gla_body, n_chunks=tl // GLA_CHUNK),
        grid=(B, nt),
        in_specs=specs,
        out_specs=[pl.BlockSpec((1, tl, vw), lambda b, t: (b, t, 0)),
                   pl.BlockSpec((1, tl, vw), lambda b, t: (b, nt - 1 - t, 0))],
        out_shape=[out, out],
        scratch_shapes=[pltpu.VMEM((2, H, GLA_DV, GLA_DK), F32)],
        compiler_params=_cparams("parallel", "arbitrary"),
        name="gla_scan",
    )(p, p, p, plr, p, p, p, plr, up_pad, up_b)


def _gla_post_body(of_ref, ob_ref, r_ref, gn_ref, o_ref):
    for h in range(of_ref.shape[2] // GLA_DV):
        sl = slice(h * GLA_DV, (h + 1) * GLA_DV)
        o = of_ref[0, :, sl] + ob_ref[0, :, sl]
        o = o * lax.rsqrt(jnp.mean(o * o, axis=-1, keepdims=True) + NORM_EPS) * gn_ref[:, sl]
        r = r_ref[0, :, sl]
        o_ref[0, :, sl] = (o * (r * jax.nn.sigmoid(r))).astype(o_ref.dtype)


def gla_post(o_f, o_b, p, r_block, norm_g, tl=512):
    B, L, W = o_f.shape
    spec = pl.BlockSpec((1, tl, W), lambda b, t: (b, t, 0))
    return pl.pallas_call(
        _gla_post_body,
        grid=(B, L // tl),
        in_specs=[spec, spec,
                  pl.BlockSpec((1, tl, W), lambda b, t: (b, t, r_block)),
                  pl.BlockSpec((1, W), lambda b, t: (0, 0))],
        out_specs=spec,
        out_shape=jax.ShapeDtypeStruct((B, L, W), BF16),
        compiler_params=_cparams("parallel", "parallel"),
        name="gla_post",
    )(o_f, o_b, p, norm_g.reshape(1, W))


def gla_mixer(p, plr, col0, up, up_b, norm_g):
    H = GLA_HEADS
    up_pad = jnp.zeros((2, LANES, H * GLA_DK), F32)
    up_pad = up_pad.at[0, 0:GLA_RANK].set(up[0]).at[1, GLA_RANK:2 * GLA_RANK].set(up[1])
    up_hi, up_lo = _split_bits(up_pad)
    up3 = jnp.concatenate([up_hi, up_hi, up_lo], axis=1)
    o_f, o_b = gla_scan(p, plr, up3, up_b.reshape(2, 1, H * GLA_DK), col0)
    r_col = col0 * LANES + 2 * H * GLA_DK + H * GLA_DV
    assert r_col % (H * GLA_DV) == 0
    return gla_post(o_f, o_b, p, r_col // (H * GLA_DV), norm_g)


def _rw_mix_body(x_ref, xm_ref, xp_ref, g_ref, sc_ref, sh_ref, mu_ref, *o_refs, n_tiles):
    i = pl.program_id(1)
    g, sc, sh = g_ref[...], sc_ref[0], sh_ref[0]
    h = _norm_mod_value(x_ref[0], g, sc, sh)
    hm = _norm_mod_value(xm_ref[0], g, sc, sh)
    hp = _norm_mod_value(xp_ref[0], g, sc, sh)
    hm1, hp1 = _neighbours(h, hm, hp, i, n_tiles)
    xx = 0.5 * (hm1 + hp1) - h
    mu = mu_ref[...]
    for j, o_ref in enumerate(o_refs):
        o_ref[0] = (h + xx * mu[j:j + 1]).astype(o_ref.dtype)


def rw_mix(x, g, scale, shift, mu, tl=256):
    B, L, D = x.shape
    n_tiles = L // tl
    out = jax.ShapeDtypeStruct((B, L, D), BF16)
    ospec = pl.BlockSpec((1, tl, D), lambda b, i: (b, i, 0))
    mu8 = jnp.zeros((8, D), F32).at[:6].set(mu)
    return pl.pallas_call(
        functools.partial(_rw_mix_body, n_tiles=n_tiles),
        grid=(B, n_tiles),
        in_specs=_neighbour_specs(tl, D, L // 8) + [pl.BlockSpec((1, D), lambda b, i: (0, 0)),
                                                    pl.BlockSpec((1, 1, D), lambda b, i: (b, 0, 0)),
                                                    pl.BlockSpec((1, 1, D), lambda b, i: (b, 0, 0)),
                                                    pl.BlockSpec((8, D), lambda b, i: (0, 0))],
        out_specs=[ospec] * 6,
        out_shape=[out] * 6,
        compiler_params=_cparams("parallel", "parallel"),
        name="rw_mix",
    )(x, x, x, g.reshape(1, D), scale, shift, mu8)


def _head_ones():
    r = _iota2((LANES, LANES), 0) // RW_HEAD
    c = _iota2((LANES, LANES), 1) // RW_HEAD
    return (r == c).astype(BF16)


def _rw_prep_body(k_ref, r_ref, v_ref, a0_ref, a1_ref, kk_w_ref, ka_ref, rk_ref, kk_ref, bon_ref):
    ones = _head_ones()
    for j in range(k_ref.shape[1] // LANES):
        sl = slice(j * LANES, (j + 1) * LANES)
        f32 = lambda ref: ref[:, sl].astype(F32)
        k = f32(k_ref)
        kk = k * kk_w_ref[:, sl]
        ss = _dot_mask(kk * kk, ones)
        kk_ref[:, sl] = (kk * lax.rsqrt(jnp.maximum(ss, 1e-24))).astype(kk_ref.dtype)
        a_sum = jax.nn.sigmoid(f32(a0_ref)) + jax.nn.sigmoid(f32(a1_ref))
        kd_sum = k * (2.0 + (a_sum - 2.0) * ka_ref[:, sl])
        bonus = _dot_mask(f32(r_ref) * kd_sum * rk_ref[:, sl], ones) * f32(v_ref)
        bon_ref[:, sl] = bonus.astype(bon_ref.dtype)


def rw_prep(k, r, v, apre0, apre1, k_k, k_a, r_k, tm=256):
    T, D = k.shape
    tm = min(tm, T)
    spec = pl.BlockSpec((tm, D), lambda i: (i, 0))
    pspec = pl.BlockSpec((1, D), lambda i: (0, 0))
    out = jax.ShapeDtypeStruct((T, D), BF16)
    return pl.pallas_call(
        _rw_prep_body,
        grid=(T // tm,),
        in_specs=[spec, spec, spec, spec, spec, pspec, pspec, pspec],
        out_specs=[spec, spec],
        out_shape=[out, out],
        compiler_params=_cparams("parallel"),
        name="rw_prep",
    )(k, r, v, apre0, apre1, k_k.reshape(1, D), k_a.reshape(1, D), r_k.reshape(1, D))


def _rw_chunk(r, k, v, kk, wpre, apre, ka, h, reverse):
    C = RW_CHUNK
    P = 2 * C
    rc = _iota2((C, C), 0)
    cc = _iota2((C, C), 1)
    tri = ((rc <= cc) if reverse else (rc >= cc)).astype(BF16)
    t_idx = _iota2((C, P), 0)
    s_idx = _iota2((C, P), 1) % C
    strict = (t_idx < s_idx) if reverse else (t_idx > s_idx)
    incl = (t_idx <= s_idx) if reverse else (t_idx >= s_idx)
    head0 = _iota2((C, LANES), 1) < RW_HEAD
    same_head = (_iota2((LANES, LANES), 0) // RW_HEAD) == (_iota2((LANES, LANES), 1) // RW_HEAD)

    def stack(x):
        return jnp.concatenate([jnp.where(head0, x, 0.0), jnp.where(head0, 0.0, x)], axis=0).astype(BF16)

    lw = -RW_LOG_DECAY_MAX * jax.nn.sigmoid(wpre)
    a = jax.nn.sigmoid(apre)
    kd = k * (1.0 + (a - 1.0) * ka)
    b = kk * a
    cum = _mask_dot(tri, lw)
    yield None
    cum_x = cum - lw
    total = cum[0:1] if reverse else cum[C - 1:C]
    e_neg = jnp.exp(-cum)
    rn = (r * jnp.exp(cum)).astype(BF16)
    xn = (kk * jnp.exp(cum_x)).astype(BF16)
    kn = kd * e_neg
    bn = b * e_neg
    vs = stack(v)
    coef = _dot_nt(jnp.concatenate([xn, rn], axis=0), jnp.concatenate([stack(kn), stack(bn)], axis=0))
    yield None
    a_k = jnp.where(strict, coef[:C, :P], 0.0).astype(BF16)
    pw = jnp.where(strict, -coef[:C, P:], 0.0)
    q_k = jnp.where(incl, coef[C:, :P], 0.0).astype(BF16)
    q_b = jnp.where(incl, coef[C:, P:], 0.0).astype(BF16)
    hv = jnp.concatenate([h.astype(BF16), vs], axis=0)
    x = _dot(jnp.concatenate([xn, a_k], axis=1), hv)
    o_hv = _dot(jnp.concatenate([rn, q_k], axis=1), hv)
    yield None
    for _ in range(int(math.log2(C)) - 1):
        both = _dot(pw.astype(BF16), jnp.concatenate([stack(x), stack(pw)], axis=1))
        x = x + both[:, :P]
        pw = both[:, P:]
        yield None
    u = x + _dot(pw.astype(BF16), stack(x))
    yield None
    o = o_hv - _dot(q_b, stack(u))
    upd = _dot_tn(jnp.concatenate([kn, bn], axis=0).astype(BF16),
                  jnp.concatenate([v, -u], axis=0).astype(BF16))
    gcol = jnp.transpose(jnp.broadcast_to(jnp.exp(total), (LANES, LANES)))
    h_new = gcol * (h + jnp.where(same_head, upd, 0.0))
    yield o, h_new


def _interleave(chains):
    while True:
        outs = [next(ch) for ch in chains]
        if outs[0] is not None:
            return outs


def _rw_scan_body(rf_ref, kf_ref, vf_ref, kkf_ref, wf_ref, af_ref, rb_ref, kb_ref, vb_ref, kkb_ref, wb_ref,
                  ab_ref, ka_ref, of_ref, ob_ref, h_ref, *, n_chunks, n_pairs):
    C = RW_CHUNK

    @pl.when(pl.program_id(2) == 0)
    def _():
        h_ref[...] = jnp.zeros_like(h_ref)

    def chunk(step, carry):
        dirs = ((0, step, False, (rf_ref, kf_ref, vf_ref, kkf_ref, wf_ref, af_ref), of_ref),
                (1, n_chunks - 1 - step, True, (rb_ref, kb_ref, vb_ref, kkb_ref, wb_ref, ab_ref), ob_ref))
        work = []
        for d, c, reverse, refs, o_ref in dirs:
            rows = pl.ds(pl.multiple_of(c * C, C), C)
            for g in range(n_pairs):
                lanes = slice(g * LANES, (g + 1) * LANES)
                vals = [ref[0, rows, lanes].astype(F32) for ref in refs]
                work.append((vals, ka_ref[:, lanes], h_ref[d, g], reverse, o_ref, rows, lanes, d, g))
        done = _interleave([_rw_chunk(*vals, ka, h, reverse) for vals, ka, h, reverse, *_ in work])
        for (o, h_new), (_, _, _, _, o_ref, rows, lanes, d, g) in zip(done, work):
            o_ref[0, rows, lanes] = o
            h_ref[d, g] = h_new
        return carry

    lax.fori_loop(0, n_chunks, chunk, 0)


def rw_scan(r, k, v, kk, wpre, apre, k_a, tl=512, n_pairs=8):
    B, L, D = r.shape
    width = n_pairs * LANES
    nt = L // tl
    fspec = pl.BlockSpec((1, tl, width), lambda b, p, t: (b, t, p))
    bspec = pl.BlockSpec((1, tl, width), lambda b, p, t: (b, nt - 1 - t, p))
    out = jax.ShapeDtypeStruct((B, L, D), F32)
    return pl.pallas_call(
        functools.partial(_rw_scan_body, n_chunks=tl // RW_CHUNK, n_pairs=n_pairs),
        grid=(B, D // width, nt),
        in_specs=[fspec] * 6 + [bspec] * 6 + [pl.BlockSpec((1, width), lambda b, p, t: (0, p))],
        out_specs=[fspec, bspec],
        out_shape=[out, out],
        scratch_shapes=[pltpu.VMEM((2, n_pairs, LANES, LANES), F32)],
        compiler_params=_cparams("parallel", "parallel", "arbitrary"),
        name="rw_scan",
    )(r, k, v, kk, wpre[0], apre[0], r, k, v, kk, wpre[1], apre[1], k_a.reshape(1, D))


def _rw_post_body(of_ref, ob_ref, bon_ref, g_ref, gng_ref, gnb_ref, o_ref):
    ones = _head_ones()
    for j in range(of_ref.shape[1] // LANES):
        sl = slice(j * LANES, (j + 1) * LANES)
        so = of_ref[:, sl] + ob_ref[:, sl]
        mean = _dot_mask(so, ones) * (1.0 / RW_HEAD)
        dlt = so - mean
        var = _dot_mask(dlt * dlt, ones) * (1.0 / RW_HEAD)
        y = dlt * lax.rsqrt(var + RW_GN_EPS) * gng_ref[:, sl] + gnb_ref[:, sl]
        o_ref[:, sl] = ((y + bon_ref[:, sl].astype(F32)) * g_ref[:, sl].astype(F32)).astype(o_ref.dtype)


def rw_post(o_f, o_b, bonus, g, gn_g, gn_b, tm=256):
    T, D = o_f.shape
    tm = min(tm, T)
    spec = pl.BlockSpec((tm, D), lambda i: (i, 0))
    pspec = pl.BlockSpec((1, D), lambda i: (0, 0))
    return pl.pallas_call(
        _rw_post_body,
        grid=(T // tm,),
        in_specs=[spec, spec, spec, spec, pspec, pspec],
        out_specs=spec,
        out_shape=jax.ShapeDtypeStruct((T, D), BF16),
        compiler_params=_cparams("parallel"),
        name="rw_post",
    )(o_f, o_b, bonus, g, gn_g.reshape(1, D), gn_b.reshape(1, D))


def _pad_cols(w, n):
    return jnp.pad(w, ((0, 0), (0, n - w.shape[1])))


def _pad_rows(w, n):
    return jnp.pad(w, ((0, n - w.shape[0]), (0, 0)))


def rwkv7_block(x, gate, g_norm, scale, shift, mu, w_r, w_k, w_v, w0, w1, w2, a0, a1, a2, g1, g2,
                k_k, k_a, r_k, gn_g, gn_b, w_o):
    B, L, D = x.shape
    T = B * L
    xr, xw, xk, xv, xa, xg = [a.reshape(T, D) for a in rw_mix(x, g_norm, scale, shift, mu)]
    r = matmul(xr, w_r.astype(BF16), out_dtype=BF16)
    k = matmul(xk, w_k.astype(BF16), out_dtype=BF16)
    v = matmul(xv, w_v.astype(BF16), out_dtype=BF16)
    w1c = jnp.concatenate([_pad_cols(w1[0], LANES), _pad_cols(w1[1], LANES)], axis=1).astype(BF16)
    a1c = jnp.concatenate([_pad_cols(a1[0], LANES), _pad_cols(a1[1], LANES)], axis=1).astype(BF16)
    wmid = matmul(xw, w1c)
    amid = matmul(xa, a1c)
    gmid = matmul(xg, g1.astype(BF16))
    wpre = [matmul(wmid, _pad_rows(w2[d], LANES).astype(BF16), a_act="tanh", bias=w0[d], a_col_block=d)
            for d in range(2)]
    apre = [matmul(amid, _pad_rows(a2[d], LANES).astype(BF16), bias=a0[d], a_col_block=d, out_dtype=BF16)
            for d in range(2)]
    g = matmul(gmid, g2.astype(BF16), a_act="sigmoid", out_dtype=BF16)
    kk, bonus = rw_prep(k, r, v, apre[0], apre[1], k_k, k_a, r_k.reshape(D))
    sh = lambda a: a.reshape(B, L, D)
    o_f, o_b = rw_scan(sh(r), sh(k), sh(v), sh(kk), [sh(a) for a in wpre], [sh(a) for a in apre], k_a)
    y = rw_post(o_f.reshape(T, D), o_b.reshape(T, D), bonus, g, gn_g, gn_b)
    out = matmul(y, w_o.astype(BF16), res=x.reshape(T, D), gate=gate, rows_per_batch=L)
    return out.reshape(B, L, D)


def ffn_block(x, c, norm_g, ada_w, ada_b, idx, w1, w2):
    B, L, D = x.shape
    T = B * L
    shift, scale, gate = ada_mod(c, ada_w, ada_b, idx)
    h = norm_mod(x, norm_g, scale, shift).reshape(T, D)
    hid = matmul(h, w1.astype(BF16), w_idx=idx, out_dtype=BF16, epi="sqrelu")
    out = matmul(hid, w2.astype(BF16), w_idx=idx, res=x.reshape(T, D), gate=gate, rows_per_batch=L,
                 tm=512, tn=512, cols_outer=True)
    return out.reshape(B, L, D)


def hyena_gla_block(x, gate, h, w_in, w_out, conv_w, conv_b, f_w1, f_b1, f_freq1, f_w2, f_b2, f_freq2,
                    f_w3, f_decay, hy_skip, gla_up, gla_up_b, gla_norm_g):
    B, L, D = x.shape
    T = B * L
    C = conv_w.shape[1] // 3
    n_main = w_in.shape[1] - 2 * GLA_RANK
    w_in = w_in.astype(BF16)
    p = matmul(h, w_in, n_cols=n_main).reshape(B, L, n_main)
    plr = matmul(h, _pad_cols(w_in[:, n_main:], LANES)).reshape(B, L, LANES)
    y_hy = hyena_mixer(p, C, conv_w, conv_b, f_w1, f_b1, f_freq1, f_w2, f_b2, f_freq2, f_w3, f_decay, hy_skip)
    y_gla = gla_mixer(p, plr, 3 * C // LANES, gla_up, gla_up_b, gla_norm_g)
    out = matmul(y_hy.reshape(T, -1), w_out.astype(BF16), a2=y_gla.reshape(T, -1), res=x.reshape(T, D),
                 gate=gate, rows_per_batch=L)
    return out.reshape(B, L, D)


def kernel(x, c, mix_norm_g, mix_ada_w, mix_ada_b, ab_w_in, ab_w_out, hy_conv_w, hy_conv_b, hy_ffn_w1, hy_ffn_b1, hy_freq1, hy_ffn_w2, hy_ffn_b2, hy_freq2, hy_ffn_w3, hy_decay, hy_skip, gla_up, gla_up_b, gla_norm_g, tm_norm_g, tm_ada_w, tm_ada_b, rw_mu, rw_w_r, rw_w_k, rw_w_v, rw_w0, rw_w1, rw_w2, rw_a0, rw_a1, rw_a2, rw_g1, rw_g2, rw_k_k, rw_k_a, rw_r_k, rw_gn_g, rw_gn_b, rw_w_o, ffn_norm_g, ffn_ada_w, ffn_ada_b, ffn_w1, ffn_w2, final_norm_g):
    B, L, D = x.shape
    depth = ffn_w1.shape[0]
    for layer in range(depth):
        i = layer // 2
        if layer % 2 == 0:
            shift, scale, gate = ada_mod(c, mix_ada_w, mix_ada_b, i)
            h = norm_mod(x, mix_norm_g[i], scale, shift).reshape(B * L, D)
            x = hyena_gla_block(x, gate, h, ab_w_in[i], ab_w_out[i], hy_conv_w[i], hy_conv_b[i],
                                hy_ffn_w1[i], hy_ffn_b1[i], hy_freq1[i], hy_ffn_w2[i], hy_ffn_b2[i],
                                hy_freq2[i], hy_ffn_w3[i], hy_decay[i], hy_skip[i],
                                gla_up[i], gla_up_b[i], gla_norm_g[i])
        else:
            shift, scale, gate = ada_mod(c, tm_ada_w, tm_ada_b, i)
            x = rwkv7_block(x, gate, tm_norm_g[i], scale, shift, rw_mu[i], rw_w_r[i], rw_w_k[i], rw_w_v[i],
                            rw_w0[i], rw_w1[i], rw_w2[i], rw_a0[i], rw_a1[i], rw_a2[i], rw_g1[i], rw_g2[i],
                            rw_k_k[i], rw_k_a[i], rw_r_k[i], rw_gn_g[i], rw_gn_b[i], rw_w_o[i])
        x = ffn_block(x, c, ffn_norm_g[layer], ffn_ada_w, ffn_ada_b, layer, ffn_w1, ffn_w2)
    return rms_final(x, final_norm_g)
```
